```python
import math
import jax
import jax.numpy as jnp
from jax import lax
import numpy as np

D_MODEL = 1024
BATCH = 4
SEQ = 8192
DEPTH = 4

GRID_W = 64
CTX_LEN = 256
HEAD_DIM = 64
EPS = 1e-6

SSD_HEAD_DIM = 64
D_SSD = D_MODEL
SSD_HEADS = D_SSD // SSD_HEAD_DIM
SSD_GROUPS = 4
SSD_STATE = 128
SSD_CONV_W = 5
SSD_CHUNK = 128
SSD_CONV_DIM = D_SSD + 2 * SSD_GROUPS * SSD_STATE
D_SC = D_MODEL
SC_CONV_W = 3
EVEN_IN = D_SSD + SSD_CONV_DIM + 2 * SSD_HEADS + 3 * D_SC
EVEN_MIX = D_SSD + D_SC

NA_HEADS = D_MODEL // (2 * HEAD_DIM)
NA_ROWS = 8
NA_COLS = 16
D_NA = NA_HEADS * HEAD_DIM
DF_HEADS = D_MODEL // (4 * HEAD_DIM)
DF_V_DIM = 2 * HEAD_DIM
D_DF = DF_HEADS * DF_V_DIM
ODD_IN = 3 * D_NA + 2 * DF_HEADS * 2 * HEAD_DIM + D_DF
ODD_MIX = D_NA + D_DF
Q_BLOCK = 128
ROPE_BASE = 10000.0

N_EXPERTS = 32
TOP_K = 4
D_EXPERT = D_MODEL
SWIGLU_LIMIT = 7.0
SWIGLU_ALPHA = 1.702
MOE_BLOCK = 128

kernel_name = "hybrid_ssd_shortconv_natten_diffattn_moe_dit"

F32 = jnp.float32


def rms_norm(x, g):
    xf = x.astype(F32)
    y = xf * lax.rsqrt(jnp.mean(xf * xf, axis=-1, keepdims=True) + EPS)
    return y.astype(x.dtype) * g


def modulate(h, shift, scale):
    return h * (1.0 + scale) + shift


def dw_conv(x, w):
    width = w.shape[0]
    return lax.conv_general_dilated(
        x, w[:, None, :].astype(x.dtype), window_strides=(1,),
        padding=[(width // 2, width // 2)],
        dimension_numbers=("NWC", "WIO", "NWC"),
        feature_group_count=x.shape[-1])


def axial_rope(n_tok):
    pos = jnp.arange(n_tok)
    rows = (pos // GRID_W).astype(F32)
    cols = (pos % GRID_W).astype(F32)
    quarter = HEAD_DIM // 4
    inv_freq = ROPE_BASE ** (-jnp.arange(quarter, dtype=F32) / quarter)
    ar = rows[:, None] * inv_freq
    ac = cols[:, None] * inv_freq
    ang = jnp.concatenate([ar, ar, ac, ac], axis=-1)
    return jnp.cos(ang), jnp.sin(ang)


def apply_rope(x, cos, sin):
    x1, x2, x3, x4 = jnp.split(x, 4, axis=-1)
    rot = jnp.concatenate([-x2, x1, -x4, x3], axis=-1)
    return (x * cos + rot * sin).astype(x.dtype)


def ssd_scan(x, dt, a_coef, bmat, cmat, state0):
    b, l, h, p = x.shape
    g, n = bmat.shape[2], bmat.shape[3]
    k = h // g
    nc = l // SSD_CHUNK
    xs = (x * dt[..., None]).reshape(b, nc, SSD_CHUNK, g, k, p)
    a = (dt * a_coef).astype(F32).reshape(b, nc, SSD_CHUNK, g, k)
    a_cum = jnp.cumsum(jnp.moveaxis(a, 2, -1), axis=-1)
    bc = bmat.reshape(b, nc, SSD_CHUNK, g, n)
    cc = cmat.reshape(b, nc, SSD_CHUNK, g, n)
    seg = a_cum[..., :, None] - a_cum[..., None, :]
    lower = jnp.tril(jnp.ones((SSD_CHUNK, SSD_CHUNK), dtype=bool))
    decay_in = jnp.exp(jnp.where(lower, seg, -jnp.inf))
    cb = jnp.einsum("bclgn,bcsgn->bcgls", cc, bc)
    y_diag = jnp.einsum("bcgls,bcgkls,bcsgkp->bclgkp", cb, decay_in, xs)
    decay_out = jnp.exp(a_cum[..., -1:] - a_cum)
    chunk_states = jnp.einsum("bcsgn,bcgks,bcsgkp->bcgkpn", bc, decay_out, xs)
    chunk_decay = jnp.exp(a_cum[..., -1])

    def carry(state, inp):
        st, dec = inp
        return state * dec[..., None, None] + st, state

    final, starts = lax.scan(carry, state0,
                             (jnp.moveaxis(chunk_states, 1, 0), jnp.moveaxis(chunk_decay, 1, 0)))
    y_off = jnp.einsum("bclgn,cbgkpn,bcgkl->bclgkp", cc, starts, jnp.exp(a_cum))
    return (y_diag + y_off).reshape(b, l, h, p), final


def even_mixer(h_lat, h_ctx, w_in, w_out, conv_w, conv_b, a_log, dt_bias, d_skip, norm_g, sc_conv_w, with_ctx_out):
    a_neg = -jnp.exp(a_log.astype(F32))
    splits = [D_SSD, D_SSD + SSD_CONV_DIM, D_SSD + SSD_CONV_DIM + 2 * SSD_HEADS,
              D_SSD + SSD_CONV_DIM + 2 * SSD_HEADS + D_SC, D_SSD + SSD_CONV_DIM + 2 * SSD_HEADS + 2 * D_SC]

    def project(h):
        b, l, _ = h.shape
        z, xbc, dt, gb, gc, gh = jnp.split(h @ w_in, splits, axis=-1)
        xbc = jax.nn.silu(dw_conv(xbc, conv_w) + conv_b)
        xs, bm, cm = jnp.split(xbc, [D_SSD, D_SSD + SSD_GROUPS * SSD_STATE], axis=-1)
        xs = xs.reshape(b, l, SSD_HEADS, SSD_HEAD_DIM)
        bm = bm.reshape(b, l, SSD_GROUPS, SSD_STATE)
        cm = cm.reshape(b, l, SSD_GROUPS, SSD_STATE)
        dt = jax.nn.softplus(dt.astype(F32).reshape(b, l, 2, SSD_HEADS) + dt_bias)
        sc = gb * dw_conv(gc * gh, sc_conv_w)
        return z, xs, bm, cm, dt, sc

    def flip(t):
        return jnp.flip(t, axis=1)

    def scan_both(xs, bm, cm, dt, s_fwd, s_bwd):
        yf, sf = ssd_scan(xs, dt[:, :, 0], a_neg[0], bm, cm, s_fwd)
        yb, sb = ssd_scan(flip(xs), flip(dt[:, :, 1]), a_neg[1], flip(bm), flip(cm), s_bwd)
        return yf + flip(yb), sf, sb

    def output(z, xs, y, sc):
        b, l = z.shape[:2]
        y = (y + d_skip[:, None] * xs).reshape(b, l, D_SSD)
        y = rms_norm(y * jax.nn.silu(z), norm_g)
        return jnp.concatenate([y.astype(sc.dtype), sc], axis=-1) @ w_out

    zc, xc, bc, cc, dtc, scc = project(h_ctx)
    zero = jnp.zeros((h_ctx.shape[0], SSD_GROUPS, SSD_HEADS // SSD_GROUPS, SSD_HEAD_DIM, SSD_STATE), F32)
    yc, s_f, s_b = scan_both(xc, bc, cc, dtc, zero, zero)
    zl, xl, bl, cl, dtl, scl = project(h_lat)
    yl, _, _ = scan_both(xl, bl, cl, dtl, s_f, s_b)
    out_lat = output(zl, xl, yl, scl)
    out_ctx = output(zc, xc, yc, scc) if with_ctx_out else None
    return out_lat, out_ctx


def dense_attention(q, k, v):
    s = jnp.einsum("bqhd,bkhd->bhqk", q, k).astype(F32) * q.shape[-1] ** -0.5
    p = jax.nn.softmax(s, axis=-1).astype(v.dtype)
    return jnp.einsum("bhqk,bkhd->bqhd", p, v)


def diff_attention(q, k, v, lam):
    s = jnp.einsum("bqhid,bkhid->bhiqk", q, k).astype(F32) * HEAD_DIM ** -0.5
    a = jax.nn.softmax(s, axis=-1)
    w = (a[:, :, 0] - lam * a[:, :, 1]).astype(v.dtype)
    return jnp.einsum("bhqk,bkhe->bqhe", w, v)


def neighbourhood_attention(q, k, v, k_ctx, v_ctx, rel_bias):
    b, s, nh, hd = q.shape
    rows = s // GRID_W
    kr = min(NA_ROWS, rows)
    kg = k.reshape(b, rows, GRID_W, nh, hd)
    vg = v.reshape(b, rows, GRID_W, nh, hd)
    qg = jnp.swapaxes(q.reshape(b, rows, GRID_W, nh, hd), 0, 1)
    cols = jnp.arange(GRID_W)
    col_start = jnp.clip(cols - NA_COLS // 2, 0, GRID_W - NA_COLS)
    col_idx = col_start[:, None] + jnp.arange(NA_COLS)[None, :]
    rel_c = rel_bias[:, :, col_idx - cols[:, None] + NA_COLS - 1]
    scale = hd ** -0.5
    n_nb = kr * NA_COLS

    def one_row(args):
        r, q_row = args
        r_start = jnp.clip(r - kr // 2, 0, rows - kr)
        k_win = lax.dynamic_slice_in_dim(kg, r_start, kr, axis=1)[:, :, col_idx]
        v_win = lax.dynamic_slice_in_dim(vg, r_start, kr, axis=1)[:, :, col_idx]
        bias = jnp.swapaxes(rel_c[:, r_start + jnp.arange(kr) - r + NA_ROWS - 1], 1, 2)
        s_nb = jnp.einsum("bqhd,brqchd->bhqrc", q_row, k_win).astype(F32) * scale + bias
        s_ctx = jnp.einsum("bqhd,blhd->bhql", q_row, k_ctx).astype(F32) * scale
        p = jax.nn.softmax(jnp.concatenate([s_nb.reshape(b, nh, GRID_W, n_nb), s_ctx], axis=-1), axis=-1)
        p = p.astype(v.dtype)
        o = jnp.einsum("bhqrc,brqchd->bqhd", p[..., :n_nb].reshape(b, nh, GRID_W, kr, NA_COLS), v_win)
        return o + jnp.einsum("bhql,blhd->bqhd", p[..., n_nb:], v_ctx)

    out = lax.map(one_row, (jnp.arange(rows), qg))
    return jnp.swapaxes(out, 0, 1).reshape(b, s, nh, hd)


def odd_mixer(h_lat, h_ctx, w_in, w_out, na_qk_g, na_rel_bias, df_qk_g, df_lambda, df_subln_g,
              lambda_init, rope_cos, rope_sin, with_ctx_out):
    dq_w = DF_HEADS * 2 * HEAD_DIM
    splits = [D_NA, 2 * D_NA, 3 * D_NA, 3 * D_NA + dq_w, 3 * D_NA + 2 * dq_w]

    def project(h, rotate):
        b, l, _ = h.shape
        nq, nk, nv, dq, dk, dv = jnp.split(h @ w_in, splits, axis=-1)
        nq = rms_norm(nq.reshape(b, l, NA_HEADS, HEAD_DIM), na_qk_g[0])
        nk = rms_norm(nk.reshape(b, l, NA_HEADS, HEAD_DIM), na_qk_g[1])
        nv = nv.reshape(b, l, NA_HEADS, HEAD_DIM)
        dq = rms_norm(dq.reshape(b, l, DF_HEADS, 2, HEAD_DIM), df_qk_g[0])
        dk = rms_norm(dk.reshape(b, l, DF_HEADS, 2, HEAD_DIM), df_qk_g[1])
        dv = dv.reshape(b, l, DF_HEADS, DF_V_DIM)
        if rotate:
            cos, sin = rope_cos[:, None, None, :], rope_sin[:, None, None, :]
            dq = apply_rope(dq, cos, sin)
            dk = apply_rope(dk, cos, sin)
        return nq, nk, nv, dq, dk, dv

    lam = (jnp.exp(jnp.sum(df_lambda[0] * df_lambda[1]).astype(F32))
           - jnp.exp(jnp.sum(df_lambda[2] * df_lambda[3]).astype(F32)) + lambda_init)

    def merge(na_o, df_o):
        b, l = na_o.shape[:2]
        df_o = rms_norm(df_o, df_subln_g) * (1.0 - lambda_init)
        return jnp.concatenate([na_o.reshape(b, l, D_NA), df_o.reshape(b, l, D_DF).astype(na_o.dtype)], axis=-1) @ w_out

    cnq, cnk, cnv, cdq, cdk, cdv = project(h_ctx, False)
    nq, nk, nv, dq, dk, dv = project(h_lat, True)
    b, s = h_lat.shape[:2]
    na_o = neighbourhood_attention(nq, nk, nv, cnk, cnv, na_rel_bias)
    k_all = jnp.concatenate([cdk, dk], axis=1)
    v_all = jnp.concatenate([cdv, dv], axis=1)
    q_blocks = jnp.swapaxes(dq.reshape(b, s // Q_BLOCK, Q_BLOCK, DF_HEADS, 2, HEAD_DIM), 0, 1)
    df_o = lax.map(lambda qb: diff_attention(qb, k_all, v_all, lam), q_blocks)
    df_o = jnp.swapaxes(df_o, 0, 1).reshape(b, s, DF_HEADS, DF_V_DIM)
    out_lat = merge(na_o, df_o)
    out_ctx = merge(dense_attention(cnq, cnk, cnv), diff_attention(cdq, cdk, cdv, lam)) if with_ctx_out else None
    return out_lat, out_ctx


def moe_ffn(h, router_w, router_b, w_gu, b_gu, w_dn, b_dn):
    n_tok, d = h.shape
    logits = (h @ router_w + router_b).astype(F32)
    top_logit, top_idx = lax.top_k(logits, TOP_K)
    gates = jax.nn.softmax(top_logit, axis=-1)
    n_assign = n_tok * TOP_K
    flat_e = top_idx.reshape(-1)
    order = jnp.argsort(flat_e)
    e_sorted = flat_e[order]
    tok_sorted = (order // TOP_K).astype(jnp.int32)
    gate_sorted = gates.reshape(-1)[order]
    counts = jnp.bincount(flat_e, length=N_EXPERTS)
    padded = (counts + MOE_BLOCK - 1) // MOE_BLOCK * MOE_BLOCK
    pad_end = jnp.cumsum(padded)
    pad_start = pad_end - padded
    grp_start = jnp.cumsum(counts) - counts
    dest = pad_start[e_sorted] + jnp.arange(n_assign) - grp_start[e_sorted]
    n_blocks = -(-(n_assign + N_EXPERTS * (MOE_BLOCK - 1)) // MOE_BLOCK)
    cap = n_blocks * MOE_BLOCK
    slot_tok = jnp.full((cap,), n_tok, jnp.int32).at[dest].set(tok_sorted)
    slot_gate = jnp.zeros((cap,), F32).at[dest].set(gate_sorted)
    block_expert = jnp.minimum(jnp.searchsorted(pad_end, jnp.arange(n_blocks) * MOE_BLOCK, side="right"),
                               N_EXPERTS - 1)
    h_pad = jnp.concatenate([h, jnp.zeros((1, d), h.dtype)], axis=0)

    def run_block(args):
        toks, e = args
        gu = h_pad[toks] @ w_gu[e] + b_gu[e]
        gate = jnp.minimum(gu[..., ::2], SWIGLU_LIMIT)
        up = jnp.clip(gu[..., 1::2], -SWIGLU_LIMIT, SWIGLU_LIMIT)
        act = gate * jax.nn.sigmoid(SWIGLU_ALPHA * gate) * (up + 1.0)
        return act @ w_dn[e] + b_dn[e]

    y = lax.map(run_block, (slot_tok.reshape(n_blocks, MOE_BLOCK), block_expert))
    y = y.reshape(cap, d) * slot_gate[:, None].astype(y.dtype)
    return jax.ops.segment_sum(y, slot_tok, num_segments=n_tok + 1)[:n_tok]


def setup_inputs(seed: int = 0) -> dict:
    key = jax.random.key(seed)
    ks = iter(jax.random.split(key, 40))
    d = D_MODEL
    ne, no = (DEPTH + 1) // 2, DEPTH // 2

    def nrm(shape, s):
        return jax.random.normal(next(ks), shape, F32) * s

    def gain(shape):
        return 1.0 + nrm(shape, 0.05)

    dt0 = jnp.exp(jax.random.uniform(next(ks), (ne, 2, SSD_HEADS), F32, math.log(1e-3), math.log(1e-1)))
    a_init = jax.random.uniform(next(ks), (ne, 2, SSD_HEADS), F32, 1.0, 16.0)
    return {
        "x": nrm((BATCH, SEQ, d), 1.0),
        "c": nrm((BATCH, d), 1.0),
        "ctx": nrm((BATCH, CTX_LEN, d), 1.0),
        "c_ctx": nrm((d,), 1.0),
        "ada_w": nrm((DEPTH, d, 6 * d), 0.5 * d ** -0.5),
        "ada_b": nrm((DEPTH, 6 * d), 0.02),
        "mix_norm_g": gain((DEPTH, d)),
        "ffn_norm_g": gain((DEPTH, d)),
        "router_w": nrm((DEPTH, d, N_EXPERTS), d ** -0.5),
        "router_b": nrm((DEPTH, N_EXPERTS), 0.01),
        "moe_w_gu": nrm((DEPTH, N_EXPERTS, d, 2 * D_EXPERT), d ** -0.5),
        "moe_b_gu": nrm((DEPTH, N_EXPERTS, 2 * D_EXPERT), 0.02),
        "moe_w_down": nrm((DEPTH, N_EXPERTS, D_EXPERT, d), D_EXPERT ** -0.5),
        "moe_b_down": nrm((DEPTH, N_EXPERTS, d), 0.02),
        "ev_w_in": nrm((ne, d, EVEN_IN), d ** -0.5),
        "ev_w_out": nrm((ne, EVEN_MIX, d), EVEN_MIX ** -0.5),
        "ssd_conv_w": nrm((ne, SSD_CONV_W, SSD_CONV_DIM), SSD_CONV_W ** -0.5),
        "ssd_conv_b": nrm((ne, SSD_CONV_DIM), 0.02),
        "ssd_a_log": jnp.log(a_init),
        "ssd_dt_bias": dt0 + jnp.log(-jnp.expm1(-dt0)),
        "ssd_d": gain((ne, SSD_HEADS)),
        "ssd_norm_g": gain((ne, D_SSD)),
        "sc_conv_w": nrm((ne, SC_CONV_W, D_SC), SC_CONV_W ** -0.5),
        "od_w_in": nrm((no, d, ODD_IN), d ** -0.5),
        "od_w_out": nrm((no, ODD_MIX, d), ODD_MIX ** -0.5),
        "na_qk_g": gain((no, 2, HEAD_DIM)),
        "na_rel_bias": nrm((no, NA_HEADS, 2 * NA_ROWS - 1, 2 * NA_COLS - 1), 0.1),
        "df_qk_g": gain((no, 2, HEAD_DIM)),
        "df_lambda": nrm((no, 4, HEAD_DIM), 0.1),
        "df_subln_g": gain((no, DF_V_DIM)),
    }


def reference(x, c, ctx, c_ctx, ada_w, ada_b, mix_norm_g, ffn_norm_g, router_w, router_b,
              moe_w_gu, moe_b_gu, moe_w_down, moe_b_down,
              ev_w_in, ev_w_out, ssd_conv_w, ssd_conv_b, ssd_a_log, ssd_dt_bias, ssd_d, ssd_norm_g, sc_conv_w,
              od_w_in, od_w_out, na_qk_g, na_rel_bias, df_qk_g, df_lambda, df_subln_g):
    b, s, d = x.shape
    n_ctx = ctx.shape[1]
    rope_cos, rope_sin = axial_rope(s)
    cond = jax.nn.silu(c)
    cond_ctx = jax.nn.silu(c_ctx)
    for i in range(DEPTH):
        last = i == DEPTH - 1
        j = i // 2
        sh_m, sc_m, g_m, sh_f, sc_f, g_f = jnp.split((cond @ ada_w[i] + ada_b[i])[:, None, :], 6, axis=-1)
        csh_m, csc_m, cg_m, csh_f, csc_f, cg_f = jnp.split(cond_ctx @ ada_w[i] + ada_b[i], 6, axis=-1)
        h = modulate(rms_norm(x, mix_norm_g[i]), sh_m, sc_m)
        hc = modulate(rms_norm(ctx, mix_norm_g[i]), csh_m, csc_m)
        if i % 2 == 0:
            y, yc = even_mixer(h, hc, ev_w_in[j], ev_w_out[j], ssd_conv_w[j], ssd_conv_b[j], ssd_a_log[j],
                               ssd_dt_bias[j], ssd_d[j], ssd_norm_g[j], sc_conv_w[j], not last)
        else:
            lambda_init = 0.8 - 0.6 * math.exp(-0.3 * i)
            y, yc = odd_mixer(h, hc, od_w_in[j], od_w_out[j], na_qk_g[j], na_rel_bias[j], df_qk_g[j],
                              df_lambda[j], df_subln_g[j], lambda_init, rope_cos, rope_sin, not last)
        x = x + g_m * y
        h = modulate(rms_norm(x, ffn_norm_g[i]), sh_f, sc_f).reshape(b * s, d)
        moe_args = (router_w[i], router_b[i], moe_w_gu[i], moe_b_gu[i], moe_w_down[i], moe_b_down[i])
        if last:
            x = x + g_f * moe_ffn(h, *moe_args).reshape(b, s, d)
        else:
            ctx = ctx + cg_m * yc
            hc = modulate(rms_norm(ctx, ffn_norm_g[i]), csh_f, csc_f).reshape(b * n_ctx, d)
            f = moe_ffn(jnp.concatenate([h, hc], axis=0), *moe_args)
            x = x + g_f * f[:b * s].reshape(b, s, d)
            ctx = ctx + cg_f * f[b * s:].reshape(b, n_ctx, d)
    return x
```

```python
import functools
import math

import jax
import jax.numpy as jnp
from jax import lax
from jax.experimental import pallas as pl
from jax.experimental.pallas import tpu as pltpu

F32 = jnp.float32
BF16 = jnp.bfloat16

GRID_W = 64
HEAD_DIM = 64
EPS = 1e-6
SSD_HEADS = 16
SSD_GROUPS = 4
SSD_STATE = 128
NA_ROWS = 8
NA_COLS = 16
NA_HEADS = 8
DF_HEADS = 4
N_EXPERTS = 32
TOP_K = 4
SWIGLU_LIMIT = 7.0
SWIGLU_ALPHA = 1.702
ROPE_BASE = 10000.0

LANES = 128
SSD_Q = 128
MOE_TM = 512
ROUTER_TM = 256
VMEM_LIMIT = 56 * 1024 * 1024
NEG = -1e30


def _cparams(sem):
    return pltpu.CompilerParams(dimension_semantics=sem, vmem_limit_bytes=VMEM_LIMIT)


def _mm_kernel(a_ref, w_ref, o_ref):
    o_ref[...] = jnp.dot(a_ref[...], w_ref[...], preferred_element_type=F32).astype(o_ref.dtype)


def matmul(a, w, tm, tn, out_dtype=F32):
    m, k = a.shape
    n = w.shape[1]
    return pl.pallas_call(
        _mm_kernel,
        grid=(n // tn, m // tm),
        in_specs=[pl.BlockSpec((tm, k), lambda j, i: (i, 0)),
                  pl.BlockSpec((k, tn), lambda j, i: (0, j))],
        out_specs=pl.BlockSpec((tm, tn), lambda j, i: (i, j)),
        out_shape=jax.ShapeDtypeStruct((m, n), out_dtype),
        compiler_params=_cparams(("parallel", "parallel")),
        name="matmul",
    )(a, w)


def _split3_dot(tri, a, dims):
    a1 = a.astype(BF16)
    r1 = a - a1.astype(F32)
    a2 = r1.astype(BF16)
    a3 = (r1 - a2.astype(F32)).astype(BF16)
    out = None
    for piece in (a1, a2, a3):
        if dims == "tri_a":
            t = jnp.dot(tri, piece, preferred_element_type=F32)
        else:
            t = jnp.dot(piece, tri, preferred_element_type=F32)
        out = t if out is None else out + t
    return out


def _ssd_kernel(x_ref, bt_ref, b_ref, c_ref, da_ref, dat_ref, y_ref, st_ref, *, rev):
    q = SSD_Q
    step = pl.program_id(1)

    @pl.when(step == 0)
    def _():
        st_ref[...] = jnp.zeros_like(st_ref)

    da = da_ref[0, 0]
    dat = dat_ref[0, 0]
    a_c = da[:, SSD_HEADS:2 * SSD_HEADS]
    dt_r = dat[0:SSD_HEADS]
    a_r = dat[SSD_HEADS:2 * SSD_HEADS]
    ri = lax.broadcasted_iota(jnp.int32, (q, q), 0)
    ci = lax.broadcasted_iota(jnp.int32, (q, q), 1)
    tri = (ci <= ri).astype(BF16)
    tri_t = (ri <= ci).astype(BF16)
    cum_c = _split3_dot(tri, a_c, "tri_a")
    cum_r = _split3_dot(tri_t, a_r, "a_tri")
    tot_r = cum_r[:, q - 1:q]
    if rev:
        pos_c = cum_c - a_c
        pos_r = cum_r - a_r
        mask = ci >= ri
    else:
        pos_c = cum_c
        pos_r = cum_r
        mask = ri >= ci
    lane = lax.broadcasted_iota(jnp.int32, (q, LANES), 1)
    lane_n = lax.broadcasted_iota(jnp.int32, (SSD_STATE, LANES), 1)
    heads_per_group = SSD_HEADS // SSD_GROUPS

    for g in range(SSD_GROUPS):
        gs = slice(g * SSD_STATE, (g + 1) * SSD_STATE)
        bg = b_ref[0, :, gs]
        cg = c_ref[0, :, gs]
        btg = bt_ref[0, gs, :].astype(F32)
        cb = lax.dot_general(cg, bg, (((1,), (1,)), ((), ())), preferred_element_type=F32)
        cg32 = cg.astype(F32)
        for pp in range(heads_per_group // 2):
            p = g * (heads_per_group // 2) + pp
            xp = x_ref[0, :, p * LANES:(p + 1) * LANES]
            st = st_ref[p]
            rhs = jnp.concatenate([xp, st.astype(BF16)], axis=0)
            ys, ds, decs = [], [], []
            for hh in range(2):
                h = 2 * p + hh
                colb = jnp.broadcast_to(pos_c[:, h:h + 1], (q, LANES))
                row = pos_r[h:h + 1, :]
                dt_row = dt_r[h:h + 1, :]
                tot = tot_r[h:h + 1, :]
                if rev:
                    seg = row - colb
                    coff = jnp.exp(tot - colb)
                    w_row = dt_row * jnp.exp(row)
                else:
                    seg = colb - row
                    coff = jnp.exp(colb)
                    w_row = dt_row * jnp.exp(tot - row)
                decay = jnp.exp(jnp.where(mask, seg, NEG))
                m_h = (cb * decay * dt_row).astype(BF16)
                c_h = (cg32 * coff).astype(BF16)
                lhs = jnp.concatenate([m_h, c_h], axis=1)
                ys.append(jnp.dot(lhs, rhs, preferred_element_type=F32))
                btw = (btg * w_row).astype(BF16)
                ds.append(jnp.dot(btw, xp, preferred_element_type=F32))
                decs.append(jnp.exp(tot))
            y_ref[0, :, p * LANES:(p + 1) * LANES] = jnp.where(lane < HEAD_DIM, ys[0], ys[1])
            st_ref[p] = jnp.where(lane_n < HEAD_DIM, decs[0] * st + ds[0], decs[1] * st + ds[1])


def ssd_scan(xs, bt, bm, cm, da, dat, n_lat, rev):
    b, l, d = xs.shape
    nch = l // SSD_Q
    nlat = n_lat // SSD_Q
    nctx = nch - nlat
    d_idx = 1 if rev else 0
    if rev:
        def chunk(j):
            return nch - 1 - j
    else:
        def chunk(j):
            return jnp.where(j < nctx, nlat + j, j - nctx)
    gn = SSD_GROUPS * SSD_STATE
    return pl.pallas_call(
        functools.partial(_ssd_kernel, rev=rev),
        grid=(b, nch),
        in_specs=[pl.BlockSpec((1, SSD_Q, d), lambda i, j: (i, chunk(j), 0)),
                  pl.BlockSpec((1, gn, SSD_Q), lambda i, j: (i, 0, chunk(j))),
                  pl.BlockSpec((1, SSD_Q, gn), lambda i, j: (i, chunk(j), 0)),
                  pl.BlockSpec((1, SSD_Q, gn), lambda i, j: (i, chunk(j), 0)),
                  pl.BlockSpec((1, 1, SSD_Q, 2 * SSD_HEADS), lambda i, j: (i, d_idx, chunk(j), 0)),
                  pl.BlockSpec((1, 1, 2 * SSD_HEADS, SSD_Q), lambda i, j: (i, d_idx, 0, chunk(j)))],
        out_specs=pl.BlockSpec((1, SSD_Q, d), lambda i, j: (i, chunk(j), 0)),
        out_shape=jax.ShapeDtypeStruct((b, l, d), F32),
        scratch_shapes=[pltpu.VMEM((SSD_HEADS // 2, SSD_STATE, LANES), F32)],
        compiler_params=_cparams(("parallel", "arbitrary")),
        name="ssd_bwd" if rev else "ssd_fwd",
    )(xs, bt, bm, cm, da, dat)


def _stack_halves(qv):
    lane = lax.broadcasted_iota(jnp.int32, qv.shape, 1)
    zero = jnp.zeros_like(qv)
    q1 = jnp.where(lane < HEAD_DIM, qv, zero)
    q2 = jnp.where(lane >= HEAD_DIM, qv, zero)
    return jnp.concatenate([q1, q2], axis=0) * jnp.asarray(HEAD_DIM ** -0.5, qv.dtype)


def _flash_kernel(lam_ref, q_ref, k_ref, v_ref, o_ref, m_ref, l_ref, acc_ref, *, mode, tk, lambda_init):
    tq = q_ref.shape[1]
    nk = k_ref.shape[1] // tk
    qs = _stack_halves(q_ref[0])
    m_ref[...] = jnp.full_like(m_ref, -jnp.inf)
    l_ref[...] = jnp.zeros_like(l_ref)
    acc_ref[...] = jnp.zeros_like(acc_ref)

    def body(c, carry):
        start = pl.multiple_of(c * tk, tk)
        kc = k_ref[0, pl.ds(start, tk), :]
        vc = v_ref[0, pl.ds(start, tk), :]
        s = lax.dot_general(qs, kc, (((1,), (1,)), ((), ())), preferred_element_type=F32)
        m_old = m_ref[...]
        m_new = jnp.maximum(m_old, jnp.max(s, axis=-1, keepdims=True))
        alpha = jnp.exp(m_old - m_new)
        p = jnp.exp(s - m_new)
        l_ref[...] = alpha * l_ref[...] + jnp.sum(p, axis=-1, keepdims=True)
        acc_ref[...] = alpha * acc_ref[...] + jnp.dot(p.astype(BF16), vc, preferred_element_type=F32)
        m_ref[...] = m_new
        return carry

    lax.fori_loop(0, nk, body, 0)
    o = acc_ref[...] / l_ref[...]
    o1, o2 = o[:tq], o[tq:]
    if mode == "diff":
        lp = lam_ref[...]
        s01 = jnp.sum(lp[0:1] * lp[1:2], axis=-1, keepdims=True)
        s23 = jnp.sum(lp[2:3] * lp[3:4], axis=-1, keepdims=True)
        lam = jnp.exp(s01) - jnp.exp(s23) + lambda_init
        out = o1 - lam * o2
    else:
        lane = lax.broadcasted_iota(jnp.int32, o1.shape, 1)
        out = jnp.where(lane < HEAD_DIM, o1, o2)
    o_ref[0] = out.astype(o_ref.dtype)


def flash_slabs(q, k, v, lam_p, mode, tq, tk, lambda_init=0.0):
    b, lq, w = q.shape
    lk = k.shape[1]
    ns = w // LANES
    return pl.pallas_call(
        functools.partial(_flash_kernel, mode=mode, tk=tk, lambda_init=lambda_init),
        grid=(b, ns, lq // tq),
        in_specs=[pl.BlockSpec(lam_p.shape, lambda i, h, j: (0, 0)),
                  pl.BlockSpec((1, tq, LANES), lambda i, h, j: (i, j, h)),
                  pl.BlockSpec((1, lk, LANES), lambda i, h, j: (i, 0, h)),
                  pl.BlockSpec((1, lk, LANES), lambda i, h, j: (i, 0, h))],
        out_specs=pl.BlockSpec((1, tq, LANES), lambda i, h, j: (i, j, h)),
        out_shape=jax.ShapeDtypeStruct((b, lq, w), F32),
        scratch_shapes=[pltpu.VMEM((2 * tq, 1), F32), pltpu.VMEM((2 * tq, 1), F32),
                        pltpu.VMEM((2 * tq, LANES), F32)],
        compiler_params=_cparams(("parallel", "parallel", "arbitrary")),
        name="flash_" + mode,
    )(lam_p, q, k, v)


NA_RB = 8
NA_BLK = NA_RB * GRID_W
NA_WIN = NA_ROWS * GRID_W


def _na_kernel(q_ref, kp_ref, kc_ref, kn_ref, vp_ref, vc_ref, vn_ref, kx_ref, vx_ref, bias_ref, o_ref,
               kbuf, vbuf, *, rows):
    rb = pl.program_id(1)
    kbuf[0:NA_BLK] = kp_ref[0]
    kbuf[NA_BLK:2 * NA_BLK] = kc_ref[0]
    kbuf[2 * NA_BLK:3 * NA_BLK] = kn_ref[0]
    vbuf[0:NA_BLK] = vp_ref[0]
    vbuf[NA_BLK:2 * NA_BLK] = vc_ref[0]
    vbuf[2 * NA_BLK:3 * NA_BLK] = vn_ref[0]
    npairs = q_ref.shape[2] // LANES
    lane = lax.broadcasted_iota(jnp.int32, (GRID_W, LANES), 1)

    def row_body(rl, carry):
        r = rb * NA_RB + rl
        r_start = jnp.clip(r - NA_ROWS // 2, 0, rows - NA_ROWS)
        off = r_start - (rb * NA_RB - NA_RB)
        di0 = r_start - r + NA_ROWS - 1
        tok0 = pl.multiple_of(off * GRID_W, GRID_W)
        q0 = pl.multiple_of(rl * GRID_W, GRID_W)
        for p in range(npairs):
            ls = slice(p * LANES, (p + 1) * LANES)
            qs = _stack_halves(q_ref[0, pl.ds(q0, GRID_W), ls])
            kw = kbuf[pl.ds(tok0, NA_WIN), ls]
            vw = vbuf[pl.ds(tok0, NA_WIN), ls]
            s_nb = lax.dot_general(qs, kw, (((1,), (1,)), ((), ())), preferred_element_type=F32)
            bias = jnp.concatenate([bias_ref[p, di0 + 2 * j] for j in range(NA_ROWS // 2)], axis=1)
            s_nb = s_nb + bias
            s_cx = lax.dot_general(qs, kx_ref[0, :, ls], (((1,), (1,)), ((), ())), preferred_element_type=F32)
            m = jnp.maximum(jnp.max(s_nb, axis=-1, keepdims=True), jnp.max(s_cx, axis=-1, keepdims=True))
            p_nb = jnp.exp(s_nb - m)
            p_cx = jnp.exp(s_cx - m)
            l = jnp.sum(p_nb, axis=-1, keepdims=True) + jnp.sum(p_cx, axis=-1, keepdims=True)
            o = (jnp.dot(p_nb.astype(BF16), vw, preferred_element_type=F32)
                 + jnp.dot(p_cx.astype(BF16), vx_ref[0, :, ls], preferred_element_type=F32)) / l
            o_ref[0, pl.ds(q0, GRID_W), ls] = jnp.where(lane < HEAD_DIM, o[:GRID_W], o[GRID_W:])
        return carry

    lax.fori_loop(0, NA_RB, row_body, 0)


def na_bias_table(rel_bias):
    cols = jnp.arange(GRID_W)
    c_start = jnp.clip(cols - NA_COLS // 2, 0, GRID_W - NA_COLS)
    kc = jnp.arange(GRID_W)
    valid = (kc[None, :] >= c_start[:, None]) & (kc[None, :] < c_start[:, None] + NA_COLS)
    idx = jnp.clip(kc[None, :] - cols[:, None] + NA_COLS - 1, 0, 2 * NA_COLS - 2)
    t = jnp.where(valid[None, None], rel_bias[:, :, idx], NEG)
    t2 = jnp.concatenate([t[:, :-1], t[:, 1:]], axis=-1)
    nh, nd = t2.shape[0], t2.shape[1]
    t2 = t2.reshape(nh // 2, 2, nd, GRID_W, LANES).transpose(0, 2, 1, 3, 4)
    return t2.reshape(nh // 2, nd, 2 * GRID_W, LANES).astype(F32)


def na_attention(q, k, v, bias_tab, n_lat):
    b, l, w = q.shape
    rows = n_lat // GRID_W
    nrb = rows // NA_RB
    n_ctx = l - n_lat
    ctx_blk = n_lat // n_ctx
    blk = lambda f: pl.BlockSpec((1, NA_BLK, w), f)
    prev = lambda i, j: (i, jnp.maximum(j - 1, 0), 0)
    cur = lambda i, j: (i, j, 0)
    nxt = lambda i, j: (i, jnp.minimum(j + 1, nrb - 1), 0)
    ctx = pl.BlockSpec((1, n_ctx, w), lambda i, j: (i, ctx_blk, 0))
    return pl.pallas_call(
        functools.partial(_na_kernel, rows=rows),
        grid=(b, nrb),
        in_specs=[blk(cur), blk(prev), blk(cur), blk(nxt), blk(prev), blk(cur), blk(nxt), ctx, ctx,
                  pl.BlockSpec(bias_tab.shape, lambda i, j: (0, 0, 0, 0))],
        out_specs=blk(cur),
        out_shape=jax.ShapeDtypeStruct((b, n_lat, w), F32),
        scratch_shapes=[pltpu.VMEM((3 * NA_BLK, w), BF16), pltpu.VMEM((3 * NA_BLK, w), BF16)],
        compiler_params=_cparams(("parallel", "parallel")),
        name="na_attention",
    )(q, k, k, k, v, v, v, k, v, bias_tab)


def _router_kernel(x_ref, g_ref, sh_ref, sc_ref, w_ref, b_ref, h_ref, ri_ref, rg_ref, cnt_ref, carry_ref):
    step = pl.program_id(0)

    @pl.when(step == 0)
    def _():
        carry_ref[...] = jnp.zeros_like(carry_ref)

    tm = x_ref.shape[0]
    x = x_ref[...]
    y = x * lax.rsqrt(jnp.mean(x * x, axis=-1, keepdims=True) + EPS) * g_ref[...]
    h = y * (1.0 + sc_ref[0]) + sh_ref[0]
    h_ref[...] = h.astype(BF16)
    w = w_ref[...]
    h1 = h.astype(BF16)
    h2 = (h - h1.astype(F32)).astype(BF16)
    w1 = w.astype(BF16)
    w2 = (w - w1.astype(F32)).astype(BF16)
    logits = (jnp.dot(h1, w1, preferred_element_type=F32) + jnp.dot(h2, w1, preferred_element_type=F32)
              + jnp.dot(h1, w2, preferred_element_type=F32)) + b_ref[...]
    lane = lax.broadcasted_iota(jnp.int32, (tm, LANES), 1)
    work = logits
    tops, idxs, hots = [], [], []
    for _ in range(TOP_K):
        mx = jnp.max(work, axis=-1, keepdims=True)
        ix = jnp.min(jnp.where(work == mx, lane, LANES), axis=-1, keepdims=True)
        hot = lane == ix
        work = jnp.where(hot, -jnp.inf, work)
        tops.append(mx)
        idxs.append(ix)
        hots.append(hot)
    es = [jnp.exp(t - tops[0]) for t in tops]
    den = es[0] + es[1] + es[2] + es[3]
    multi = jnp.zeros((tm, LANES), F32)
    for hot in hots:
        multi = multi + hot.astype(F32)
    ri = lax.broadcasted_iota(jnp.int32, (tm, tm), 0)
    ci = lax.broadcasted_iota(jnp.int32, (tm, tm), 1)
    tri = (ci < ri).astype(BF16)
    cum = jnp.dot(tri, multi.astype(BF16), preferred_element_type=F32) + carry_ref[...]
    carry_new = carry_ref[...] + jnp.sum(multi, axis=0, keepdims=True)
    carry_ref[...] = carry_new
    cnt_ref[...] = carry_new
    out_i = jnp.zeros((tm, LANES), jnp.int32)
    out_g = jnp.zeros((tm, LANES), F32)
    for kk in range(TOP_K):
        rank = jnp.sum(jnp.where(hots[kk], cum, 0.0), axis=-1, keepdims=True).astype(jnp.int32)
        out_i = jnp.where(lane == kk, idxs[kk], out_i)
        out_i = jnp.where(lane == TOP_K + kk, rank, out_i)
        out_g = jnp.where(lane == kk, es[kk] / den, out_g)
    ri_ref[...] = out_i
    rg_ref[...] = out_g


def moe_router(x2, gain, modarr, sh_row, sc_row, n_lat_tiles, tiles_per_batch, router_w, router_b):
    n, d = x2.shape
    tm = ROUTER_TM
    w_pad = jnp.zeros((d, LANES), F32).at[:, :N_EXPERTS].set(router_w)
    b_pad = jnp.full((1, LANES), NEG, F32).at[0, :N_EXPERTS].set(router_b)

    def mod_idx(which):
        def f(i):
            bidx = i // tiles_per_batch
            is_ctx = (i % tiles_per_batch) >= n_lat_tiles
            return ((bidx * 2 + is_ctx.astype(jnp.int32)) * 6 + which, 0, 0)
        return f

    return pl.pallas_call(
        _router_kernel,
        grid=(n // tm,),
        in_specs=[pl.BlockSpec((tm, d), lambda i: (i, 0)),
                  pl.BlockSpec((1, d), lambda i: (0, 0)),
                  pl.BlockSpec((1, 1, d), mod_idx(sh_row)),
                  pl.BlockSpec((1, 1, d), mod_idx(sc_row)),
                  pl.BlockSpec((d, LANES), lambda i: (0, 0)),
                  pl.BlockSpec((1, LANES), lambda i: (0, 0))],
        out_specs=[pl.BlockSpec((tm, d), lambda i: (i, 0)),
                   pl.BlockSpec((tm, LANES), lambda i: (i, 0)),
                   pl.BlockSpec((tm, LANES), lambda i: (i, 0)),
                   pl.BlockSpec((1, LANES), lambda i: (0, 0))],
        out_shape=[jax.ShapeDtypeStruct((n, d), BF16),
                   jax.ShapeDtypeStruct((n, LANES), jnp.int32),
                   jax.ShapeDtypeStruct((n, LANES), F32),
                   jax.ShapeDtypeStruct((1, LANES), F32)],
        scratch_shapes=[pltpu.VMEM((1, LANES), F32)],
        compiler_params=_cparams(("arbitrary",)),
        name="moe_router",
    )(x2, gain.reshape(1, d), modarr, modarr, w_pad, b_pad)


def _expert_kernel(be_ref, nu_ref, x_ref, wgu_ref, bgu_ref, wd_ref, bd_ref, o_ref):
    j = pl.program_id(0)

    @pl.when(j < nu_ref[0])
    def _():
        de = wd_ref.shape[1]
        gu = jnp.dot(x_ref[...], wgu_ref[0], preferred_element_type=F32) + bgu_ref[0]
        gate = jnp.minimum(gu[:, :de], SWIGLU_LIMIT)
        up = jnp.clip(gu[:, de:], -SWIGLU_LIMIT, SWIGLU_LIMIT)
        act = gate * jax.nn.sigmoid(SWIGLU_ALPHA * gate) * (up + 1.0)
        o_ref[...] = jnp.dot(act.astype(BF16), wd_ref[0], preferred_element_type=F32) + bd_ref[0]

    @pl.when(j >= nu_ref[0])
    def _():
        o_ref[...] = jnp.zeros_like(o_ref)


def moe_experts(xs, block_expert, n_used, wgu, bgu, wd, bd):
    cap, d = xs.shape
    de = wd.shape[1]
    nblk = cap // MOE_TM
    grid_spec = pltpu.PrefetchScalarGridSpec(
        num_scalar_prefetch=2,
        grid=(nblk,),
        in_specs=[pl.BlockSpec((MOE_TM, d), lambda j, be, nu: (j, 0)),
                  pl.BlockSpec((1, d, 2 * de), lambda j, be, nu: (be[j], 0, 0)),
                  pl.BlockSpec((1, 1, 2 * de), lambda j, be, nu: (be[j], 0, 0)),
                  pl.BlockSpec((1, de, d), lambda j, be, nu: (be[j], 0, 0)),
                  pl.BlockSpec((1, 1, d), lambda j, be, nu: (be[j], 0, 0))],
        out_specs=pl.BlockSpec((MOE_TM, d), lambda j, be, nu: (j, 0)),
    )
    return pl.pallas_call(
        _expert_kernel,
        grid_spec=grid_spec,
        out_shape=jax.ShapeDtypeStruct((cap, d), F32),
        compiler_params=_cparams(("arbitrary",)),
        name="moe_experts",
    )(block_expert, n_used, xs, wgu, bgu, wd, bd)


def moe_ffn(x2, gain, modarr, sh_row, sc_row, n_lat_tiles, tiles_per_batch, router_w, router_b,
            wgu, bgu, wd, bd):
    n, d = x2.shape
    h_bf, r_i, r_g, cnt = moe_router(x2, gain, modarr, sh_row, sc_row, n_lat_tiles, tiles_per_batch,
                                     router_w, router_b)
    top_idx = r_i[:, :TOP_K]
    rank = r_i[:, TOP_K:2 * TOP_K]
    gates = r_g[:, :TOP_K]
    counts = cnt[0, :N_EXPERTS].astype(jnp.int32)
    padded = (counts + MOE_TM - 1) // MOE_TM * MOE_TM
    pad_end = jnp.cumsum(padded)
    pad_start = pad_end - padded
    dest = pad_start[top_idx] + rank
    nblk = -(-(n * TOP_K + N_EXPERTS * (MOE_TM - 1)) // MOE_TM)
    cap = nblk * MOE_TM
    tok = jnp.broadcast_to(jnp.arange(n, dtype=jnp.int32)[:, None], (n, TOP_K))
    slot_tok = jnp.zeros((cap,), jnp.int32).at[dest.reshape(-1)].set(tok.reshape(-1), unique_indices=True)
    block_expert = jnp.minimum(jnp.searchsorted(pad_end, jnp.arange(nblk) * MOE_TM, side="right"),
                               N_EXPERTS - 1).astype(jnp.int32)
    n_used = (pad_end[-1:] // MOE_TM).astype(jnp.int32)
    xs = h_bf[slot_tok]
    y = moe_experts(xs, block_expert, n_used, wgu, bgu, wd, bd)
    yk = y[dest]
    return jnp.sum(yk * gates[:, :, None], axis=1)


def _rms(x, g):
    return x * lax.rsqrt(jnp.mean(x * x, axis=-1, keepdims=True) + EPS) * g


def _shift_conv(x, w, n_lat):
    width = w.shape[0]
    half = width // 2

    def seg(t):
        tp = jnp.pad(t, ((0, 0), (half, half), (0, 0)))
        ln = t.shape[1]
        out = tp[:, 0:ln] * w[0]
        for kk in range(1, width):
            out = out + tp[:, kk:kk + ln] * w[kk]
        return out

    return jnp.concatenate([seg(x[:, :n_lat]), seg(x[:, n_lat:])], axis=1)


def _axial_rope(n_tok):
    pos = jnp.arange(n_tok)
    rows = (pos // GRID_W).astype(F32)
    cols = (pos % GRID_W).astype(F32)
    quarter = HEAD_DIM // 4
    inv_freq = ROPE_BASE ** (-jnp.arange(quarter, dtype=F32) / quarter)
    ar = rows[:, None] * inv_freq
    ac = cols[:, None] * inv_freq
    ang = jnp.concatenate([ar, ar, ac, ac], axis=-1)
    return jnp.cos(ang), jnp.sin(ang)


def _rope(x, cos, sin):
    x1, x2, x3, x4 = jnp.split(x, 4, axis=-1)
    rot = jnp.concatenate([-x2, x1, -x4, x3], axis=-1)
    return x * cos + rot * sin


def _pick_tn(n):
    for tn in (1536, 1280, 1024, 768, 512, 256, 128):
        if n % tn == 0:
            return tn
    raise ValueError(n)


def _pick_tm(m):
    return 512 if m % 512 == 0 else 256


def _proj(h_bf, w_bf):
    return matmul(h_bf, w_bf, _pick_tm(h_bf.shape[0]), _pick_tn(w_bf.shape[1]))


def _even_mixer(h, n_lat, w_in, w_out, conv_w, conv_b, a_log, dt_bias, d_skip, norm_g, sc_conv_w):
    b, l, d = h.shape
    d_ssd = d
    gn = SSD_GROUPS * SSD_STATE
    conv_dim = d_ssd + 2 * gn
    n_in = w_in.shape[1]
    n_pad = -(-n_in // 256) * 256
    w_pad = jnp.pad(w_in, ((0, 0), (0, n_pad - n_in))).astype(BF16)
    proj = _proj(h.reshape(b * l, d).astype(BF16), w_pad).reshape(b, l, n_pad)
    o = 0
    z = proj[..., o:o + d_ssd]; o += d_ssd
    xbc = proj[..., o:o + conv_dim]; o += conv_dim
    dtr = proj[..., o:o + 2 * SSD_HEADS]; o += 2 * SSD_HEADS
    gb = proj[..., o:o + d]; o += d
    gc = proj[..., o:o + d]; o += d
    gh = proj[..., o:o + d]
    xbc = jax.nn.silu(_shift_conv(xbc, conv_w, n_lat) + conv_b)
    xs = xbc[..., :d_ssd]
    bm = xbc[..., d_ssd:d_ssd + gn].astype(BF16)
    cm = xbc[..., d_ssd + gn:].astype(BF16)
    dt = jax.nn.softplus(dtr.reshape(b, l, 2, SSD_HEADS) + dt_bias)
    a_neg = -jnp.exp(a_log)
    da = jnp.concatenate([dt, dt * a_neg], axis=-1)
    da = jnp.moveaxis(da, 2, 1)
    dat = jnp.swapaxes(da, 2, 3)
    sc = gb * _shift_conv(gc * gh, sc_conv_w, n_lat)
    xs_bf = xs.astype(BF16)
    bt = jnp.swapaxes(bm, 1, 2)
    yf = ssd_scan(xs_bf, bt, bm, cm, da, dat, n_lat, rev=False)
    yb = ssd_scan(xs_bf, bt, bm, cm, da, dat, n_lat, rev=True)
    dexp = jnp.repeat(d_skip, HEAD_DIM)
    y = yf + yb + dexp * xs
    y = _rms(y * jax.nn.silu(z), norm_g)
    cat = jnp.concatenate([y, sc], axis=-1).reshape(b * l, -1).astype(BF16)
    return matmul(cat, w_out.astype(BF16), _pick_tm(b * l), 1024).reshape(b, l, d)


def _odd_mixer(h, n_lat, w_in, w_out, na_qk_g, na_rel_bias, df_qk_g, df_lambda, df_subln_g, lambda_init,
               cos_t, sin_t):
    b, l, d = h.shape
    d_na = NA_HEADS * HEAD_DIM
    dq_w = DF_HEADS * 2 * HEAD_DIM
    proj = _proj(h.reshape(b * l, d).astype(BF16), w_in.astype(BF16)).reshape(b, l, -1)
    nq, nk, nv, dq, dk, dv = jnp.split(proj, [d_na, 2 * d_na, 3 * d_na, 3 * d_na + dq_w, 3 * d_na + 2 * dq_w],
                                       axis=-1)
    nq = _rms(nq.reshape(b, l, NA_HEADS, HEAD_DIM), na_qk_g[0]).reshape(b, l, d_na).astype(BF16)
    nk = _rms(nk.reshape(b, l, NA_HEADS, HEAD_DIM), na_qk_g[1]).reshape(b, l, d_na).astype(BF16)
    nv = nv.astype(BF16)
    dq = _rms(dq.reshape(b, l, 2 * DF_HEADS, HEAD_DIM), df_qk_g[0])
    dk = _rms(dk.reshape(b, l, 2 * DF_HEADS, HEAD_DIM), df_qk_g[1])
    dq = _rope(dq, cos_t[:, None, :], sin_t[:, None, :]).reshape(b, l, dq_w).astype(BF16)
    dk = _rope(dk, cos_t[:, None, :], sin_t[:, None, :]).reshape(b, l, dq_w).astype(BF16)
    dv = dv.astype(BF16)
    n_ctx = l - n_lat
    lam_p = df_lambda.astype(F32)
    na_lat = na_attention(nq, nk, nv, na_bias_table(na_rel_bias), n_lat)
    na_ctx = flash_slabs(nq[:, n_lat:], nk[:, n_lat:], nv[:, n_lat:], lam_p, "pair", n_ctx, n_ctx)
    tk = 768 if l % 768 == 0 else 256
    df_lat = flash_slabs(dq[:, :n_lat], dk, dv, lam_p, "diff", 256, tk, lambda_init)
    df_ctx = flash_slabs(dq[:, n_lat:], dk[:, n_lat:], dv[:, n_lat:], lam_p, "diff", n_ctx, n_ctx, lambda_init)
    na_o = jnp.concatenate([na_lat, na_ctx], axis=1)
    df_o = jnp.concatenate([df_lat, df_ctx], axis=1).reshape(b, l, DF_HEADS, 2 * HEAD_DIM)
    df_o = (_rms(df_o, df_subln_g) * (1.0 - lambda_init)).reshape(b, l, dq_w)
    cat = jnp.concatenate([na_o, df_o], axis=-1).reshape(b * l, -1).astype(BF16)
    return matmul(cat, w_out.astype(BF16), _pick_tm(b * l), 1024).reshape(b, l, d)


def kernel(x, c, ctx, c_ctx, ada_w, ada_b, mix_norm_g, ffn_norm_g, router_w, router_b, moe_w_gu, moe_b_gu,
           moe_w_down, moe_b_down, ev_w_in, ev_w_out, ssd_conv_w, ssd_conv_b, ssd_a_log, ssd_dt_bias, ssd_d,
           ssd_norm_g, sc_conv_w, od_w_in, od_w_out, na_qk_g, na_rel_bias, df_qk_g, df_lambda, df_subln_g):
    b, s, d = x.shape
    n_ctx = ctx.shape[1]
    l = s + n_ctx
    depth = ada_w.shape[0]
    de = moe_w_down.shape[2]
    xa = jnp.concatenate([x, ctx], axis=1)
    is_ctx = (jnp.arange(l) >= s)[None, :, None]
    cos_l, sin_l = _axial_rope(s)
    cos_t = jnp.concatenate([cos_l, jnp.ones((n_ctx, HEAD_DIM), F32)], axis=0)
    sin_t = jnp.concatenate([sin_l, jnp.zeros((n_ctx, HEAD_DIM), F32)], axis=0)
    cond = jnp.concatenate([jax.nn.silu(c), jax.nn.silu(c_ctx)[None, :]], axis=0)
    cond_pad = jnp.zeros((16, d), F32).at[:b + 1].set(cond).astype(BF16)
    tiles_per_batch = l // ROUTER_TM
    n_lat_tiles = s // ROUTER_TM

    for i in range(depth):
        j = i // 2
        mod = matmul(cond_pad, ada_w[i].astype(BF16), 16, 6 * d // 4)[:b + 1] + ada_b[i]
        mod6 = mod.reshape(b + 1, 6, d)
        modarr = jnp.stack([mod6[:b], jnp.broadcast_to(mod6[b], (b, 6, d))], axis=1)

        def tokmod(which):
            return jnp.where(is_ctx, modarr[:, 1, which][:, None, :], modarr[:, 0, which][:, None, :])

        h = _rms(xa, mix_norm_g[i]) * (1.0 + tokmod(1)) + tokmod(0)
        if i % 2 == 0:
            y = _even_mixer(h, s, ev_w_in[j], ev_w_out[j], ssd_conv_w[j], ssd_conv_b[j], ssd_a_log[j],
                            ssd_dt_bias[j], ssd_d[j], ssd_norm_g[j], sc_conv_w[j])
        else:
            lambda_init = 0.8 - 0.6 * math.exp(-0.3 * i)
            y = _odd_mixer(h, s, od_w_in[j], od_w_out[j], na_qk_g[j], na_rel_bias[j], df_qk_g[j], df_lambda[j],
                           df_subln_g[j], lambda_init, cos_t, sin_t)
        xa = xa + tokmod(2) * y
        wgu = jnp.concatenate([moe_w_gu[i][:, :, 0::2], moe_w_gu[i][:, :, 1::2]], axis=-1).astype(BF16)
        bgu = jnp.concatenate([moe_b_gu[i][:, 0::2], moe_b_gu[i][:, 1::2]], axis=-1)[:, None, :]
        f = moe_ffn(xa.reshape(b * l, d), ffn_norm_g[i], modarr.reshape(b * 2 * 6, 1, d), 3, 4,
                    n_lat_tiles, tiles_per_batch, router_w[i], router_b[i],
                    wgu, bgu, moe_w_down[i].astype(BF16), moe_b_down[i][:, None, :])
        xa = xa + tokmod(5) * f.reshape(b, l, d)
    return xa[:, :s]
```

```python
import functools
import math

import jax
import jax.numpy as jnp
from jax import lax
from jax.experimental import pallas as pl
from jax.experimental.pallas import tpu as pltpu

F32 = jnp.float32
BF16 = jnp.bfloat16

GRID_W = 64
HEAD_DIM = 64
EPS = 1e-6
SSD_HEADS = 16
SSD_GROUPS = 4
SSD_STATE = 128
NA_ROWS = 8
NA_COLS = 16
NA_HEADS = 8
DF_HEADS = 4
N_EXPERTS = 32
TOP_K = 4
SWIGLU_LIMIT = 7.0
SWIGLU_ALPHA = 1.702
ROPE_BASE = 10000.0

LANES = 128
SSD_Q = 128
MOE_TM = 512
ROUTER_TM = 256
VMEM_LIMIT = 56 * 1024 * 1024
NEG = -1e30


def _cparams(sem):
    return pltpu.CompilerParams(dimension_semantics=sem, vmem_limit_bytes=VMEM_LIMIT)


def _mm_kernel(a_ref, w_ref, o_ref):
    o_ref[...] = jnp.dot(a_ref[...], w_ref[...], preferred_element_type=F32).astype(o_ref.dtype)


def matmul(a, w, tm, tn, out_dtype=F32):
    m, k = a.shape
    n = w.shape[1]
    return pl.pallas_call(
        _mm_kernel,
        grid=(n // tn, m // tm),
        in_specs=[pl.BlockSpec((tm, k), lambda j, i: (i, 0)),
                  pl.BlockSpec((k, tn), lambda j, i: (0, j))],
        out_specs=pl.BlockSpec((tm, tn), lambda j, i: (i, j)),
        out_shape=jax.ShapeDtypeStruct((m, n), out_dtype),
        compiler_params=_cparams(("parallel", "parallel")),
        name="matmul",
    )(a, w)


MOD_ROWS = 16
CTX_ROW0 = 8
ODD_TN = 512


def _row_is_ctx(i, tm, tiles_per_batch, n_lat):
    t = i % tiles_per_batch
    row = t * tm + lax.broadcasted_iota(jnp.int32, (tm, 1), 0)
    return row >= n_lat


def _mod_row(mod, is_ctx, which):
    return jnp.where(is_ctx, mod[CTX_ROW0 + which:CTX_ROW0 + which + 1], mod[which:which + 1])


def _modulated_rows(x_ref, g_ref, mod_ref, is_ctx, sh_row, sc_row):
    x = x_ref[...]
    y = x * lax.rsqrt(jnp.mean(x * x, axis=-1, keepdims=True) + EPS) * g_ref[...]
    mod = mod_ref[0]
    return y * (1.0 + _mod_row(mod, is_ctx, sc_row)) + _mod_row(mod, is_ctx, sh_row)


def _proj_kernel(x_ref, g_ref, mod_ref, w_ref, o_ref, h_ref, *, tiles_per_batch, n_lat, sh_row, sc_row):
    i = pl.program_id(0)

    @pl.when(pl.program_id(1) == 0)
    def _():
        is_ctx = _row_is_ctx(i, x_ref.shape[0], tiles_per_batch, n_lat)
        h_ref[...] = _modulated_rows(x_ref, g_ref, mod_ref, is_ctx, sh_row, sc_row).astype(BF16)

    o_ref[...] = jnp.dot(h_ref[...], w_ref[...], preferred_element_type=F32)


def _group_rms(a, gsum, gain):
    outs = []
    half = gsum.shape[0]
    for hf in range(a.shape[1] // half):
        ah = a[:, hf * half:(hf + 1) * half]
        ms = jnp.dot((ah * ah).astype(BF16), gsum, preferred_element_type=F32) * (1.0 / HEAD_DIM)
        outs.append(ah * lax.rsqrt(ms + EPS))
    return jnp.concatenate(outs, axis=1) * gain


def _proj_odd_kernel(x_ref, g_ref, mod_ref, w_ref, gain_ref, gsum_ref, cos_ref, sina_ref, sinb_ref, o_ref, h_ref, *,
                     tiles_per_batch, n_lat, sh_row, sc_row):
    i = pl.program_id(0)
    j = pl.program_id(1)

    @pl.when(j == 0)
    def _():
        is_ctx = _row_is_ctx(i, x_ref.shape[0], tiles_per_batch, n_lat)
        h_ref[...] = _modulated_rows(x_ref, g_ref, mod_ref, is_ctx, sh_row, sc_row).astype(BF16)

    acc = jnp.dot(h_ref[...], w_ref[...], preferred_element_type=F32)
    is_norm = (j == 0) | (j == 1) | (j == 3) | (j == 4)
    is_rope = (j == 3) | (j == 4)

    @pl.when(jnp.logical_not(is_norm))
    def _():
        o_ref[...] = acc.astype(BF16)

    @pl.when(is_norm & jnp.logical_not(is_rope))
    def _():
        o_ref[...] = _group_rms(acc, gsum_ref[...], gain_ref[0]).astype(BF16)

    @pl.when(is_rope)
    def _():
        xn = _group_rms(acc, gsum_ref[...], gain_ref[0])
        reps = xn.shape[1] // LANES
        cos = jnp.concatenate([cos_ref[...]] * reps, axis=1)
        sina = jnp.concatenate([sina_ref[...]] * reps, axis=1)
        sinb = jnp.concatenate([sinb_ref[...]] * reps, axis=1)
        quarter = HEAD_DIM // 4
        up = pltpu.roll(xn, xn.shape[1] - quarter, axis=1)
        dn = pltpu.roll(xn, quarter, axis=1)
        o_ref[...] = (xn * cos + up * sina + dn * sinb).astype(BF16)


def _pick_rows(l):
    for tm in (768, 512, 256):
        if l % tm == 0:
            return tm
    raise ValueError(l)


def proj_modulated(x2, gain, mod16, w_bf, l, n_lat, sh_row, sc_row, tn):
    m, d = x2.shape
    n = w_bf.shape[1]
    tm = _pick_rows(l)
    tpb = l // tm
    return pl.pallas_call(
        functools.partial(_proj_kernel, tiles_per_batch=tpb, n_lat=n_lat, sh_row=sh_row, sc_row=sc_row),
        grid=(m // tm, n // tn),
        in_specs=[pl.BlockSpec((tm, d), lambda i, j: (i, 0)),
                  pl.BlockSpec((1, d), lambda i, j: (0, 0)),
                  pl.BlockSpec((1, MOD_ROWS, d), lambda i, j: (i // tpb, 0, 0)),
                  pl.BlockSpec((d, tn), lambda i, j: (0, j))],
        out_specs=pl.BlockSpec((tm, tn), lambda i, j: (i, j)),
        out_shape=jax.ShapeDtypeStruct((m, n), F32),
        scratch_shapes=[pltpu.VMEM((tm, d), BF16)],
        compiler_params=_cparams(("parallel", "arbitrary")),
        name="proj_even",
    )(x2, gain.reshape(1, d), mod16, w_bf)


def proj_odd(x2, gain, mod16, w_bf, gains6, cos2, sina2, sinb2, l, n_lat, sh_row, sc_row):
    m, d = x2.shape
    n = w_bf.shape[1]
    tn = ODD_TN
    tm = _pick_rows(l)
    tpb = l // tm
    half = 256
    gi = jnp.arange(half) // HEAD_DIM
    gsum = (gi[:, None] == gi[None, :]).astype(BF16)
    tab = pl.BlockSpec((tm, LANES), lambda i, j: (i % tpb, 0))
    return pl.pallas_call(
        functools.partial(_proj_odd_kernel, tiles_per_batch=tpb, n_lat=n_lat, sh_row=sh_row, sc_row=sc_row),
        grid=(m // tm, n // tn),
        in_specs=[pl.BlockSpec((tm, d), lambda i, j: (i, 0)),
                  pl.BlockSpec((1, d), lambda i, j: (0, 0)),
                  pl.BlockSpec((1, MOD_ROWS, d), lambda i, j: (i // tpb, 0, 0)),
                  pl.BlockSpec((d, tn), lambda i, j: (0, j)),
                  pl.BlockSpec((1, 1, tn), lambda i, j: (j, 0, 0)),
                  pl.BlockSpec((half, half), lambda i, j: (0, 0)),
                  tab, tab, tab],
        out_specs=pl.BlockSpec((tm, tn), lambda i, j: (i, j)),
        out_shape=jax.ShapeDtypeStruct((m, n), BF16),
        scratch_shapes=[pltpu.VMEM((tm, d), BF16)],
        compiler_params=_cparams(("parallel", "arbitrary")),
        name="proj_odd",
    )(x2, gain.reshape(1, d), mod16, w_bf, gains6, gsum, cos2, sina2, sinb2)


def _mm_res_kernel(a_ref, w_ref, r_ref, mod_ref, o_ref, *, tiles_per_batch, n_lat, gate_row):
    is_ctx = _row_is_ctx(pl.program_id(0), a_ref.shape[0], tiles_per_batch, n_lat)
    gate = _mod_row(mod_ref[0], is_ctx, gate_row)
    o_ref[...] = r_ref[...] + gate * jnp.dot(a_ref[...], w_ref[...], preferred_element_type=F32)


def matmul_residual(a, w_bf, res, mod16, l, n_lat, gate_row):
    m, k = a.shape
    d = w_bf.shape[1]
    tm = _pick_rows(l)
    tpb = l // tm
    return pl.pallas_call(
        functools.partial(_mm_res_kernel, tiles_per_batch=tpb, n_lat=n_lat, gate_row=gate_row),
        grid=(m // tm,),
        in_specs=[pl.BlockSpec((tm, k), lambda i: (i, 0)),
                  pl.BlockSpec((k, d), lambda i: (0, 0)),
                  pl.BlockSpec((tm, d), lambda i: (i, 0)),
                  pl.BlockSpec((1, MOD_ROWS, d), lambda i: (i // tpb, 0, 0))],
        out_specs=pl.BlockSpec((tm, d), lambda i: (i, 0)),
        out_shape=jax.ShapeDtypeStruct((m, d), F32),
        compiler_params=_cparams(("parallel",)),
        name="out_proj",
    )(a, w_bf, res, mod16)


GU_BLK = 2 * LANES


def _deint_kernel(w_ref, p_ref, o_ref):
    p = p_ref[...]
    for blk in range(w_ref.shape[1] // GU_BLK):
        cs = slice(blk * GU_BLK, (blk + 1) * GU_BLK)
        o_ref[:, cs] = jnp.dot(w_ref[:, cs].astype(BF16), p, preferred_element_type=F32).astype(BF16)


def deinterleave_gate_up(w2):
    r, n = w2.shape
    tm = 512
    src = jnp.arange(GU_BLK)
    dst = jnp.where(src % 2 == 0, src // 2, LANES + src // 2)
    perm = (dst[:, None] == jnp.arange(GU_BLK)[None, :]).astype(BF16)
    return pl.pallas_call(
        _deint_kernel,
        grid=(r // tm,),
        in_specs=[pl.BlockSpec((tm, n), lambda i: (i, 0)),
                  pl.BlockSpec((GU_BLK, GU_BLK), lambda i: (0, 0))],
        out_specs=pl.BlockSpec((tm, n), lambda i: (i, 0)),
        out_shape=jax.ShapeDtypeStruct((r, n), BF16),
        compiler_params=_cparams(("parallel",)),
        name="deinterleave_gate_up",
    )(w2, perm)


def _split3_dot(tri, a, dims):
    a1 = a.astype(BF16)
    r1 = a - a1.astype(F32)
    a2 = r1.astype(BF16)
    a3 = (r1 - a2.astype(F32)).astype(BF16)
    out = None
    for piece in (a1, a2, a3):
        if dims == "tri_a":
            t = jnp.dot(tri, piece, preferred_element_type=F32)
        else:
            t = jnp.dot(piece, tri, preferred_element_type=F32)
        out = t if out is None else out + t
    return out


def _ssd_kernel(x_ref, bt_ref, b_ref, c_ref, da_ref, dat_ref, y_ref, st_ref, *, rev):
    q = SSD_Q
    step = pl.program_id(1)

    @pl.when(step == 0)
    def _():
        st_ref[...] = jnp.zeros_like(st_ref)

    da = da_ref[0, 0]
    dat = dat_ref[0, 0]
    a_c = da[:, SSD_HEADS:2 * SSD_HEADS]
    dt_r = dat[0:SSD_HEADS]
    a_r = dat[SSD_HEADS:2 * SSD_HEADS]
    ri = lax.broadcasted_iota(jnp.int32, (q, q), 0)
    ci = lax.broadcasted_iota(jnp.int32, (q, q), 1)
    tri = (ci <= ri).astype(BF16)
    tri_t = (ri <= ci).astype(BF16)
    cum_c = _split3_dot(tri, a_c, "tri_a")
    cum_r = _split3_dot(tri_t, a_r, "a_tri")
    tot_r = cum_r[:, q - 1:q]
    if rev:
        pos_c = cum_c - a_c
        pos_r = cum_r - a_r
        mask = ci >= ri
    else:
        pos_c = cum_c
        pos_r = cum_r
        mask = ri >= ci
    lane = lax.broadcasted_iota(jnp.int32, (q, LANES), 1)
    lane_n = lax.broadcasted_iota(jnp.int32, (SSD_STATE, LANES), 1)
    heads_per_group = SSD_HEADS // SSD_GROUPS

    for g in range(SSD_GROUPS):
        gs = slice(g * SSD_STATE, (g + 1) * SSD_STATE)
        bg = b_ref[0, :, gs]
        cg = c_ref[0, :, gs]
        btg = bt_ref[0, gs, :].astype(F32)
        cb = lax.dot_general(cg, bg, (((1,), (1,)), ((), ())), preferred_element_type=F32)
        cg32 = cg.astype(F32)
        for pp in range(heads_per_group // 2):
            p = g * (heads_per_group // 2) + pp
            xp = x_ref[0, :, p * LANES:(p + 1) * LANES]
            st = st_ref[p]
            rhs = jnp.concatenate([xp, st.astype(BF16)], axis=0)
            ys, ds, decs = [], [], []
            for hh in range(2):
                h = 2 * p + hh
                colb = jnp.broadcast_to(pos_c[:, h:h + 1], (q, LANES))
                row = pos_r[h:h + 1, :]
                dt_row = dt_r[h:h + 1, :]
                tot = tot_r[h:h + 1, :]
                if rev:
                    seg = row - colb
                    coff = jnp.exp(tot - colb)
                    w_row = dt_row * jnp.exp(row)
                else:
                    seg = colb - row
                    coff = jnp.exp(colb)
                    w_row = dt_row * jnp.exp(tot - row)
                decay = jnp.exp(jnp.where(mask, seg, NEG))
                m_h = (cb * decay * dt_row).astype(BF16)
                c_h = (cg32 * coff).astype(BF16)
                lhs = jnp.concatenate([m_h, c_h], axis=1)
                ys.append(jnp.dot(lhs, rhs, preferred_element_type=F32))
                btw = (btg * w_row).astype(BF16)
                ds.append(jnp.dot(btw, xp, preferred_element_type=F32))
                decs.append(jnp.exp(tot))
            y_ref[0, :, p * LANES:(p + 1) * LANES] = jnp.where(lane < HEAD_DIM, ys[0], ys[1])
            st_ref[p] = jnp.where(lane_n < HEAD_DIM, decs[0] * st + ds[0], decs[1] * st + ds[1])


def ssd_scan(xs, bt, bm, cm, da, dat, n_lat, rev):
    b, l, d = xs.shape
    nch = l // SSD_Q
    nlat = n_lat // SSD_Q
    nctx = nch - nlat
    d_idx = 1 if rev else 0
    if rev:
        def chunk(j):
            return nch - 1 - j
    else:
        def chunk(j):
            return jnp.where(j < nctx, nlat + j, j - nctx)
    gn = SSD_GROUPS * SSD_STATE
    return pl.pallas_call(
        functools.partial(_ssd_kernel, rev=rev),
        grid=(b, nch),
        in_specs=[pl.BlockSpec((1, SSD_Q, d), lambda i, j: (i, chunk(j), 0)),
                  pl.BlockSpec((1, gn, SSD_Q), lambda i, j: (i, 0, chunk(j))),
                  pl.BlockSpec((1, SSD_Q, gn), lambda i, j: (i, chunk(j), 0)),
                  pl.BlockSpec((1, SSD_Q, gn), lambda i, j: (i, chunk(j), 0)),
                  pl.BlockSpec((1, 1, SSD_Q, 2 * SSD_HEADS), lambda i, j: (i, d_idx, chunk(j), 0)),
                  pl.BlockSpec((1, 1, 2 * SSD_HEADS, SSD_Q), lambda i, j: (i, d_idx, 0, chunk(j)))],
        out_specs=pl.BlockSpec((1, SSD_Q, d), lambda i, j: (i, chunk(j), 0)),
        out_shape=jax.ShapeDtypeStruct((b, l, d), F32),
        scratch_shapes=[pltpu.VMEM((SSD_HEADS // 2, SSD_STATE, LANES), F32)],
        compiler_params=_cparams(("parallel", "arbitrary")),
        name="ssd_bwd" if rev else "ssd_fwd",
    )(xs, bt, bm, cm, da, dat)


def _stack_halves(qv):
    lane = lax.broadcasted_iota(jnp.int32, qv.shape, 1)
    zero = jnp.zeros_like(qv)
    q1 = jnp.where(lane < HEAD_DIM, qv, zero)
    q2 = jnp.where(lane >= HEAD_DIM, qv, zero)
    return jnp.concatenate([q1, q2], axis=0) * jnp.asarray(HEAD_DIM ** -0.5, qv.dtype)


def _flash_kernel(lam_ref, q_ref, k_ref, v_ref, o_ref, qs_ref, m_ref, l_ref, acc_ref, *, mode, tk, lambda_init):
    tq = q_ref.shape[1]
    nk = k_ref.shape[1] // tk
    nt = tk // LANES
    qs_ref[...] = _stack_halves(q_ref[0])
    m_ref[...] = jnp.full_like(m_ref, -jnp.inf)
    l_ref[...] = jnp.zeros_like(l_ref)
    acc_ref[...] = jnp.zeros_like(acc_ref)

    def body(c, carry):
        start = pl.multiple_of(c * tk, tk)
        kc = k_ref[0, pl.ds(start, tk), :]
        vc = v_ref[0, pl.ds(start, tk), :]
        s = lax.dot_general(qs_ref[...], kc, (((1,), (1,)), ((), ())), preferred_element_type=F32)
        tiles = [s[:, t * LANES:(t + 1) * LANES] for t in range(nt)]
        smax = tiles[0]
        for t in tiles[1:]:
            smax = jnp.maximum(smax, t)
        m_old = m_ref[...]
        m_new = jnp.maximum(m_old, jnp.max(smax, axis=-1, keepdims=True))
        alpha = jnp.exp(m_old - m_new)
        ps = [jnp.exp(t - m_new) for t in tiles]
        psum = ps[0]
        for t in ps[1:]:
            psum = psum + t
        l_ref[...] = alpha * l_ref[...] + jnp.sum(psum, axis=-1, keepdims=True)
        p = jnp.concatenate([t.astype(BF16) for t in ps], axis=1)
        acc_ref[...] = alpha * acc_ref[...] + jnp.dot(p, vc, preferred_element_type=F32)
        m_ref[...] = m_new
        return carry

    lax.fori_loop(0, nk, body, 0)
    o = acc_ref[...] / l_ref[...]
    o1, o2 = o[:tq], o[tq:]
    if mode == "diff":
        lp = lam_ref[...]
        s01 = jnp.sum(lp[0:1] * lp[1:2], axis=-1, keepdims=True)
        s23 = jnp.sum(lp[2:3] * lp[3:4], axis=-1, keepdims=True)
        lam = jnp.exp(s01) - jnp.exp(s23) + lambda_init
        out = o1 - lam * o2
    else:
        lane = lax.broadcasted_iota(jnp.int32, o1.shape, 1)
        out = jnp.where(lane < HEAD_DIM, o1, o2)
    o_ref[0] = out.astype(o_ref.dtype)


def flash_slabs(qkv, lam_p, mode, tq, tk, lq, lk, q_blk0, k_blk, q_slab0, k_slab0, v_slab0, n_slabs,
                lambda_init=0.0):
    b = qkv.shape[0]
    return pl.pallas_call(
        functools.partial(_flash_kernel, mode=mode, tk=tk, lambda_init=lambda_init),
        grid=(b, n_slabs, lq // tq),
        in_specs=[pl.BlockSpec(lam_p.shape, lambda i, h, j: (0, 0)),
                  pl.BlockSpec((1, tq, LANES), lambda i, h, j: (i, q_blk0 + j, q_slab0 + h)),
                  pl.BlockSpec((1, lk, LANES), lambda i, h, j: (i, k_blk, k_slab0 + h)),
                  pl.BlockSpec((1, lk, LANES), lambda i, h, j: (i, k_blk, v_slab0 + h))],
        out_specs=pl.BlockSpec((1, tq, LANES), lambda i, h, j: (i, j, h)),
        out_shape=jax.ShapeDtypeStruct((b, lq, n_slabs * LANES), F32),
        scratch_shapes=[pltpu.VMEM((2 * tq, LANES), BF16), pltpu.VMEM((2 * tq, LANES), F32),
                        pltpu.VMEM((2 * tq, LANES), F32), pltpu.VMEM((2 * tq, LANES), F32)],
        compiler_params=_cparams(("parallel", "parallel", "arbitrary")),
        name="flash_" + mode,
    )(lam_p, qkv, qkv, qkv)


NA_RB = 8
NA_BLK = NA_RB * GRID_W
NA_WIN = NA_ROWS * GRID_W


def _na_kernel(q_ref, kp_ref, kc_ref, kn_ref, vp_ref, vc_ref, vn_ref, kx_ref, vx_ref, bias_ref, o_ref,
               kbuf, vbuf, *, rows):
    rb = pl.program_id(1)
    kbuf[0:NA_BLK] = kp_ref[0]
    kbuf[NA_BLK:2 * NA_BLK] = kc_ref[0]
    kbuf[2 * NA_BLK:3 * NA_BLK] = kn_ref[0]
    vbuf[0:NA_BLK] = vp_ref[0]
    vbuf[NA_BLK:2 * NA_BLK] = vc_ref[0]
    vbuf[2 * NA_BLK:3 * NA_BLK] = vn_ref[0]
    npairs = q_ref.shape[2] // LANES
    lane = lax.broadcasted_iota(jnp.int32, (GRID_W, LANES), 1)

    def row_body(rl, carry):
        r = rb * NA_RB + rl
        r_start = jnp.clip(r - NA_ROWS // 2, 0, rows - NA_ROWS)
        off = r_start - (rb * NA_RB - NA_RB)
        di0 = r_start - r + NA_ROWS - 1
        tok0 = pl.multiple_of(off * GRID_W, GRID_W)
        q0 = pl.multiple_of(rl * GRID_W, GRID_W)
        for p in range(npairs):
            ls = slice(p * LANES, (p + 1) * LANES)
            qs = _stack_halves(q_ref[0, pl.ds(q0, GRID_W), ls])
            kw = kbuf[pl.ds(tok0, NA_WIN), ls]
            vw = vbuf[pl.ds(tok0, NA_WIN), ls]
            s_nb = lax.dot_general(qs, kw, (((1,), (1,)), ((), ())), preferred_element_type=F32)
            bias = jnp.concatenate([bias_ref[p, di0 + 2 * j] for j in range(NA_ROWS // 2)], axis=1)
            s_nb = s_nb + bias
            s_cx = lax.dot_general(qs, kx_ref[0, :, ls], (((1,), (1,)), ((), ())), preferred_element_type=F32)
            m = jnp.maximum(jnp.max(s_nb, axis=-1, keepdims=True), jnp.max(s_cx, axis=-1, keepdims=True))
            p_nb = jnp.exp(s_nb - m)
            p_cx = jnp.exp(s_cx - m)
            l = jnp.sum(p_nb, axis=-1, keepdims=True) + jnp.sum(p_cx, axis=-1, keepdims=True)
            o = (jnp.dot(p_nb.astype(BF16), vw, preferred_element_type=F32)
                 + jnp.dot(p_cx.astype(BF16), vx_ref[0, :, ls], preferred_element_type=F32)) / l
            o_ref[0, pl.ds(q0, GRID_W), ls] = jnp.where(lane < HEAD_DIM, o[:GRID_W], o[GRID_W:])
        return carry

    lax.fori_loop(0, NA_RB, row_body, 0)


def na_bias_table(rel_bias):
    cols = jnp.arange(GRID_W)
    c_start = jnp.clip(cols - NA_COLS // 2, 0, GRID_W - NA_COLS)
    kc = jnp.arange(GRID_W)
    valid = (kc[None, :] >= c_start[:, None]) & (kc[None, :] < c_start[:, None] + NA_COLS)
    idx = jnp.clip(kc[None, :] - cols[:, None] + NA_COLS - 1, 0, 2 * NA_COLS - 2)
    t = jnp.where(valid[None, None], rel_bias[:, :, idx], NEG)
    t2 = jnp.concatenate([t[:, :-1], t[:, 1:]], axis=-1)
    nh, nd = t2.shape[0], t2.shape[1]
    t2 = t2.reshape(nh // 2, 2, nd, GRID_W, LANES).transpose(0, 2, 1, 3, 4)
    return t2.reshape(nh // 2, nd, 2 * GRID_W, LANES).astype(F32)


def na_attention(qkv, bias_tab, n_lat):
    b, l, _ = qkv.shape
    w = NA_HEADS * HEAD_DIM
    rows = n_lat // GRID_W
    nrb = rows // NA_RB
    n_ctx = l - n_lat
    ctx_blk = n_lat // n_ctx

    def blk(step, stream):
        if step < 0:
            return pl.BlockSpec((1, NA_BLK, w), lambda i, j: (i, jnp.maximum(j - 1, 0), stream))
        if step > 0:
            return pl.BlockSpec((1, NA_BLK, w), lambda i, j: (i, jnp.minimum(j + 1, nrb - 1), stream))
        return pl.BlockSpec((1, NA_BLK, w), lambda i, j: (i, j, stream))

    def ctx(stream):
        return pl.BlockSpec((1, n_ctx, w), lambda i, j: (i, ctx_blk, stream))

    return pl.pallas_call(
        functools.partial(_na_kernel, rows=rows),
        grid=(b, nrb),
        in_specs=[blk(0, 0), blk(-1, 1), blk(0, 1), blk(1, 1), blk(-1, 2), blk(0, 2), blk(1, 2), ctx(1), ctx(2),
                  pl.BlockSpec(bias_tab.shape, lambda i, j: (0, 0, 0, 0))],
        out_specs=pl.BlockSpec((1, NA_BLK, w), lambda i, j: (i, j, 0)),
        out_shape=jax.ShapeDtypeStruct((b, n_lat, w), F32),
        scratch_shapes=[pltpu.VMEM((3 * NA_BLK, w), BF16), pltpu.VMEM((3 * NA_BLK, w), BF16)],
        compiler_params=_cparams(("parallel", "parallel")),
        name="na_attention",
    )(qkv, qkv, qkv, qkv, qkv, qkv, qkv, qkv, qkv, bias_tab)


def _router_kernel(x_ref, g_ref, sh_ref, sc_ref, w_ref, b_ref, h_ref, ri_ref, rg_ref, cnt_ref, carry_ref):
    step = pl.program_id(0)

    @pl.when(step == 0)
    def _():
        carry_ref[...] = jnp.zeros_like(carry_ref)

    tm = x_ref.shape[0]
    x = x_ref[...]
    y = x * lax.rsqrt(jnp.mean(x * x, axis=-1, keepdims=True) + EPS) * g_ref[...]
    h = y * (1.0 + sc_ref[0]) + sh_ref[0]
    h_ref[...] = h.astype(BF16)
    w = w_ref[...]
    h1 = h.astype(BF16)
    h2 = (h - h1.astype(F32)).astype(BF16)
    w1 = w.astype(BF16)
    w2 = (w - w1.astype(F32)).astype(BF16)
    logits = (jnp.dot(h1, w1, preferred_element_type=F32) + jnp.dot(h2, w1, preferred_element_type=F32)
              + jnp.dot(h1, w2, preferred_element_type=F32)) + b_ref[...]
    lane = lax.broadcasted_iota(jnp.int32, (tm, LANES), 1)
    work = logits
    tops, idxs, hots = [], [], []
    for _ in range(TOP_K):
        mx = jnp.max(work, axis=-1, keepdims=True)
        ix = jnp.min(jnp.where(work == mx, lane, LANES), axis=-1, keepdims=True)
        hot = lane == ix
        work = jnp.where(hot, -jnp.inf, work)
        tops.append(mx)
        idxs.append(ix)
        hots.append(hot)
    es = [jnp.exp(t - tops[0]) for t in tops]
    den = es[0] + es[1] + es[2] + es[3]
    multi = jnp.zeros((tm, LANES), F32)
    for hot in hots:
        multi = multi + hot.astype(F32)
    ri = lax.broadcasted_iota(jnp.int32, (tm, tm), 0)
    ci = lax.broadcasted_iota(jnp.int32, (tm, tm), 1)
    tri = (ci < ri).astype(BF16)
    cum = jnp.dot(tri, multi.astype(BF16), preferred_element_type=F32) + carry_ref[...]
    carry_new = carry_ref[...] + jnp.sum(multi, axis=0, keepdims=True)
    carry_ref[...] = carry_new
    cnt_ref[...] = carry_new
    out_i = jnp.zeros((tm, LANES), jnp.int32)
    out_g = jnp.zeros((tm, LANES), F32)
    for kk in range(TOP_K):
        rank = jnp.sum(jnp.where(hots[kk], cum, 0.0), axis=-1, keepdims=True).astype(jnp.int32)
        out_i = jnp.where(lane == kk, idxs[kk], out_i)
        out_i = jnp.where(lane == TOP_K + kk, rank, out_i)
        out_g = jnp.where(lane == kk, es[kk] / den, out_g)
    ri_ref[...] = out_i
    rg_ref[...] = out_g


def moe_router(x2, gain, modarr, sh_row, sc_row, n_lat_tiles, tiles_per_batch, router_w, router_b):
    n, d = x2.shape
    tm = ROUTER_TM
    w_pad = jnp.zeros((d, LANES), F32).at[:, :N_EXPERTS].set(router_w)
    b_pad = jnp.full((1, LANES), NEG, F32).at[0, :N_EXPERTS].set(router_b)

    def mod_idx(which):
        def f(i):
            bidx = i // tiles_per_batch
            is_ctx = (i % tiles_per_batch) >= n_lat_tiles
            return (bidx * MOD_ROWS + is_ctx.astype(jnp.int32) * CTX_ROW0 + which, 0, 0)
        return f

    return pl.pallas_call(
        _router_kernel,
        grid=(n // tm,),
        in_specs=[pl.BlockSpec((tm, d), lambda i: (i, 0)),
                  pl.BlockSpec((1, d), lambda i: (0, 0)),
                  pl.BlockSpec((1, 1, d), mod_idx(sh_row)),
                  pl.BlockSpec((1, 1, d), mod_idx(sc_row)),
                  pl.BlockSpec((d, LANES), lambda i: (0, 0)),
                  pl.BlockSpec((1, LANES), lambda i: (0, 0))],
        out_specs=[pl.BlockSpec((tm, d), lambda i: (i, 0)),
                   pl.BlockSpec((tm, LANES), lambda i: (i, 0)),
                   pl.BlockSpec((tm, LANES), lambda i: (i, 0)),
                   pl.BlockSpec((1, LANES), lambda i: (0, 0))],
        out_shape=[jax.ShapeDtypeStruct((n, d), BF16),
                   jax.ShapeDtypeStruct((n, LANES), jnp.int32),
                   jax.ShapeDtypeStruct((n, LANES), F32),
                   jax.ShapeDtypeStruct((1, LANES), F32)],
        scratch_shapes=[pltpu.VMEM((1, LANES), F32)],
        compiler_params=_cparams(("arbitrary",)),
        name="moe_router",
    )(x2, gain.reshape(1, d), modarr, modarr, w_pad, b_pad)


def _expert_kernel(be_ref, nu_ref, x_ref, wgu_ref, bgu_ref, wd_ref, bd_ref, o_ref):
    j = pl.program_id(0)

    @pl.when(j < nu_ref[0])
    def _():
        gu = jnp.dot(x_ref[...], wgu_ref[0], preferred_element_type=F32) + bgu_ref[0]
        acts = []
        for blk in range(gu.shape[1] // GU_BLK):
            gate = jnp.minimum(gu[:, blk * GU_BLK:blk * GU_BLK + LANES], SWIGLU_LIMIT)
            up = jnp.clip(gu[:, blk * GU_BLK + LANES:(blk + 1) * GU_BLK], -SWIGLU_LIMIT, SWIGLU_LIMIT)
            acts.append((gate * jax.nn.sigmoid(SWIGLU_ALPHA * gate) * (up + 1.0)).astype(BF16))
        act = jnp.concatenate(acts, axis=1)
        o_ref[...] = jnp.dot(act, wd_ref[0], preferred_element_type=F32) + bd_ref[0]

    @pl.when(j >= nu_ref[0])
    def _():
        o_ref[...] = jnp.zeros_like(o_ref)


def moe_experts(xs, block_expert, n_used, wgu, bgu, wd, bd):
    cap, d = xs.shape
    de = wd.shape[1]
    nblk = cap // MOE_TM
    grid_spec = pltpu.PrefetchScalarGridSpec(
        num_scalar_prefetch=2,
        grid=(nblk,),
        in_specs=[pl.BlockSpec((MOE_TM, d), lambda j, be, nu: (j, 0)),
                  pl.BlockSpec((1, d, 2 * de), lambda j, be, nu: (be[j], 0, 0)),
                  pl.BlockSpec((1, 1, 2 * de), lambda j, be, nu: (be[j], 0, 0)),
                  pl.BlockSpec((1, de, d), lambda j, be, nu: (be[j], 0, 0)),
                  pl.BlockSpec((1, 1, d), lambda j, be, nu: (be[j], 0, 0))],
        out_specs=pl.BlockSpec((MOE_TM, d), lambda j, be, nu: (j, 0)),
    )
    return pl.pallas_call(
        _expert_kernel,
        grid_spec=grid_spec,
        out_shape=jax.ShapeDtypeStruct((cap, d), F32),
        compiler_params=_cparams(("arbitrary",)),
        name="moe_experts",
    )(block_expert, n_used, xs, wgu, bgu, wd, bd)


def moe_ffn(x2, gain, modarr, sh_row, sc_row, n_lat_tiles, tiles_per_batch, router_w, router_b,
            wgu, bgu, wd, bd):
    n, d = x2.shape
    h_bf, r_i, r_g, cnt = moe_router(x2, gain, modarr, sh_row, sc_row, n_lat_tiles, tiles_per_batch,
                                     router_w, router_b)
    top_idx = r_i[:, :TOP_K]
    rank = r_i[:, TOP_K:2 * TOP_K]
    gates = r_g[:, :TOP_K]
    counts = cnt[0, :N_EXPERTS].astype(jnp.int32)
    padded = (counts + MOE_TM - 1) // MOE_TM * MOE_TM
    pad_end = jnp.cumsum(padded)
    pad_start = pad_end - padded
    dest = pad_start[top_idx] + rank
    nblk = -(-(n * TOP_K + N_EXPERTS * (MOE_TM - 1)) // MOE_TM)
    cap = nblk * MOE_TM
    tok = jnp.broadcast_to(jnp.arange(n, dtype=jnp.int32)[:, None], (n, TOP_K))
    slot_tok = jnp.zeros((cap,), jnp.int32).at[dest.reshape(-1)].set(tok.reshape(-1), unique_indices=True)
    block_expert = jnp.minimum(jnp.searchsorted(pad_end, jnp.arange(nblk) * MOE_TM, side="right"),
                               N_EXPERTS - 1).astype(jnp.int32)
    n_used = (pad_end[-1:] // MOE_TM).astype(jnp.int32)
    xs = h_bf[slot_tok]
    y = moe_experts(xs, block_expert, n_used, wgu, bgu, wd, bd)
    yk = y[dest]
    return jnp.sum(yk * gates[:, :, None], axis=1)


def _rms(x, g):
    return x * lax.rsqrt(jnp.mean(x * x, axis=-1, keepdims=True) + EPS) * g


def _shift_conv(x, w, n_lat):
    width = w.shape[0]
    half = width // 2

    def seg(t):
        tp = jnp.pad(t, ((0, 0), (half, half), (0, 0)))
        ln = t.shape[1]
        out = tp[:, 0:ln] * w[0]
        for kk in range(1, width):
            out = out + tp[:, kk:kk + ln] * w[kk]
        return out

    return jnp.concatenate([seg(x[:, :n_lat]), seg(x[:, n_lat:])], axis=1)


def _axial_rope(n_tok):
    pos = jnp.arange(n_tok)
    rows = (pos // GRID_W).astype(F32)
    cols = (pos % GRID_W).astype(F32)
    quarter = HEAD_DIM // 4
    inv_freq = ROPE_BASE ** (-jnp.arange(quarter, dtype=F32) / quarter)
    ar = rows[:, None] * inv_freq
    ac = cols[:, None] * inv_freq
    ang = jnp.concatenate([ar, ar, ac, ac], axis=-1)
    return jnp.cos(ang), jnp.sin(ang)


def _pick_tn(n):
    for tn in (1536, 1280, 1024, 768, 512, 256, 128):
        if n % tn == 0:
            return tn
    raise ValueError(n)


def _even_mixer(xa, mix_g, mod16, n_lat, w_in, w_out, conv_w, conv_b, a_log, dt_bias, d_skip, norm_g, sc_conv_w):
    b, l, d = xa.shape
    d_ssd = d
    gn = SSD_GROUPS * SSD_STATE
    conv_dim = d_ssd + 2 * gn
    n_in = w_in.shape[1]
    n_pad = -(-n_in // 256) * 256
    w_pad = jnp.pad(w_in, ((0, 0), (0, n_pad - n_in))).astype(BF16)
    proj = proj_modulated(xa.reshape(b * l, d), mix_g, mod16, w_pad, l, n_lat, 0, 1,
                          _pick_tn(n_pad)).reshape(b, l, n_pad)
    o = 0
    z = proj[..., o:o + d_ssd]; o += d_ssd
    xbc = proj[..., o:o + conv_dim]; o += conv_dim
    dtr = proj[..., o:o + 2 * SSD_HEADS]; o += 2 * SSD_HEADS
    gb = proj[..., o:o + d]; o += d
    gc = proj[..., o:o + d]; o += d
    gh = proj[..., o:o + d]
    xbc = jax.nn.silu(_shift_conv(xbc, conv_w, n_lat) + conv_b)
    xs = xbc[..., :d_ssd]
    bm = xbc[..., d_ssd:d_ssd + gn].astype(BF16)
    cm = xbc[..., d_ssd + gn:].astype(BF16)
    dt = jax.nn.softplus(dtr.reshape(b, l, 2, SSD_HEADS) + dt_bias)
    a_neg = -jnp.exp(a_log)
    da = jnp.concatenate([dt, dt * a_neg], axis=-1)
    da = jnp.moveaxis(da, 2, 1)
    dat = jnp.swapaxes(da, 2, 3)
    sc = gb * _shift_conv(gc * gh, sc_conv_w, n_lat)
    xs_bf = xs.astype(BF16)
    bt = jnp.swapaxes(bm, 1, 2)
    yf = ssd_scan(xs_bf, bt, bm, cm, da, dat, n_lat, rev=False)
    yb = ssd_scan(xs_bf, bt, bm, cm, da, dat, n_lat, rev=True)
    dexp = jnp.repeat(d_skip, HEAD_DIM)
    y = yf + yb + dexp * xs
    y = _rms(y * jax.nn.silu(z), norm_g)
    cat = jnp.concatenate([y, sc], axis=-1).reshape(b * l, -1).astype(BF16)
    return matmul_residual(cat, w_out.astype(BF16), xa.reshape(b * l, d), mod16, l, n_lat, 2).reshape(b, l, d)


def _odd_mixer(xa, mix_g, mod16, n_lat, w_in, w_out, na_qk_g, na_rel_bias, df_qk_g, df_lambda, df_subln_g,
               lambda_init, rope_tabs):
    b, l, d = xa.shape
    d_na = NA_HEADS * HEAD_DIM
    dq_w = DF_HEADS * 2 * HEAD_DIM
    n_ctx = l - n_lat
    reps = ODD_TN // HEAD_DIM
    ones = jnp.ones((ODD_TN,), F32)
    gains6 = jnp.stack([jnp.tile(na_qk_g[0], reps), jnp.tile(na_qk_g[1], reps), ones,
                        jnp.tile(df_qk_g[0], reps), jnp.tile(df_qk_g[1], reps), ones])[:, None, :]
    qkv = proj_odd(xa.reshape(b * l, d), mix_g, mod16, w_in.astype(BF16), gains6, *rope_tabs, l, n_lat, 0, 1)
    qkv = qkv.reshape(b, l, -1)
    lam_p = df_lambda.astype(F32)
    na_slabs = d_na // LANES
    df_slabs = dq_w // LANES
    na_lat = na_attention(qkv, na_bias_table(na_rel_bias), n_lat)
    na_ctx = flash_slabs(qkv, lam_p, "pair", n_ctx, n_ctx, n_ctx, n_ctx, n_lat // n_ctx, n_lat // n_ctx,
                         0, na_slabs, 2 * na_slabs, na_slabs)
    tk = 768 if l % 768 == 0 else 256
    dq0 = 3 * na_slabs
    df_lat = flash_slabs(qkv, lam_p, "diff", 256, tk, n_lat, l, 0, 0, dq0, dq0 + df_slabs, dq0 + 2 * df_slabs,
                         df_slabs, lambda_init)
    df_ctx = flash_slabs(qkv, lam_p, "diff", n_ctx, n_ctx, n_ctx, n_ctx, n_lat // n_ctx, n_lat // n_ctx,
                         dq0, dq0 + df_slabs, dq0 + 2 * df_slabs, df_slabs, lambda_init)
    na_o = jnp.concatenate([na_lat, na_ctx], axis=1)
    df_o = jnp.concatenate([df_lat, df_ctx], axis=1).reshape(b, l, DF_HEADS, 2 * HEAD_DIM)
    df_o = (_rms(df_o, df_subln_g) * (1.0 - lambda_init)).reshape(b, l, dq_w)
    cat = jnp.concatenate([na_o, df_o], axis=-1).reshape(b * l, -1).astype(BF16)
    return matmul_residual(cat, w_out.astype(BF16), xa.reshape(b * l, d), mod16, l, n_lat, 2).reshape(b, l, d)


def kernel(x, c, ctx, c_ctx, ada_w, ada_b, mix_norm_g, ffn_norm_g, router_w, router_b, moe_w_gu, moe_b_gu,
           moe_w_down, moe_b_down, ev_w_in, ev_w_out, ssd_conv_w, ssd_conv_b, ssd_a_log, ssd_dt_bias, ssd_d,
           ssd_norm_g, sc_conv_w, od_w_in, od_w_out, na_qk_g, na_rel_bias, df_qk_g, df_lambda, df_subln_g):
    b, s, d = x.shape
    n_ctx = ctx.shape[1]
    l = s + n_ctx
    depth = ada_w.shape[0]
    de = moe_w_down.shape[2]
    ne = moe_w_gu.shape[1]
    xa = jnp.concatenate([x, ctx], axis=1)
    is_ctx = (jnp.arange(l) >= s)[None, :, None]
    cos_l, sin_l = _axial_rope(s)
    cos_t = jnp.concatenate([cos_l, jnp.ones((n_ctx, HEAD_DIM), F32)], axis=0)
    sin_t = jnp.concatenate([sin_l, jnp.zeros((n_ctx, HEAD_DIM), F32)], axis=0)
    even_q = ((jnp.arange(HEAD_DIM) // (HEAD_DIM // 4)) % 2 == 0)[None, :]
    rope_tabs = tuple(jnp.tile(t, (1, LANES // HEAD_DIM))
                      for t in (cos_t, jnp.where(even_q, -sin_t, 0.0), jnp.where(even_q, 0.0, sin_t)))
    cond = jnp.concatenate([jax.nn.silu(c), jax.nn.silu(c_ctx)[None, :]], axis=0)
    cond_pad = jnp.zeros((16, d), F32).at[:b + 1].set(cond).astype(BF16)
    tiles_per_batch = l // ROUTER_TM
    n_lat_tiles = s // ROUTER_TM
    wgu_all = deinterleave_gate_up(moe_w_gu.reshape(depth * ne * d, 2 * de)).reshape(depth, ne, d, 2 * de)
    bgu_all = moe_b_gu.reshape(depth, ne, 2 * de // GU_BLK, LANES, 2).swapaxes(-1, -2).reshape(depth, ne, 1, 2 * de)

    for i in range(depth):
        j = i // 2
        mod = matmul(cond_pad, ada_w[i].astype(BF16), 16, 6 * d // 4)[:b + 1] + ada_b[i]
        mod6 = mod.reshape(b + 1, 6, d)
        pad2 = jnp.zeros((b, CTX_ROW0 - 6, d), F32)
        mod16 = jnp.concatenate([mod6[:b], pad2, jnp.broadcast_to(mod6[b], (b, 6, d)), pad2], axis=1)

        def tokmod(which):
            return jnp.where(is_ctx, mod16[:, CTX_ROW0 + which][:, None, :], mod16[:, which][:, None, :])

        if i % 2 == 0:
            xa = _even_mixer(xa, mix_norm_g[i], mod16, s, ev_w_in[j], ev_w_out[j], ssd_conv_w[j], ssd_conv_b[j],
                             ssd_a_log[j], ssd_dt_bias[j], ssd_d[j], ssd_norm_g[j], sc_conv_w[j])
        else:
            lambda_init = 0.8 - 0.6 * math.exp(-0.3 * i)
            xa = _odd_mixer(xa, mix_norm_g[i], mod16, s, od_w_in[j], od_w_out[j], na_qk_g[j], na_rel_bias[j],
                            df_qk_g[j], df_lambda[j], df_subln_g[j], lambda_init, rope_tabs)
        f = moe_ffn(xa.reshape(b * l, d), ffn_norm_g[i], mod16.reshape(b * MOD_ROWS, 1, d), 3, 4,
                    n_lat_tiles, tiles_per_batch, router_w[i], router_b[i],
                    wgu_all[i], bgu_all[i], moe_w_down[i].astype(BF16), moe_b_down[i][:, None, :])
        xa = xa + tokmod(5) * f.reshape(b, l, d)
    return xa[:, :s]
```

```python
import functools
import math

import jax
import jax.numpy as jnp
from jax import lax
from jax.experimental import pallas as pl
from jax.experimental.pallas import tpu as pltpu

F32 = jnp.float32
BF16 = jnp.bfloat16

GRID_W = 64
HEAD_DIM = 64
EPS = 1e-6
SSD_HEADS = 16
SSD_GROUPS = 4
SSD_STATE = 128
NA_ROWS = 8
NA_COLS = 16
NA_HEADS = 8
DF_HEADS = 4
N_EXPERTS = 32
TOP_K = 4
SWIGLU_LIMIT = 7.0
SWIGLU_ALPHA = 1.702
ROPE_BASE = 10000.0

LANES = 128
SSD_Q = 128
MOE_TM = 512
ROUTER_TM = 256
VMEM_LIMIT = 56 * 1024 * 1024
NEG = -1e30


def _cparams(sem):
    return pltpu.CompilerParams(dimension_semantics=sem, vmem_limit_bytes=VMEM_LIMIT)


def _mm_kernel(a_ref, w_ref, o_ref):
    o_ref[...] = jnp.dot(a_ref[...], w_ref[...], preferred_element_type=F32).astype(o_ref.dtype)


def matmul(a, w, tm, tn, out_dtype=F32):
    m, k = a.shape
    n = w.shape[1]
    return pl.pallas_call(
        _mm_kernel,
        grid=(n // tn, m // tm),
        in_specs=[pl.BlockSpec((tm, k), lambda j, i: (i, 0)),
                  pl.BlockSpec((k, tn), lambda j, i: (0, j))],
        out_specs=pl.BlockSpec((tm, tn), lambda j, i: (i, j)),
        out_shape=jax.ShapeDtypeStruct((m, n), out_dtype),
        compiler_params=_cparams(("parallel", "parallel")),
        name="matmul",
    )(a, w)


MOD_ROWS = 16
CTX_ROW0 = 8
ODD_TN = 512


def _row_is_ctx(i, tm, tiles_per_batch, n_lat):
    t = i % tiles_per_batch
    row = t * tm + lax.broadcasted_iota(jnp.int32, (tm, 1), 0)
    return row >= n_lat


def _mod_row(mod, is_ctx, which):
    return jnp.where(is_ctx, mod[CTX_ROW0 + which:CTX_ROW0 + which + 1], mod[which:which + 1])


def _modulated_rows(x_ref, g_ref, mod_ref, is_ctx, sh_row, sc_row):
    x = x_ref[...]
    y = x * lax.rsqrt(jnp.mean(x * x, axis=-1, keepdims=True) + EPS) * g_ref[...]
    mod = mod_ref[0]
    return y * (1.0 + _mod_row(mod, is_ctx, sc_row)) + _mod_row(mod, is_ctx, sh_row)


def _proj_kernel(x_ref, g_ref, mod_ref, w_ref, o_ref, h_ref, *, tiles_per_batch, n_lat, sh_row, sc_row):
    i = pl.program_id(0)

    @pl.when(pl.program_id(1) == 0)
    def _():
        is_ctx = _row_is_ctx(i, x_ref.shape[0], tiles_per_batch, n_lat)
        h_ref[...] = _modulated_rows(x_ref, g_ref, mod_ref, is_ctx, sh_row, sc_row).astype(BF16)

    o_ref[...] = jnp.dot(h_ref[...], w_ref[...], preferred_element_type=F32)


def _group_rms(a, gsum, gain):
    outs = []
    half = gsum.shape[0]
    for hf in range(a.shape[1] // half):
        ah = a[:, hf * half:(hf + 1) * half]
        ms = jnp.dot((ah * ah).astype(BF16), gsum, preferred_element_type=F32) * (1.0 / HEAD_DIM)
        outs.append(ah * lax.rsqrt(ms + EPS))
    return jnp.concatenate(outs, axis=1) * gain


def _proj_odd_kernel(x_ref, g_ref, mod_ref, w_ref, gain_ref, gsum_ref, cos_ref, sina_ref, sinb_ref, o_ref, h_ref, *,
                     tiles_per_batch, n_lat, sh_row, sc_row):
    i = pl.program_id(0)
    j = pl.program_id(1)

    @pl.when(j == 0)
    def _():
        is_ctx = _row_is_ctx(i, x_ref.shape[0], tiles_per_batch, n_lat)
        h_ref[...] = _modulated_rows(x_ref, g_ref, mod_ref, is_ctx, sh_row, sc_row).astype(BF16)

    acc = jnp.dot(h_ref[...], w_ref[...], preferred_element_type=F32)
    is_norm = (j == 0) | (j == 1) | (j == 3) | (j == 4)
    is_rope = (j == 3) | (j == 4)

    @pl.when(jnp.logical_not(is_norm))
    def _():
        o_ref[...] = acc.astype(BF16)

    @pl.when(is_norm & jnp.logical_not(is_rope))
    def _():
        o_ref[...] = _group_rms(acc, gsum_ref[...], gain_ref[0]).astype(BF16)

    @pl.when(is_rope)
    def _():
        xn = _group_rms(acc, gsum_ref[...], gain_ref[0])
        reps = xn.shape[1] // LANES
        cos = jnp.concatenate([cos_ref[...]] * reps, axis=1)
        sina = jnp.concatenate([sina_ref[...]] * reps, axis=1)
        sinb = jnp.concatenate([sinb_ref[...]] * reps, axis=1)
        quarter = HEAD_DIM // 4
        up = pltpu.roll(xn, xn.shape[1] - quarter, axis=1)
        dn = pltpu.roll(xn, quarter, axis=1)
        o_ref[...] = (xn * cos + up * sina + dn * sinb).astype(BF16)


def _pick_rows(l):
    for tm in (768, 512, 256):
        if l % tm == 0:
            return tm
    raise ValueError(l)


def proj_modulated(x2, gain, mod16, w_bf, l, n_lat, sh_row, sc_row, tn):
    m, d = x2.shape
    n = w_bf.shape[1]
    tm = _pick_rows(l)
    tpb = l // tm
    return pl.pallas_call(
        functools.partial(_proj_kernel, tiles_per_batch=tpb, n_lat=n_lat, sh_row=sh_row, sc_row=sc_row),
        grid=(m // tm, n // tn),
        in_specs=[pl.BlockSpec((tm, d), lambda i, j: (i, 0)),
                  pl.BlockSpec((1, d), lambda i, j: (0, 0)),
                  pl.BlockSpec((1, MOD_ROWS, d), lambda i, j: (i // tpb, 0, 0)),
                  pl.BlockSpec((d, tn), lambda i, j: (0, j))],
        out_specs=pl.BlockSpec((tm, tn), lambda i, j: (i, j)),
        out_shape=jax.ShapeDtypeStruct((m, n), F32),
        scratch_shapes=[pltpu.VMEM((tm, d), BF16)],
        compiler_params=_cparams(("parallel", "arbitrary")),
        name="proj_even",
    )(x2, gain.reshape(1, d), mod16, w_bf)


def proj_odd(x2, gain, mod16, w_bf, gains6, cos2, sina2, sinb2, l, n_lat, sh_row, sc_row):
    m, d = x2.shape
    n = w_bf.shape[1]
    tn = ODD_TN
    tm = _pick_rows(l)
    tpb = l // tm
    half = 256
    gi = jnp.arange(half) // HEAD_DIM
    gsum = (gi[:, None] == gi[None, :]).astype(BF16)
    tab = pl.BlockSpec((tm, LANES), lambda i, j: (i % tpb, 0))
    return pl.pallas_call(
        functools.partial(_proj_odd_kernel, tiles_per_batch=tpb, n_lat=n_lat, sh_row=sh_row, sc_row=sc_row),
        grid=(m // tm, n // tn),
        in_specs=[pl.BlockSpec((tm, d), lambda i, j: (i, 0)),
                  pl.BlockSpec((1, d), lambda i, j: (0, 0)),
                  pl.BlockSpec((1, MOD_ROWS, d), lambda i, j: (i // tpb, 0, 0)),
                  pl.BlockSpec((d, tn), lambda i, j: (0, j)),
                  pl.BlockSpec((1, 1, tn), lambda i, j: (j, 0, 0)),
                  pl.BlockSpec((half, half), lambda i, j: (0, 0)),
                  tab, tab, tab],
        out_specs=pl.BlockSpec((tm, tn), lambda i, j: (i, j)),
        out_shape=jax.ShapeDtypeStruct((m, n), BF16),
        scratch_shapes=[pltpu.VMEM((tm, d), BF16)],
        compiler_params=_cparams(("parallel", "arbitrary")),
        name="proj_odd",
    )(x2, gain.reshape(1, d), mod16, w_bf, gains6, gsum, cos2, sina2, sinb2)


def _mm_res_kernel(a_ref, w_ref, r_ref, mod_ref, o_ref, *, tiles_per_batch, n_lat, gate_row):
    is_ctx = _row_is_ctx(pl.program_id(0), a_ref.shape[0], tiles_per_batch, n_lat)
    gate = _mod_row(mod_ref[0], is_ctx, gate_row)
    o_ref[...] = r_ref[...] + gate * jnp.dot(a_ref[...], w_ref[...], preferred_element_type=F32)


def matmul_residual(a, w_bf, res, mod16, l, n_lat, gate_row):
    m, k = a.shape
    d = w_bf.shape[1]
    tm = _pick_rows(l)
    tpb = l // tm
    return pl.pallas_call(
        functools.partial(_mm_res_kernel, tiles_per_batch=tpb, n_lat=n_lat, gate_row=gate_row),
        grid=(m // tm,),
        in_specs=[pl.BlockSpec((tm, k), lambda i: (i, 0)),
                  pl.BlockSpec((k, d), lambda i: (0, 0)),
                  pl.BlockSpec((tm, d), lambda i: (i, 0)),
                  pl.BlockSpec((1, MOD_ROWS, d), lambda i: (i // tpb, 0, 0))],
        out_specs=pl.BlockSpec((tm, d), lambda i: (i, 0)),
        out_shape=jax.ShapeDtypeStruct((m, d), F32),
        compiler_params=_cparams(("parallel",)),
        name="out_proj",
    )(a, w_bf, res, mod16)


GU_BLK = 2 * LANES


def _deint_kernel(w_ref, p_ref, o_ref):
    p = p_ref[...]
    for blk in range(w_ref.shape[1] // GU_BLK):
        cs = slice(blk * GU_BLK, (blk + 1) * GU_BLK)
        o_ref[:, cs] = jnp.dot(w_ref[:, cs].astype(BF16), p, preferred_element_type=F32).astype(BF16)


def deinterleave_gate_up(w2):
    r, n = w2.shape
    tm = 512
    src = jnp.arange(GU_BLK)
    dst = jnp.where(src % 2 == 0, src // 2, LANES + src // 2)
    perm = (dst[:, None] == jnp.arange(GU_BLK)[None, :]).astype(BF16)
    return pl.pallas_call(
        _deint_kernel,
        grid=(r // tm,),
        in_specs=[pl.BlockSpec((tm, n), lambda i: (i, 0)),
                  pl.BlockSpec((GU_BLK, GU_BLK), lambda i: (0, 0))],
        out_specs=pl.BlockSpec((tm, n), lambda i: (i, 0)),
        out_shape=jax.ShapeDtypeStruct((r, n), BF16),
        compiler_params=_cparams(("parallel",)),
        name="deinterleave_gate_up",
    )(w2, perm)


def _split3_dot(tri, a, dims):
    a1 = a.astype(BF16)
    r1 = a - a1.astype(F32)
    a2 = r1.astype(BF16)
    a3 = (r1 - a2.astype(F32)).astype(BF16)
    out = None
    for piece in (a1, a2, a3):
        if dims == "tri_a":
            t = jnp.dot(tri, piece, preferred_element_type=F32)
        else:
            t = jnp.dot(piece, tri, preferred_element_type=F32)
        out = t if out is None else out + t
    return out


def _ssd_kernel(x_ref, bt_ref, b_ref, c_ref, da_ref, dat_ref, y_ref, st_ref, *, rev):
    q = SSD_Q
    step = pl.program_id(1)

    @pl.when(step == 0)
    def _():
        st_ref[...] = jnp.zeros_like(st_ref)

    da = da_ref[0, 0]
    dat = dat_ref[0, 0]
    a_c = da[:, SSD_HEADS:2 * SSD_HEADS]
    dt_r = dat[0:SSD_HEADS]
    a_r = dat[SSD_HEADS:2 * SSD_HEADS]
    ri = lax.broadcasted_iota(jnp.int32, (q, q), 0)
    ci = lax.broadcasted_iota(jnp.int32, (q, q), 1)
    tri = (ci <= ri).astype(BF16)
    tri_t = (ri <= ci).astype(BF16)
    cum_c = _split3_dot(tri, a_c, "tri_a")
    cum_r = _split3_dot(tri_t, a_r, "a_tri")
    tot_r = cum_r[:, q - 1:q]
    if rev:
        pos_c = cum_c - a_c
        pos_r = cum_r - a_r
        mask = ci >= ri
    else:
        pos_c = cum_c
        pos_r = cum_r
        mask = ri >= ci
    lane = lax.broadcasted_iota(jnp.int32, (q, LANES), 1)
    lane_n = lax.broadcasted_iota(jnp.int32, (SSD_STATE, LANES), 1)
    heads_per_group = SSD_HEADS // SSD_GROUPS

    for g in range(SSD_GROUPS):
        gs = slice(g * SSD_STATE, (g + 1) * SSD_STATE)
        bg = b_ref[0, :, gs]
        cg = c_ref[0, :, gs]
        btg = bt_ref[0, gs, :].astype(F32)
        cb = lax.dot_general(cg, bg, (((1,), (1,)), ((), ())), preferred_element_type=F32)
        cg32 = cg.astype(F32)
        for pp in range(heads_per_group // 2):
            p = g * (heads_per_group // 2) + pp
            xp = x_ref[0, :, p * LANES:(p + 1) * LANES]
            st = st_ref[p]
            rhs = jnp.concatenate([xp, st.astype(BF16)], axis=0)
            ys, ds, decs = [], [], []
            for hh in range(2):
                h = 2 * p + hh
                colb = jnp.broadcast_to(pos_c[:, h:h + 1], (q, LANES))
                row = pos_r[h:h + 1, :]
                dt_row = dt_r[h:h + 1, :]
                tot = tot_r[h:h + 1, :]
                if rev:
                    seg = row - colb
                    coff = jnp.exp(tot - colb)
                    w_row = dt_row * jnp.exp(row)
                else:
                    seg = colb - row
                    coff = jnp.exp(colb)
                    w_row = dt_row * jnp.exp(tot - row)
                decay = jnp.exp(jnp.where(mask, seg, NEG))
                m_h = (cb * decay * dt_row).astype(BF16)
                c_h = (cg32 * coff).astype(BF16)
                lhs = jnp.concatenate([m_h, c_h], axis=1)
                ys.append(jnp.dot(lhs, rhs, preferred_element_type=F32))
                btw = (btg * w_row).astype(BF16)
                ds.append(jnp.dot(btw, xp, preferred_element_type=F32))
                decs.append(jnp.exp(tot))
            y_ref[0, :, p * LANES:(p + 1) * LANES] = jnp.where(lane < HEAD_DIM, ys[0], ys[1])
            st_ref[p] = jnp.where(lane_n < HEAD_DIM, decs[0] * st + ds[0], decs[1] * st + ds[1])


def ssd_scan(xs, bt, bm, cm, da, dat, n_lat, rev):
    b, l, d = xs.shape
    nch = l // SSD_Q
    nlat = n_lat // SSD_Q
    nctx = nch - nlat
    d_idx = 1 if rev else 0
    if rev:
        def chunk(j):
            return nch - 1 - j
    else:
        def chunk(j):
            return jnp.where(j < nctx, nlat + j, j - nctx)
    gn = SSD_GROUPS * SSD_STATE
    return pl.pallas_call(
        functools.partial(_ssd_kernel, rev=rev),
        grid=(b, nch),
        in_specs=[pl.BlockSpec((1, SSD_Q, d), lambda i, j: (i, chunk(j), 0)),
                  pl.BlockSpec((1, gn, SSD_Q), lambda i, j: (i, 0, chunk(j))),
                  pl.BlockSpec((1, SSD_Q, gn), lambda i, j: (i, chunk(j), 0)),
                  pl.BlockSpec((1, SSD_Q, gn), lambda i, j: (i, chunk(j), 0)),
                  pl.BlockSpec((1, 1, SSD_Q, 2 * SSD_HEADS), lambda i, j: (i, d_idx, chunk(j), 0)),
                  pl.BlockSpec((1, 1, 2 * SSD_HEADS, SSD_Q), lambda i, j: (i, d_idx, 0, chunk(j)))],
        out_specs=pl.BlockSpec((1, SSD_Q, d), lambda i, j: (i, chunk(j), 0)),
        out_shape=jax.ShapeDtypeStruct((b, l, d), F32),
        scratch_shapes=[pltpu.VMEM((SSD_HEADS // 2, SSD_STATE, LANES), F32)],
        compiler_params=_cparams(("parallel", "arbitrary")),
        name="ssd_bwd" if rev else "ssd_fwd",
    )(xs, bt, bm, cm, da, dat)


def _stack_halves(qv):
    lane = lax.broadcasted_iota(jnp.int32, qv.shape, 1)
    zero = jnp.zeros_like(qv)
    q1 = jnp.where(lane < HEAD_DIM, qv, zero)
    q2 = jnp.where(lane >= HEAD_DIM, qv, zero)
    return jnp.concatenate([q1, q2], axis=0) * jnp.asarray(HEAD_DIM ** -0.5, qv.dtype)


def _flash_kernel(lam_ref, q_ref, k_ref, v_ref, o_ref, *, mode, tk, lambda_init):
    tq = q_ref.shape[1]
    nk = k_ref.shape[1] // tk
    nt = tk // LANES
    qs = _stack_halves(q_ref[0])

    def scores(u):
        return lax.dot_general(qs, k_ref[0, u * tk:(u + 1) * tk, :], (((1,), (1,)), ((), ())),
                               preferred_element_type=F32)

    m_old = jnp.full((2 * tq, LANES), -jnp.inf, F32)
    l_run = jnp.zeros((2 * tq, LANES), F32)
    acc = jnp.zeros((2 * tq, LANES), F32)
    s_next = scores(0)
    for u in range(nk):
        s = s_next
        if u + 1 < nk:
            s_next = scores(u + 1)
        tiles = [s[:, t * LANES:(t + 1) * LANES] for t in range(nt)]
        smax = tiles[0]
        for t in tiles[1:]:
            smax = jnp.maximum(smax, t)
        m_new = jnp.maximum(m_old, jnp.max(smax, axis=-1, keepdims=True))
        alpha = jnp.exp(m_old - m_new)
        ps = [jnp.exp(t - m_new) for t in tiles]
        psum = ps[0]
        for t in ps[1:]:
            psum = psum + t
        l_run = alpha * l_run + jnp.sum(psum, axis=-1, keepdims=True)
        p = jnp.concatenate([t.astype(BF16) for t in ps], axis=1)
        acc = alpha * acc + jnp.dot(p, v_ref[0, u * tk:(u + 1) * tk, :], preferred_element_type=F32)
        m_old = m_new
    o = acc / l_run
    o1, o2 = o[:tq], o[tq:]
    if mode == "diff":
        lp = lam_ref[...]
        s01 = jnp.sum(lp[0:1] * lp[1:2], axis=-1, keepdims=True)
        s23 = jnp.sum(lp[2:3] * lp[3:4], axis=-1, keepdims=True)
        lam = jnp.exp(s01) - jnp.exp(s23) + lambda_init
        out = o1 - lam * o2
    else:
        lane = lax.broadcasted_iota(jnp.int32, o1.shape, 1)
        out = jnp.where(lane < HEAD_DIM, o1, o2)
    o_ref[0] = out.astype(o_ref.dtype)


def flash_slabs(qkv, lam_p, mode, tq, tk, lq, lk, q_blk0, k_blk, q_slab0, k_slab0, v_slab0, n_slabs,
                lambda_init=0.0):
    b = qkv.shape[0]
    return pl.pallas_call(
        functools.partial(_flash_kernel, mode=mode, tk=tk, lambda_init=lambda_init),
        grid=(b, n_slabs, lq // tq),
        in_specs=[pl.BlockSpec(lam_p.shape, lambda i, h, j: (0, 0)),
                  pl.BlockSpec((1, tq, LANES), lambda i, h, j: (i, q_blk0 + j, q_slab0 + h)),
                  pl.BlockSpec((1, lk, LANES), lambda i, h, j: (i, k_blk, k_slab0 + h)),
                  pl.BlockSpec((1, lk, LANES), lambda i, h, j: (i, k_blk, v_slab0 + h))],
        out_specs=pl.BlockSpec((1, tq, LANES), lambda i, h, j: (i, j, h)),
        out_shape=jax.ShapeDtypeStruct((b, lq, n_slabs * LANES), F32),
        compiler_params=_cparams(("parallel", "parallel", "arbitrary")),
        name="flash_" + mode,
    )(lam_p, qkv, qkv, qkv)


NA_RB = 8
NA_BLK = NA_RB * GRID_W
NA_WIN = NA_ROWS * GRID_W


def _na_kernel(q_ref, kp_ref, kc_ref, kn_ref, vp_ref, vc_ref, vn_ref, kx_ref, vx_ref, bias_ref, o_ref,
               kbuf, vbuf, *, rows):
    rb = pl.program_id(1)
    kbuf[0:NA_BLK] = kp_ref[0]
    kbuf[NA_BLK:2 * NA_BLK] = kc_ref[0]
    kbuf[2 * NA_BLK:3 * NA_BLK] = kn_ref[0]
    vbuf[0:NA_BLK] = vp_ref[0]
    vbuf[NA_BLK:2 * NA_BLK] = vc_ref[0]
    vbuf[2 * NA_BLK:3 * NA_BLK] = vn_ref[0]
    npairs = q_ref.shape[2] // LANES
    lane = lax.broadcasted_iota(jnp.int32, (GRID_W, LANES), 1)

    def row_body(rl, carry):
        r = rb * NA_RB + rl
        r_start = jnp.clip(r - NA_ROWS // 2, 0, rows - NA_ROWS)
        off = r_start - (rb * NA_RB - NA_RB)
        di0 = r_start - r + NA_ROWS - 1
        tok0 = pl.multiple_of(off * GRID_W, GRID_W)
        q0 = pl.multiple_of(rl * GRID_W, GRID_W)
        for p in range(npairs):
            ls = slice(p * LANES, (p + 1) * LANES)
            qs = _stack_halves(q_ref[0, pl.ds(q0, GRID_W), ls])
            kw = kbuf[pl.ds(tok0, NA_WIN), ls]
            vw = vbuf[pl.ds(tok0, NA_WIN), ls]
            s_nb = lax.dot_general(qs, kw, (((1,), (1,)), ((), ())), preferred_element_type=F32)
            bias = jnp.concatenate([bias_ref[p, di0 + 2 * j] for j in range(NA_ROWS // 2)], axis=1)
            s_nb = s_nb + bias
            s_cx = lax.dot_general(qs, kx_ref[0, :, ls], (((1,), (1,)), ((), ())), preferred_element_type=F32)
            m = jnp.maximum(jnp.max(s_nb, axis=-1, keepdims=True), jnp.max(s_cx, axis=-1, keepdims=True))
            p_nb = jnp.exp(s_nb - m)
            p_cx = jnp.exp(s_cx - m)
            l = jnp.sum(p_nb, axis=-1, keepdims=True) + jnp.sum(p_cx, axis=-1, keepdims=True)
            o = (jnp.dot(p_nb.astype(BF16), vw, preferred_element_type=F32)
                 + jnp.dot(p_cx.astype(BF16), vx_ref[0, :, ls], preferred_element_type=F32)) / l
            o_ref[0, pl.ds(q0, GRID_W), ls] = jnp.where(lane < HEAD_DIM, o[:GRID_W], o[GRID_W:])
        return carry

    lax.fori_loop(0, NA_RB, row_body, 0)


def na_bias_table(rel_bias):
    cols = jnp.arange(GRID_W)
    c_start = jnp.clip(cols - NA_COLS // 2, 0, GRID_W - NA_COLS)
    kc = jnp.arange(GRID_W)
    valid = (kc[None, :] >= c_start[:, None]) & (kc[None, :] < c_start[:, None] + NA_COLS)
    idx = jnp.clip(kc[None, :] - cols[:, None] + NA_COLS - 1, 0, 2 * NA_COLS - 2)
    t = jnp.where(valid[None, None], rel_bias[:, :, idx], NEG)
    t2 = jnp.concatenate([t[:, :-1], t[:, 1:]], axis=-1)
    nh, nd = t2.shape[0], t2.shape[1]
    t2 = t2.reshape(nh // 2, 2, nd, GRID_W, LANES).transpose(0, 2, 1, 3, 4)
    return t2.reshape(nh // 2, nd, 2 * GRID_W, LANES).astype(F32)


def na_attention(qkv, bias_tab, n_lat):
    b, l, _ = qkv.shape
    w = NA_HEADS * HEAD_DIM
    rows = n_lat // GRID_W
    nrb = rows // NA_RB
    n_ctx = l - n_lat
    ctx_blk = n_lat // n_ctx

    def blk(step, stream):
        if step < 0:
            return pl.BlockSpec((1, NA_BLK, w), lambda i, j: (i, jnp.maximum(j - 1, 0), stream))
        if step > 0:
            return pl.BlockSpec((1, NA_BLK, w), lambda i, j: (i, jnp.minimum(j + 1, nrb - 1), stream))
        return pl.BlockSpec((1, NA_BLK, w), lambda i, j: (i, j, stream))

    def ctx(stream):
        return pl.BlockSpec((1, n_ctx, w), lambda i, j: (i, ctx_blk, stream))

    return pl.pallas_call(
        functools.partial(_na_kernel, rows=rows),
        grid=(b, nrb),
        in_specs=[blk(0, 0), blk(-1, 1), blk(0, 1), blk(1, 1), blk(-1, 2), blk(0, 2), blk(1, 2), ctx(1), ctx(2),
                  pl.BlockSpec(bias_tab.shape, lambda i, j: (0, 0, 0, 0))],
        out_specs=pl.BlockSpec((1, NA_BLK, w), lambda i, j: (i, j, 0)),
        out_shape=jax.ShapeDtypeStruct((b, n_lat, w), F32),
        scratch_shapes=[pltpu.VMEM((3 * NA_BLK, w), BF16), pltpu.VMEM((3 * NA_BLK, w), BF16)],
        compiler_params=_cparams(("parallel", "parallel")),
        name="na_attention",
    )(qkv, qkv, qkv, qkv, qkv, qkv, qkv, qkv, qkv, bias_tab)


def _router_kernel(x_ref, g_ref, sh_ref, sc_ref, w_ref, b_ref, h_ref, ri_ref, rg_ref, cnt_ref, carry_ref):
    step = pl.program_id(0)

    @pl.when(step == 0)
    def _():
        carry_ref[...] = jnp.zeros_like(carry_ref)

    tm = x_ref.shape[0]
    x = x_ref[...]
    y = x * lax.rsqrt(jnp.mean(x * x, axis=-1, keepdims=True) + EPS) * g_ref[...]
    h = y * (1.0 + sc_ref[0]) + sh_ref[0]
    h_ref[...] = h.astype(BF16)
    w = w_ref[...]
    h1 = h.astype(BF16)
    h2 = (h - h1.astype(F32)).astype(BF16)
    w1 = w.astype(BF16)
    w2 = (w - w1.astype(F32)).astype(BF16)
    logits = (jnp.dot(h1, w1, preferred_element_type=F32) + jnp.dot(h2, w1, preferred_element_type=F32)
              + jnp.dot(h1, w2, preferred_element_type=F32)) + b_ref[...]
    lane = lax.broadcasted_iota(jnp.int32, (tm, LANES), 1)
    work = logits
    tops, idxs, hots = [], [], []
    for _ in range(TOP_K):
        mx = jnp.max(work, axis=-1, keepdims=True)
        ix = jnp.min(jnp.where(work == mx, lane, LANES), axis=-1, keepdims=True)
        hot = lane == ix
        work = jnp.where(hot, -jnp.inf, work)
        tops.append(mx)
        idxs.append(ix)
        hots.append(hot)
    es = [jnp.exp(t - tops[0]) for t in tops]
    den = es[0] + es[1] + es[2] + es[3]
    multi = jnp.zeros((tm, LANES), F32)
    for hot in hots:
        multi = multi + hot.astype(F32)
    ri = lax.broadcasted_iota(jnp.int32, (tm, tm), 0)
    ci = lax.broadcasted_iota(jnp.int32, (tm, tm), 1)
    tri = (ci < ri).astype(BF16)
    cum = jnp.dot(tri, multi.astype(BF16), preferred_element_type=F32) + carry_ref[...]
    carry_new = carry_ref[...] + jnp.sum(multi, axis=0, keepdims=True)
    carry_ref[...] = carry_new
    cnt_ref[...] = carry_new
    out_i = jnp.zeros((tm, LANES), jnp.int32)
    out_g = jnp.zeros((tm, LANES), F32)
    for kk in range(TOP_K):
        rank = jnp.sum(jnp.where(hots[kk], cum, 0.0), axis=-1, keepdims=True).astype(jnp.int32)
        out_i = jnp.where(lane == kk, idxs[kk], out_i)
        out_i = jnp.where(lane == TOP_K + kk, rank, out_i)
        out_g = jnp.where(lane == kk, es[kk] / den, out_g)
    ri_ref[...] = out_i
    rg_ref[...] = out_g


def moe_router(x2, gain, modarr, sh_row, sc_row, n_lat_tiles, tiles_per_batch, router_w, router_b):
    n, d = x2.shape
    tm = ROUTER_TM
    w_pad = jnp.zeros((d, LANES), F32).at[:, :N_EXPERTS].set(router_w)
    b_pad = jnp.full((1, LANES), NEG, F32).at[0, :N_EXPERTS].set(router_b)

    def mod_idx(which):
        def f(i):
            bidx = i // tiles_per_batch
            is_ctx = (i % tiles_per_batch) >= n_lat_tiles
            return (bidx * MOD_ROWS + is_ctx.astype(jnp.int32) * CTX_ROW0 + which, 0, 0)
        return f

    return pl.pallas_call(
        _router_kernel,
        grid=(n // tm,),
        in_specs=[pl.BlockSpec((tm, d), lambda i: (i, 0)),
                  pl.BlockSpec((1, d), lambda i: (0, 0)),
                  pl.BlockSpec((1, 1, d), mod_idx(sh_row)),
                  pl.BlockSpec((1, 1, d), mod_idx(sc_row)),
                  pl.BlockSpec((d, LANES), lambda i: (0, 0)),
                  pl.BlockSpec((1, LANES), lambda i: (0, 0))],
        out_specs=[pl.BlockSpec((tm, d), lambda i: (i, 0)),
                   pl.BlockSpec((tm, LANES), lambda i: (i, 0)),
                   pl.BlockSpec((tm, LANES), lambda i: (i, 0)),
                   pl.BlockSpec((1, LANES), lambda i: (0, 0))],
        out_shape=[jax.ShapeDtypeStruct((n, d), BF16),
                   jax.ShapeDtypeStruct((n, LANES), jnp.int32),
                   jax.ShapeDtypeStruct((n, LANES), F32),
                   jax.ShapeDtypeStruct((1, LANES), F32)],
        scratch_shapes=[pltpu.VMEM((1, LANES), F32)],
        compiler_params=_cparams(("arbitrary",)),
        name="moe_router",
    )(x2, gain.reshape(1, d), modarr, modarr, w_pad, b_pad)


def _expert_kernel(be_ref, nu_ref, x_ref, wgu_ref, bgu_ref, wd_ref, bd_ref, o_ref, wd_bf):
    j = pl.program_id(0)
    used = j < nu_ref[0]
    changed = (j == 0) | (be_ref[j] != be_ref[jnp.maximum(j - 1, 0)])

    @pl.when(used & changed)
    def _():
        wd_bf[...] = wd_ref[0].astype(BF16)

    @pl.when(used)
    def _():
        gu = jnp.dot(x_ref[...], wgu_ref[0], preferred_element_type=F32) + bgu_ref[0]
        acts = []
        for blk in range(gu.shape[1] // GU_BLK):
            gate = jnp.minimum(gu[:, blk * GU_BLK:blk * GU_BLK + LANES], SWIGLU_LIMIT)
            up = jnp.clip(gu[:, blk * GU_BLK + LANES:(blk + 1) * GU_BLK], -SWIGLU_LIMIT, SWIGLU_LIMIT)
            acts.append((gate * jax.nn.sigmoid(SWIGLU_ALPHA * gate) * (up + 1.0)).astype(BF16))
        act = jnp.concatenate(acts, axis=1)
        o_ref[...] = jnp.dot(act, wd_bf[...], preferred_element_type=F32) + bd_ref[0]

    @pl.when(jnp.logical_not(used))
    def _():
        o_ref[...] = jnp.zeros_like(o_ref)


def moe_experts(xs, block_expert, n_used, wgu, bgu, wd, bd, e_off):
    cap, d = xs.shape
    de = wd.shape[1]
    nblk = cap // MOE_TM
    wmap = lambda j, be, nu: (e_off + be[j], 0, 0)
    grid_spec = pltpu.PrefetchScalarGridSpec(
        num_scalar_prefetch=2,
        grid=(nblk,),
        in_specs=[pl.BlockSpec((MOE_TM, d), lambda j, be, nu: (j, 0)),
                  pl.BlockSpec((1, d, 2 * de), wmap),
                  pl.BlockSpec((1, 1, 2 * de), wmap),
                  pl.BlockSpec((1, de, d), wmap),
                  pl.BlockSpec((1, 1, d), wmap)],
        out_specs=pl.BlockSpec((MOE_TM, d), lambda j, be, nu: (j, 0)),
        scratch_shapes=[pltpu.VMEM((de, d), BF16)],
    )
    return pl.pallas_call(
        _expert_kernel,
        grid_spec=grid_spec,
        out_shape=jax.ShapeDtypeStruct((cap, d), F32),
        compiler_params=_cparams(("arbitrary",)),
        name="moe_experts",
    )(block_expert, n_used, xs, wgu, bgu, wd, bd)


def moe_ffn(x2, gain, modarr, sh_row, sc_row, n_lat_tiles, tiles_per_batch, router_w, router_b,
            wgu, bgu, wd, bd, e_off):
    n, d = x2.shape
    h_bf, r_i, r_g, cnt = moe_router(x2, gain, modarr, sh_row, sc_row, n_lat_tiles, tiles_per_batch,
                                     router_w, router_b)
    top_idx = r_i[:, :TOP_K]
    rank = r_i[:, TOP_K:2 * TOP_K]
    gates = r_g[:, :TOP_K]
    counts = cnt[0, :N_EXPERTS].astype(jnp.int32)
    padded = (counts + MOE_TM - 1) // MOE_TM * MOE_TM
    pad_end = jnp.cumsum(padded)
    pad_start = pad_end - padded
    dest = (pad_start[top_idx] + rank).T
    nblk = -(-(n * TOP_K + N_EXPERTS * (MOE_TM - 1)) // MOE_TM)
    cap = nblk * MOE_TM
    tok = jnp.broadcast_to(jnp.arange(n, dtype=jnp.int32)[None, :], (TOP_K, n))
    slot_tok = jnp.zeros((cap,), jnp.int32).at[dest.reshape(-1)].set(tok.reshape(-1), unique_indices=True)
    block_expert = jnp.minimum(jnp.searchsorted(pad_end, jnp.arange(nblk) * MOE_TM, side="right"),
                               N_EXPERTS - 1).astype(jnp.int32)
    n_used = (pad_end[-1:] // MOE_TM).astype(jnp.int32)
    xs = h_bf[slot_tok]
    y = moe_experts(xs, block_expert, n_used, wgu, bgu, wd, bd, e_off)
    yk = y[dest]
    return jnp.sum(yk * gates.T[:, :, None], axis=0)


def _rms(x, g):
    return x * lax.rsqrt(jnp.mean(x * x, axis=-1, keepdims=True) + EPS) * g


def _shift_conv(x, w, n_lat):
    width = w.shape[0]
    half = width // 2

    def seg(t):
        tp = jnp.pad(t, ((0, 0), (half, half), (0, 0)))
        ln = t.shape[1]
        out = tp[:, 0:ln] * w[0]
        for kk in range(1, width):
            out = out + tp[:, kk:kk + ln] * w[kk]
        return out

    return jnp.concatenate([seg(x[:, :n_lat]), seg(x[:, n_lat:])], axis=1)


def _axial_rope(n_tok):
    pos = jnp.arange(n_tok)
    rows = (pos // GRID_W).astype(F32)
    cols = (pos % GRID_W).astype(F32)
    quarter = HEAD_DIM // 4
    inv_freq = ROPE_BASE ** (-jnp.arange(quarter, dtype=F32) / quarter)
    ar = rows[:, None] * inv_freq
    ac = cols[:, None] * inv_freq
    ang = jnp.concatenate([ar, ar, ac, ac], axis=-1)
    return jnp.cos(ang), jnp.sin(ang)


def _pick_tn(n):
    for tn in (1536, 1280, 1024, 768, 512, 256, 128):
        if n % tn == 0:
            return tn
    raise ValueError(n)


def _even_mixer(xa, mix_g, mod16, n_lat, w_in, w_out, conv_w, conv_b, a_log, dt_bias, d_skip, norm_g, sc_conv_w):
    b, l, d = xa.shape
    d_ssd = d
    gn = SSD_GROUPS * SSD_STATE
    conv_dim = d_ssd + 2 * gn
    n_in = w_in.shape[1]
    n_pad = -(-n_in // 256) * 256
    w_pad = jnp.pad(w_in, ((0, 0), (0, n_pad - n_in))).astype(BF16)
    proj = proj_modulated(xa.reshape(b * l, d), mix_g, mod16, w_pad, l, n_lat, 0, 1,
                          _pick_tn(n_pad)).reshape(b, l, n_pad)
    o = 0
    z = proj[..., o:o + d_ssd]; o += d_ssd
    xbc = proj[..., o:o + conv_dim]; o += conv_dim
    dtr = proj[..., o:o + 2 * SSD_HEADS]; o += 2 * SSD_HEADS
    gb = proj[..., o:o + d]; o += d
    gc = proj[..., o:o + d]; o += d
    gh = proj[..., o:o + d]
    xbc = jax.nn.silu(_shift_conv(xbc, conv_w, n_lat) + conv_b)
    xs = xbc[..., :d_ssd]
    bm = xbc[..., d_ssd:d_ssd + gn].astype(BF16)
    cm = xbc[..., d_ssd + gn:].astype(BF16)
    dt = jax.nn.softplus(dtr.reshape(b, l, 2, SSD_HEADS) + dt_bias)
    a_neg = -jnp.exp(a_log)
    da = jnp.concatenate([dt, dt * a_neg], axis=-1)
    da = jnp.moveaxis(da, 2, 1)
    dat = jnp.swapaxes(da, 2, 3)
    sc = gb * _shift_conv(gc * gh, sc_conv_w, n_lat)
    xs_bf = xs.astype(BF16)
    bt = jnp.swapaxes(bm, 1, 2)
    yf = ssd_scan(xs_bf, bt, bm, cm, da, dat, n_lat, rev=False)
    yb = ssd_scan(xs_bf, bt, bm, cm, da, dat, n_lat, rev=True)
    dexp = jnp.repeat(d_skip, HEAD_DIM)
    y = yf + yb + dexp * xs
    y = _rms(y * jax.nn.silu(z), norm_g)
    cat = jnp.concatenate([y, sc], axis=-1).reshape(b * l, -1).astype(BF16)
    return matmul_residual(cat, w_out.astype(BF16), xa.reshape(b * l, d), mod16, l, n_lat, 2).reshape(b, l, d)


def _odd_mixer(xa, mix_g, mod16, n_lat, w_in, w_out, na_qk_g, na_rel_bias, df_qk_g, df_lambda, df_subln_g,
               lambda_init, rope_tabs):
    b, l, d = xa.shape
    d_na = NA_HEADS * HEAD_DIM
    dq_w = DF_HEADS * 2 * HEAD_DIM
    n_ctx = l - n_lat
    reps = ODD_TN // HEAD_DIM
    ones = jnp.ones((ODD_TN,), F32)
    gains6 = jnp.stack([jnp.tile(na_qk_g[0], reps), jnp.tile(na_qk_g[1], reps), ones,
                        jnp.tile(df_qk_g[0], reps), jnp.tile(df_qk_g[1], reps), ones])[:, None, :]
    qkv = proj_odd(xa.reshape(b * l, d), mix_g, mod16, w_in.astype(BF16), gains6, *rope_tabs, l, n_lat, 0, 1)
    qkv = qkv.reshape(b, l, -1)
    lam_p = df_lambda.astype(F32)
    na_slabs = d_na // LANES
    df_slabs = dq_w // LANES
    na_lat = na_attention(qkv, na_bias_table(na_rel_bias), n_lat)
    na_ctx = flash_slabs(qkv, lam_p, "pair", n_ctx, n_ctx, n_ctx, n_ctx, n_lat // n_ctx, n_lat // n_ctx,
                         0, na_slabs, 2 * na_slabs, na_slabs)
    tk = next(t for t in (1408, 1280, 1024, 768, 512, 256) if l % t == 0)
    dq0 = 3 * na_slabs
    df_lat = flash_slabs(qkv, lam_p, "diff", 256, tk, n_lat, l, 0, 0, dq0, dq0 + df_slabs, dq0 + 2 * df_slabs,
                         df_slabs, lambda_init)
    df_ctx = flash_slabs(qkv, lam_p, "diff", n_ctx, n_ctx, n_ctx, n_ctx, n_lat // n_ctx, n_lat // n_ctx,
                         dq0, dq0 + df_slabs, dq0 + 2 * df_slabs, df_slabs, lambda_init)
    na_o = jnp.concatenate([na_lat, na_ctx], axis=1)
    df_o = jnp.concatenate([df_lat, df_ctx], axis=1).reshape(b, l, DF_HEADS, 2 * HEAD_DIM)
    df_o = (_rms(df_o, df_subln_g) * (1.0 - lambda_init)).reshape(b, l, dq_w)
    cat = jnp.concatenate([na_o, df_o], axis=-1).reshape(b * l, -1).astype(BF16)
    return matmul_residual(cat, w_out.astype(BF16), xa.reshape(b * l, d), mod16, l, n_lat, 2).reshape(b, l, d)


def kernel(x, c, ctx, c_ctx, ada_w, ada_b, mix_norm_g, ffn_norm_g, router_w, router_b, moe_w_gu, moe_b_gu,
           moe_w_down, moe_b_down, ev_w_in, ev_w_out, ssd_conv_w, ssd_conv_b, ssd_a_log, ssd_dt_bias, ssd_d,
           ssd_norm_g, sc_conv_w, od_w_in, od_w_out, na_qk_g, na_rel_bias, df_qk_g, df_lambda, df_subln_g):
    b, s, d = x.shape
    n_ctx = ctx.shape[1]
    l = s + n_ctx
    depth = ada_w.shape[0]
    de = moe_w_down.shape[2]
    ne = moe_w_gu.shape[1]
    xa = jnp.concatenate([x, ctx], axis=1)
    is_ctx = (jnp.arange(l) >= s)[None, :, None]
    cos_l, sin_l = _axial_rope(s)
    cos_t = jnp.concatenate([cos_l, jnp.ones((n_ctx, HEAD_DIM), F32)], axis=0)
    sin_t = jnp.concatenate([sin_l, jnp.zeros((n_ctx, HEAD_DIM), F32)], axis=0)
    even_q = ((jnp.arange(HEAD_DIM) // (HEAD_DIM // 4)) % 2 == 0)[None, :]
    rope_tabs = tuple(jnp.tile(t, (1, LANES // HEAD_DIM))
                      for t in (cos_t, jnp.where(even_q, -sin_t, 0.0), jnp.where(even_q, 0.0, sin_t)))
    cond = jnp.concatenate([jax.nn.silu(c), jax.nn.silu(c_ctx)[None, :]], axis=0)
    cond_pad = jnp.zeros((16, d), F32).at[:b + 1].set(cond).astype(BF16)
    tiles_per_batch = l // ROUTER_TM
    n_lat_tiles = s // ROUTER_TM
    wgu_all = deinterleave_gate_up(moe_w_gu.reshape(depth * ne * d, 2 * de)).reshape(depth * ne, d, 2 * de)
    bgu_all = moe_b_gu.reshape(depth * ne, 2 * de // GU_BLK, LANES, 2).swapaxes(-1, -2).reshape(depth * ne, 1, 2 * de)
    wd_all = moe_w_down.reshape(depth * ne, de, d)
    bd_all = moe_b_down.reshape(depth * ne, 1, d)

    for i in range(depth):
        j = i // 2
        mod = matmul(cond_pad, ada_w[i].astype(BF16), 16, 6 * d // 4)[:b + 1] + ada_b[i]
        mod6 = mod.reshape(b + 1, 6, d)
        pad2 = jnp.zeros((b, CTX_ROW0 - 6, d), F32)
        mod16 = jnp.concatenate([mod6[:b], pad2, jnp.broadcast_to(mod6[b], (b, 6, d)), pad2], axis=1)

        def tokmod(which):
            return jnp.where(is_ctx, mod16[:, CTX_ROW0 + which][:, None, :], mod16[:, which][:, None, :])

        if i % 2 == 0:
            xa = _even_mixer(xa, mix_norm_g[i], mod16, s, ev_w_in[j], ev_w_out[j], ssd_conv_w[j], ssd_conv_b[j],
                             ssd_a_log[j], ssd_dt_bias[j], ssd_d[j], ssd_norm_g[j], sc_conv_w[j])
        else:
            lambda_init = 0.8 - 0.6 * math.exp(-0.3 * i)
            xa = _odd_mixer(xa, mix_norm_g[i], mod16, s, od_w_in[j], od_w_out[j], na_qk_g[j], na_rel_bias[j],
                            df_qk_g[j], df_lambda[j], df_subln_g[j], lambda_init, rope_tabs)
        f = moe_ffn(xa.reshape(b * l, d), ffn_norm_g[i], mod16.reshape(b * MOD_ROWS, 1, d), 3, 4,
                    n_lat_tiles, tiles_per_batch, router_w[i], router_b[i],
                    wgu_all, bgu_all, wd_all, bd_all, i * ne)
        xa = xa + tokmod(5) * f.reshape(b, l, d)
    return xa[:, :s]
```

```python
import functools
import math

import jax
import jax.numpy as jnp
from jax import lax
from jax.experimental import pallas as pl
from jax.experimental.pallas import tpu as pltpu

F32 = jnp.float32
BF16 = jnp.bfloat16

GRID_W = 64
HEAD_DIM = 64
EPS = 1e-6
SSD_HEADS = 16
SSD_GROUPS = 4
SSD_STATE = 128
NA_ROWS = 8
NA_COLS = 16
NA_HEADS = 8
DF_HEADS = 4
N_EXPERTS = 32
TOP_K = 4
SWIGLU_LIMIT = 7.0
SWIGLU_ALPHA = 1.702
ROPE_BASE = 10000.0

LANES = 128
SSD_Q = 128
MOE_TM = 512
ROUTER_TM = 256
VMEM_LIMIT = 56 * 1024 * 1024
NEG = -1e30


def _cparams(sem):
    return pltpu.CompilerParams(dimension_semantics=sem, vmem_limit_bytes=VMEM_LIMIT)


def _mm_kernel(a_ref, w_ref, o_ref):
    o_ref[...] = jnp.dot(a_ref[...], w_ref[...], preferred_element_type=F32).astype(o_ref.dtype)


def matmul(a, w, tm, tn, out_dtype=F32):
    m, k = a.shape
    n = w.shape[1]
    return pl.pallas_call(
        _mm_kernel,
        grid=(n // tn, m // tm),
        in_specs=[pl.BlockSpec((tm, k), lambda j, i: (i, 0)),
                  pl.BlockSpec((k, tn), lambda j, i: (0, j))],
        out_specs=pl.BlockSpec((tm, tn), lambda j, i: (i, j)),
        out_shape=jax.ShapeDtypeStruct((m, n), out_dtype),
        compiler_params=_cparams(("parallel", "parallel")),
        name="matmul",
    )(a, w)


MOD_ROWS = 16
CTX_ROW0 = 8
ODD_TN = 512


def _row_is_ctx(i, tm, tiles_per_batch, n_lat):
    t = i % tiles_per_batch
    row = t * tm + lax.broadcasted_iota(jnp.int32, (tm, 1), 0)
    return row >= n_lat


def _mod_row(mod, is_ctx, which):
    return jnp.where(is_ctx, mod[CTX_ROW0 + which:CTX_ROW0 + which + 1], mod[which:which + 1])


def _modulated_rows(x_ref, g_ref, mod_ref, is_ctx, sh_row, sc_row):
    x = x_ref[...]
    y = x * lax.rsqrt(jnp.mean(x * x, axis=-1, keepdims=True) + EPS) * g_ref[...]
    mod = mod_ref[0]
    return y * (1.0 + _mod_row(mod, is_ctx, sc_row)) + _mod_row(mod, is_ctx, sh_row)


def _proj_kernel(x_ref, g_ref, mod_ref, w_ref, o_ref, h_ref, *, tiles_per_batch, n_lat, sh_row, sc_row):
    i = pl.program_id(0)

    @pl.when(pl.program_id(1) == 0)
    def _():
        is_ctx = _row_is_ctx(i, x_ref.shape[0], tiles_per_batch, n_lat)
        h_ref[...] = _modulated_rows(x_ref, g_ref, mod_ref, is_ctx, sh_row, sc_row).astype(BF16)

    o_ref[...] = jnp.dot(h_ref[...], w_ref[...], preferred_element_type=F32)


def _group_rms(a, gsum, gain):
    outs = []
    half = gsum.shape[0]
    for hf in range(a.shape[1] // half):
        ah = a[:, hf * half:(hf + 1) * half]
        ms = jnp.dot((ah * ah).astype(BF16), gsum, preferred_element_type=F32) * (1.0 / HEAD_DIM)
        outs.append(ah * lax.rsqrt(ms + EPS))
    return jnp.concatenate(outs, axis=1) * gain


def _proj_odd_kernel(x_ref, g_ref, mod_ref, w_ref, gain_ref, gsum_ref, cos_ref, sina_ref, sinb_ref, o_ref, h_ref, *,
                     tiles_per_batch, n_lat, sh_row, sc_row):
    i = pl.program_id(0)
    j = pl.program_id(1)

    @pl.when(j == 0)
    def _():
        is_ctx = _row_is_ctx(i, x_ref.shape[0], tiles_per_batch, n_lat)
        h_ref[...] = _modulated_rows(x_ref, g_ref, mod_ref, is_ctx, sh_row, sc_row).astype(BF16)

    acc = jnp.dot(h_ref[...], w_ref[...], preferred_element_type=F32)
    is_norm = (j == 0) | (j == 1) | (j == 3) | (j == 4)
    is_rope = (j == 3) | (j == 4)

    @pl.when(jnp.logical_not(is_norm))
    def _():
        o_ref[...] = acc.astype(BF16)

    @pl.when(is_norm & jnp.logical_not(is_rope))
    def _():
        o_ref[...] = _group_rms(acc, gsum_ref[...], gain_ref[0]).astype(BF16)

    @pl.when(is_rope)
    def _():
        xn = _group_rms(acc, gsum_ref[...], gain_ref[0])
        reps = xn.shape[1] // LANES
        cos = jnp.concatenate([cos_ref[...]] * reps, axis=1)
        sina = jnp.concatenate([sina_ref[...]] * reps, axis=1)
        sinb = jnp.concatenate([sinb_ref[...]] * reps, axis=1)
        quarter = HEAD_DIM // 4
        up = pltpu.roll(xn, xn.shape[1] - quarter, axis=1)
        dn = pltpu.roll(xn, quarter, axis=1)
        o_ref[...] = (xn * cos + up * sina + dn * sinb).astype(BF16)


def _pick_rows(l):
    for tm in (768, 512, 256):
        if l % tm == 0:
            return tm
    raise ValueError(l)


def proj_modulated(x2, gain, mod16, w_bf, l, n_lat, sh_row, sc_row, tn):
    m, d = x2.shape
    n = w_bf.shape[1]
    tm = _pick_rows(l)
    tpb = l // tm
    return pl.pallas_call(
        functools.partial(_proj_kernel, tiles_per_batch=tpb, n_lat=n_lat, sh_row=sh_row, sc_row=sc_row),
        grid=(m // tm, n // tn),
        in_specs=[pl.BlockSpec((tm, d), lambda i, j: (i, 0)),
                  pl.BlockSpec((1, d), lambda i, j: (0, 0)),
                  pl.BlockSpec((1, MOD_ROWS, d), lambda i, j: (i // tpb, 0, 0)),
                  pl.BlockSpec((d, tn), lambda i, j: (0, j))],
        out_specs=pl.BlockSpec((tm, tn), lambda i, j: (i, j)),
        out_shape=jax.ShapeDtypeStruct((m, n), F32),
        scratch_shapes=[pltpu.VMEM((tm, d), BF16)],
        compiler_params=_cparams(("parallel", "arbitrary")),
        name="proj_even",
    )(x2, gain.reshape(1, d), mod16, w_bf)


def proj_odd(x2, gain, mod16, w_bf, gains6, cos2, sina2, sinb2, l, n_lat, sh_row, sc_row):
    m, d = x2.shape
    n = w_bf.shape[1]
    tn = ODD_TN
    tm = _pick_rows(l)
    tpb = l // tm
    half = 256
    gi = jnp.arange(half) // HEAD_DIM
    gsum = (gi[:, None] == gi[None, :]).astype(BF16)
    tab = pl.BlockSpec((tm, LANES), lambda i, j: (i % tpb, 0))
    return pl.pallas_call(
        functools.partial(_proj_odd_kernel, tiles_per_batch=tpb, n_lat=n_lat, sh_row=sh_row, sc_row=sc_row),
        grid=(m // tm, n // tn),
        in_specs=[pl.BlockSpec((tm, d), lambda i, j: (i, 0)),
                  pl.BlockSpec((1, d), lambda i, j: (0, 0)),
                  pl.BlockSpec((1, MOD_ROWS, d), lambda i, j: (i // tpb, 0, 0)),
                  pl.BlockSpec((d, tn), lambda i, j: (0, j)),
                  pl.BlockSpec((1, 1, tn), lambda i, j: (j, 0, 0)),
                  pl.BlockSpec((half, half), lambda i, j: (0, 0)),
                  tab, tab, tab],
        out_specs=pl.BlockSpec((tm, tn), lambda i, j: (i, j)),
        out_shape=jax.ShapeDtypeStruct((m, n), BF16),
        scratch_shapes=[pltpu.VMEM((tm, d), BF16)],
        compiler_params=_cparams(("parallel", "arbitrary")),
        name="proj_odd",
    )(x2, gain.reshape(1, d), mod16, w_bf, gains6, gsum, cos2, sina2, sinb2)


def _mm_res_kernel(*refs, n_parts, tiles_per_batch, n_lat, gate_row):
    a_refs = refs[:n_parts]
    w_ref, r_ref, mod_ref, o_ref = refs[n_parts:]
    is_ctx = _row_is_ctx(pl.program_id(0), r_ref.shape[0], tiles_per_batch, n_lat)
    gate = _mod_row(mod_ref[0], is_ctx, gate_row)
    acc = None
    k0 = 0
    for a_ref in a_refs:
        k1 = k0 + a_ref.shape[1]
        t = jnp.dot(a_ref[...], w_ref[k0:k1, :], preferred_element_type=F32)
        acc = t if acc is None else acc + t
        k0 = k1
    o_ref[...] = r_ref[...] + gate * acc


def matmul_residual(parts, w_bf, res, mod16, l, n_lat, gate_row):
    m = res.shape[0]
    k, d = w_bf.shape
    tm = _pick_rows(l)
    tpb = l // tm
    return pl.pallas_call(
        functools.partial(_mm_res_kernel, n_parts=len(parts), tiles_per_batch=tpb, n_lat=n_lat, gate_row=gate_row),
        grid=(m // tm,),
        in_specs=[pl.BlockSpec((tm, a.shape[1]), lambda i: (i, 0)) for a in parts]
                 + [pl.BlockSpec((k, d), lambda i: (0, 0)),
                    pl.BlockSpec((tm, d), lambda i: (i, 0)),
                    pl.BlockSpec((1, MOD_ROWS, d), lambda i: (i // tpb, 0, 0))],
        out_specs=pl.BlockSpec((tm, d), lambda i: (i, 0)),
        out_shape=jax.ShapeDtypeStruct((m, d), F32),
        compiler_params=_cparams(("parallel",)),
        name="out_proj",
    )(*parts, w_bf, res, mod16)


EV_TM = 256
HALO = 8


def _even_prep_kernel(xa_ref, xa_p, xa_n, xb_ref, xb_p, xb_n, gb_ref, gc_ref, gc_p, gc_n, gh_ref, gh_p, gh_n,
                      cw_ref, cb_ref, sw_ref, xs_ref, bm_ref, cm_ref, sc_ref, scr, *, tiles_per_batch, n_lat, l):
    tm = xa_ref.shape[0]
    r0 = (pl.program_id(0) % tiles_per_batch) * tm
    has_prev = (r0 != 0) & (r0 != n_lat)
    has_next = (r0 + tm != n_lat) & (r0 + tm != l)

    def conv(main, prev, nxt, w):
        width = w.shape[0]
        scr[0:HALO] = jnp.where(has_prev, prev, 0.0)
        scr[HALO:HALO + tm] = main
        scr[HALO + tm:2 * HALO + tm] = jnp.where(has_next, nxt, 0.0)
        out = None
        for kk in range(width):
            o = HALO + kk - width // 2
            t = scr[o:o + tm] * w[kk:kk + 1]
            out = t if out is None else out + t
        return out

    d = xa_ref.shape[1]
    cw = cw_ref[...]
    cb = cb_ref[...]
    xs = conv(xa_ref[...], xa_p[...], xa_n[...], cw[:, :d]) + cb[:, :d]
    xs_ref[...] = (xs * jax.nn.sigmoid(xs)).astype(BF16)
    bc = conv(xb_ref[...], xb_p[...], xb_n[...], cw[:, d:]) + cb[:, d:]
    bc = (bc * jax.nn.sigmoid(bc)).astype(BF16)
    gn = bm_ref.shape[1]
    bm_ref[...] = bc[:, :gn]
    cm_ref[...] = bc[:, gn:]
    sc = gb_ref[...] * conv(gc_ref[...] * gh_ref[...], gc_p[...] * gh_p[...], gc_n[...] * gh_n[...], sw_ref[...])
    sc_ref[...] = sc.astype(BF16)


def even_prep(proj2, conv_w, conv_b, sc_conv_w, l, n_lat, d):
    m = proj2.shape[0]
    tm = EV_TM
    tpb = l // tm
    nhb = m // HALO
    gn = SSD_GROUPS * SSD_STATE

    def main(cblk):
        return pl.BlockSpec((tm, d), lambda i: (i, cblk))

    def prev(cblk):
        return pl.BlockSpec((HALO, d), lambda i: (jnp.maximum(i * (tm // HALO) - 1, 0), cblk))

    def nxt(cblk):
        return pl.BlockSpec((HALO, d), lambda i: (jnp.minimum((i + 1) * (tm // HALO), nhb - 1), cblk))

    full = lambda a: pl.BlockSpec(a.shape, lambda i: (0, 0))
    cb2 = conv_b.reshape(1, -1)
    return pl.pallas_call(
        functools.partial(_even_prep_kernel, tiles_per_batch=tpb, n_lat=n_lat, l=l),
        grid=(m // tm,),
        in_specs=[main(1), prev(1), nxt(1), main(2), prev(2), nxt(2), main(3),
                  main(4), prev(4), nxt(4), main(5), prev(5), nxt(5),
                  full(conv_w), full(cb2), full(sc_conv_w)],
        out_specs=[pl.BlockSpec((tm, d), lambda i: (i, 0)), pl.BlockSpec((tm, gn), lambda i: (i, 0)),
                   pl.BlockSpec((tm, gn), lambda i: (i, 0)), pl.BlockSpec((tm, d), lambda i: (i, 0))],
        out_shape=[jax.ShapeDtypeStruct((m, d), BF16), jax.ShapeDtypeStruct((m, gn), BF16),
                   jax.ShapeDtypeStruct((m, gn), BF16), jax.ShapeDtypeStruct((m, d), BF16)],
        scratch_shapes=[pltpu.VMEM((tm + 2 * HALO, d), F32)],
        compiler_params=_cparams(("parallel",)),
        name="even_prep",
    )(proj2, proj2, proj2, proj2, proj2, proj2, proj2, proj2, proj2, proj2, proj2, proj2, proj2,
      conv_w, cb2, sc_conv_w)


GU_BLK = 2 * LANES


def _gate_up_perm():
    src = jnp.arange(GU_BLK)
    dst = jnp.where(src % 2 == 0, src // 2, LANES + src // 2)
    return (dst[:, None] == jnp.arange(GU_BLK)[None, :]).astype(BF16)


def _deinterleave_into(w_ref, perm, o_ref):
    for blk in range(o_ref.shape[1] // GU_BLK):
        cs = slice(blk * GU_BLK, (blk + 1) * GU_BLK)
        o_ref[:, cs] = jnp.dot(w_ref[0, :, cs].astype(BF16), perm, preferred_element_type=F32).astype(BF16)


def _split3_dot(tri, a, dims):
    a1 = a.astype(BF16)
    r1 = a - a1.astype(F32)
    a2 = r1.astype(BF16)
    a3 = (r1 - a2.astype(F32)).astype(BF16)
    out = None
    for piece in (a1, a2, a3):
        if dims == "tri_a":
            t = jnp.dot(tri, piece, preferred_element_type=F32)
        else:
            t = jnp.dot(piece, tri, preferred_element_type=F32)
        out = t if out is None else out + t
    return out


def _ssd_kernel(*refs, rev, post):
    if post:
        x_ref, bt_ref, b_ref, c_ref, da_ref, dat_ref, yo_ref, z_ref, dsk_ref, ng_ref, y_ref, st_ref, yacc = refs
    else:
        x_ref, bt_ref, b_ref, c_ref, da_ref, dat_ref, y_ref, st_ref = refs
    q = SSD_Q
    step = pl.program_id(1)

    @pl.when(step == 0)
    def _():
        st_ref[...] = jnp.zeros_like(st_ref)

    da = da_ref[0, 0]
    dat = dat_ref[0, 0]
    a_c = da[:, SSD_HEADS:2 * SSD_HEADS]
    dt_r = dat[0:SSD_HEADS]
    a_r = dat[SSD_HEADS:2 * SSD_HEADS]
    ri = lax.broadcasted_iota(jnp.int32, (q, q), 0)
    ci = lax.broadcasted_iota(jnp.int32, (q, q), 1)
    tri = (ci <= ri).astype(BF16)
    tri_t = (ri <= ci).astype(BF16)
    cum_c = _split3_dot(tri, a_c, "tri_a")
    cum_r = _split3_dot(tri_t, a_r, "a_tri")
    tot_r = cum_r[:, q - 1:q]
    if rev:
        pos_c = cum_c - a_c
        pos_r = cum_r - a_r
        mask = ci >= ri
    else:
        pos_c = cum_c
        pos_r = cum_r
        mask = ri >= ci
    lane = lax.broadcasted_iota(jnp.int32, (q, LANES), 1)
    lane_n = lax.broadcasted_iota(jnp.int32, (SSD_STATE, LANES), 1)
    heads_per_group = SSD_HEADS // SSD_GROUPS

    for g in range(SSD_GROUPS):
        gs = slice(g * SSD_STATE, (g + 1) * SSD_STATE)
        bg = b_ref[0, :, gs]
        cg = c_ref[0, :, gs]
        btg = bt_ref[0, gs, :].astype(F32)
        cb = lax.dot_general(cg, bg, (((1,), (1,)), ((), ())), preferred_element_type=F32)
        cg32 = cg.astype(F32)
        for pp in range(heads_per_group // 2):
            p = g * (heads_per_group // 2) + pp
            xp = x_ref[0, :, p * LANES:(p + 1) * LANES]
            st = st_ref[p]
            rhs = jnp.concatenate([xp, st.astype(BF16)], axis=0)
            ys, ds, decs = [], [], []
            for hh in range(2):
                h = 2 * p + hh
                colb = jnp.broadcast_to(pos_c[:, h:h + 1], (q, LANES))
                row = pos_r[h:h + 1, :]
                dt_row = dt_r[h:h + 1, :]
                tot = tot_r[h:h + 1, :]
                if rev:
                    seg = row - colb
                    coff = jnp.exp(tot - colb)
                    w_row = dt_row * jnp.exp(row)
                else:
                    seg = colb - row
                    coff = jnp.exp(colb)
                    w_row = dt_row * jnp.exp(tot - row)
                decay = jnp.exp(jnp.where(mask, seg, NEG))
                m_h = (cb * decay * dt_row).astype(BF16)
                c_h = (cg32 * coff).astype(BF16)
                lhs = jnp.concatenate([m_h, c_h], axis=1)
                ys.append(jnp.dot(lhs, rhs, preferred_element_type=F32))
                btw = (btg * w_row).astype(BF16)
                ds.append(jnp.dot(btw, xp, preferred_element_type=F32))
                decs.append(jnp.exp(tot))
            y_pair = jnp.where(lane < HEAD_DIM, ys[0], ys[1])
            if post:
                yacc[:, p * LANES:(p + 1) * LANES] = y_pair
            else:
                y_ref[0, :, p * LANES:(p + 1) * LANES] = y_pair
            st_ref[p] = jnp.where(lane_n < HEAD_DIM, decs[0] * st + ds[0], decs[1] * st + ds[1])

    if post:
        z = z_ref[...]
        y = (yacc[...] + yo_ref[0] + dsk_ref[...] * x_ref[0].astype(F32)) * (z * jax.nn.sigmoid(z))
        y = y * lax.rsqrt(jnp.mean(y * y, axis=-1, keepdims=True) + EPS) * ng_ref[...]
        y_ref[0] = y.astype(y_ref.dtype)


def ssd_scan(xs, bt, bm, cm, da, dat, n_lat, rev, post=None):
    b, l, d = xs.shape
    nch = l // SSD_Q
    nlat = n_lat // SSD_Q
    nctx = nch - nlat
    d_idx = 1 if rev else 0
    if rev:
        def chunk(j):
            return nch - 1 - j
    else:
        def chunk(j):
            return jnp.where(j < nctx, nlat + j, j - nctx)
    gn = SSD_GROUPS * SSD_STATE
    tok = pl.BlockSpec((1, SSD_Q, d), lambda i, j: (i, chunk(j), 0))
    in_specs = [tok,
                pl.BlockSpec((1, gn, SSD_Q), lambda i, j: (i, 0, chunk(j))),
                pl.BlockSpec((1, SSD_Q, gn), lambda i, j: (i, chunk(j), 0)),
                pl.BlockSpec((1, SSD_Q, gn), lambda i, j: (i, chunk(j), 0)),
                pl.BlockSpec((1, 1, SSD_Q, 2 * SSD_HEADS), lambda i, j: (i, d_idx, chunk(j), 0)),
                pl.BlockSpec((1, 1, 2 * SSD_HEADS, SSD_Q), lambda i, j: (i, d_idx, 0, chunk(j)))]
    scratch = [pltpu.VMEM((SSD_HEADS // 2, SSD_STATE, LANES), F32)]
    args = (xs, bt, bm, cm, da, dat)
    if post is not None:
        row = pl.BlockSpec((1, d), lambda i, j: (0, 0))
        in_specs += [tok, pl.BlockSpec((SSD_Q, d), lambda i, j: (i * nch + chunk(j), 0)), row, row]
        scratch.append(pltpu.VMEM((SSD_Q, d), F32))
        args += tuple(post)
    return pl.pallas_call(
        functools.partial(_ssd_kernel, rev=rev, post=post is not None),
        grid=(b, nch),
        in_specs=in_specs,
        out_specs=tok,
        out_shape=jax.ShapeDtypeStruct((b, l, d), F32 if post is None else BF16),
        scratch_shapes=scratch,
        compiler_params=_cparams(("parallel", "arbitrary")),
        name="ssd_bwd" if rev else "ssd_fwd",
    )(*args)


def _stack_halves(qv):
    lane = lax.broadcasted_iota(jnp.int32, qv.shape, 1)
    zero = jnp.zeros_like(qv)
    q1 = jnp.where(lane < HEAD_DIM, qv, zero)
    q2 = jnp.where(lane >= HEAD_DIM, qv, zero)
    return jnp.concatenate([q1, q2], axis=0) * jnp.asarray(HEAD_DIM ** -0.5, qv.dtype)


def _flash_kernel(lam_ref, q_ref, k_ref, v_ref, o_ref, *, mode, tk, lambda_init):
    tq = q_ref.shape[1]
    nk = k_ref.shape[1] // tk
    nt = tk // LANES
    qs = _stack_halves(q_ref[0])

    def scores(u):
        return lax.dot_general(qs, k_ref[0, u * tk:(u + 1) * tk, :], (((1,), (1,)), ((), ())),
                               preferred_element_type=F32)

    m_old = jnp.full((2 * tq, LANES), -jnp.inf, F32)
    l_run = jnp.zeros((2 * tq, LANES), F32)
    acc = jnp.zeros((2 * tq, LANES), F32)
    s_next = scores(0)
    for u in range(nk):
        s = s_next
        if u + 1 < nk:
            s_next = scores(u + 1)
        tiles = [s[:, t * LANES:(t + 1) * LANES] for t in range(nt)]
        smax = tiles[0]
        for t in tiles[1:]:
            smax = jnp.maximum(smax, t)
        m_new = jnp.maximum(m_old, jnp.max(smax, axis=-1, keepdims=True))
        alpha = jnp.exp(m_old - m_new)
        ps = [jnp.exp(t - m_new) for t in tiles]
        psum = ps[0]
        for t in ps[1:]:
            psum = psum + t
        l_run = alpha * l_run + jnp.sum(psum, axis=-1, keepdims=True)
        p = jnp.concatenate([t.astype(BF16) for t in ps], axis=1)
        acc = alpha * acc + jnp.dot(p, v_ref[0, u * tk:(u + 1) * tk, :], preferred_element_type=F32)
        m_old = m_new
    o = acc / l_run
    o1, o2 = o[:tq], o[tq:]
    if mode == "diff":
        lp = lam_ref[...]
        s01 = jnp.sum(lp[0:1] * lp[1:2], axis=-1, keepdims=True)
        s23 = jnp.sum(lp[2:3] * lp[3:4], axis=-1, keepdims=True)
        lam = jnp.exp(s01) - jnp.exp(s23) + lambda_init
        out = o1 - lam * o2
    else:
        lane = lax.broadcasted_iota(jnp.int32, o1.shape, 1)
        out = jnp.where(lane < HEAD_DIM, o1, o2)
    o_ref[0] = out.astype(o_ref.dtype)


def flash_slabs(qkv, lam_p, mode, tq, tk, lq, lk, q_blk0, k_blk, q_slab0, k_slab0, v_slab0, n_slabs,
                lambda_init=0.0):
    b = qkv.shape[0]
    return pl.pallas_call(
        functools.partial(_flash_kernel, mode=mode, tk=tk, lambda_init=lambda_init),
        grid=(b, n_slabs, lq // tq),
        in_specs=[pl.BlockSpec(lam_p.shape, lambda i, h, j: (0, 0)),
                  pl.BlockSpec((1, tq, LANES), lambda i, h, j: (i, q_blk0 + j, q_slab0 + h)),
                  pl.BlockSpec((1, lk, LANES), lambda i, h, j: (i, k_blk, k_slab0 + h)),
                  pl.BlockSpec((1, lk, LANES), lambda i, h, j: (i, k_blk, v_slab0 + h))],
        out_specs=pl.BlockSpec((1, tq, LANES), lambda i, h, j: (i, j, h)),
        out_shape=jax.ShapeDtypeStruct((b, lq, n_slabs * LANES), F32),
        compiler_params=_cparams(("parallel", "parallel", "arbitrary")),
        name="flash_" + mode,
    )(lam_p, qkv, qkv, qkv)


NA_RB = 8
NA_BLK = NA_RB * GRID_W
NA_WIN = NA_ROWS * GRID_W


def _na_kernel(q_ref, kp_ref, kc_ref, kn_ref, vp_ref, vc_ref, vn_ref, kx_ref, vx_ref, bias_ref, o_ref,
               kbuf, vbuf, *, rows):
    rb = pl.program_id(1)
    kbuf[0:NA_BLK] = kp_ref[0]
    kbuf[NA_BLK:2 * NA_BLK] = kc_ref[0]
    kbuf[2 * NA_BLK:3 * NA_BLK] = kn_ref[0]
    vbuf[0:NA_BLK] = vp_ref[0]
    vbuf[NA_BLK:2 * NA_BLK] = vc_ref[0]
    vbuf[2 * NA_BLK:3 * NA_BLK] = vn_ref[0]
    npairs = q_ref.shape[2] // LANES
    lane = lax.broadcasted_iota(jnp.int32, (GRID_W, LANES), 1)

    def row_body(rl, carry):
        r = rb * NA_RB + rl
        r_start = jnp.clip(r - NA_ROWS // 2, 0, rows - NA_ROWS)
        off = r_start - (rb * NA_RB - NA_RB)
        di0 = r_start - r + NA_ROWS - 1
        tok0 = pl.multiple_of(off * GRID_W, GRID_W)
        q0 = pl.multiple_of(rl * GRID_W, GRID_W)
        for p in range(npairs):
            ls = slice(p * LANES, (p + 1) * LANES)
            qs = _stack_halves(q_ref[0, pl.ds(q0, GRID_W), ls])
            kw = kbuf[pl.ds(tok0, NA_WIN), ls]
            vw = vbuf[pl.ds(tok0, NA_WIN), ls]
            s_nb = lax.dot_general(qs, kw, (((1,), (1,)), ((), ())), preferred_element_type=F32)
            bias = jnp.concatenate([bias_ref[p, di0 + 2 * j] for j in range(NA_ROWS // 2)], axis=1)
            s_nb = s_nb + bias
            s_cx = lax.dot_general(qs, kx_ref[0, :, ls], (((1,), (1,)), ((), ())), preferred_element_type=F32)
            m = jnp.maximum(jnp.max(s_nb, axis=-1, keepdims=True), jnp.max(s_cx, axis=-1, keepdims=True))
            p_nb = jnp.exp(s_nb - m)
            p_cx = jnp.exp(s_cx - m)
            l = jnp.sum(p_nb, axis=-1, keepdims=True) + jnp.sum(p_cx, axis=-1, keepdims=True)
            o = (jnp.dot(p_nb.astype(BF16), vw, preferred_element_type=F32)
                 + jnp.dot(p_cx.astype(BF16), vx_ref[0, :, ls], preferred_element_type=F32)) / l
            o_ref[0, pl.ds(q0, GRID_W), ls] = jnp.where(lane < HEAD_DIM, o[:GRID_W], o[GRID_W:])
        return carry

    lax.fori_loop(0, NA_RB, row_body, 0)


def na_bias_table(rel_bias):
    cols = jnp.arange(GRID_W)
    c_start = jnp.clip(cols - NA_COLS // 2, 0, GRID_W - NA_COLS)
    kc = jnp.arange(GRID_W)
    valid = (kc[None, :] >= c_start[:, None]) & (kc[None, :] < c_start[:, None] + NA_COLS)
    idx = jnp.clip(kc[None, :] - cols[:, None] + NA_COLS - 1, 0, 2 * NA_COLS - 2)
    t = jnp.where(valid[None, None], rel_bias[:, :, idx], NEG)
    t2 = jnp.concatenate([t[:, :-1], t[:, 1:]], axis=-1)
    nh, nd = t2.shape[0], t2.shape[1]
    t2 = t2.reshape(nh // 2, 2, nd, GRID_W, LANES).transpose(0, 2, 1, 3, 4)
    return t2.reshape(nh // 2, nd, 2 * GRID_W, LANES).astype(F32)


def na_attention(qkv, bias_tab, n_lat):
    b, l, _ = qkv.shape
    w = NA_HEADS * HEAD_DIM
    rows = n_lat // GRID_W
    nrb = rows // NA_RB
    n_ctx = l - n_lat
    ctx_blk = n_lat // n_ctx

    def blk(step, stream):
        if step < 0:
            return pl.BlockSpec((1, NA_BLK, w), lambda i, j: (i, jnp.maximum(j - 1, 0), stream))
        if step > 0:
            return pl.BlockSpec((1, NA_BLK, w), lambda i, j: (i, jnp.minimum(j + 1, nrb - 1), stream))
        return pl.BlockSpec((1, NA_BLK, w), lambda i, j: (i, j, stream))

    def ctx(stream):
        return pl.BlockSpec((1, n_ctx, w), lambda i, j: (i, ctx_blk, stream))

    return pl.pallas_call(
        functools.partial(_na_kernel, rows=rows),
        grid=(b, nrb),
        in_specs=[blk(0, 0), blk(-1, 1), blk(0, 1), blk(1, 1), blk(-1, 2), blk(0, 2), blk(1, 2), ctx(1), ctx(2),
                  pl.BlockSpec(bias_tab.shape, lambda i, j: (0, 0, 0, 0))],
        out_specs=pl.BlockSpec((1, NA_BLK, w), lambda i, j: (i, j, 0)),
        out_shape=jax.ShapeDtypeStruct((b, n_lat, w), F32),
        scratch_shapes=[pltpu.VMEM((3 * NA_BLK, w), BF16), pltpu.VMEM((3 * NA_BLK, w), BF16)],
        compiler_params=_cparams(("parallel", "parallel")),
        name="na_attention",
    )(qkv, qkv, qkv, qkv, qkv, qkv, qkv, qkv, qkv, bias_tab)


def _router_kernel(x_ref, g_ref, sh_ref, sc_ref, w_ref, b_ref, h_ref, ri_ref, rg_ref, cnt_ref, carry_ref):
    step = pl.program_id(0)

    @pl.when(step == 0)
    def _():
        carry_ref[...] = jnp.zeros_like(carry_ref)

    tm = x_ref.shape[0]
    x = x_ref[...]
    y = x * lax.rsqrt(jnp.mean(x * x, axis=-1, keepdims=True) + EPS) * g_ref[...]
    h = y * (1.0 + sc_ref[0]) + sh_ref[0]
    h_ref[...] = h.astype(BF16)
    w = w_ref[...]
    h1 = h.astype(BF16)
    h2 = (h - h1.astype(F32)).astype(BF16)
    w1 = w.astype(BF16)
    w2 = (w - w1.astype(F32)).astype(BF16)
    logits = (jnp.dot(h1, w1, preferred_element_type=F32) + jnp.dot(h2, w1, preferred_element_type=F32)
              + jnp.dot(h1, w2, preferred_element_type=F32)) + b_ref[...]
    lane = lax.broadcasted_iota(jnp.int32, (tm, LANES), 1)
    work = logits
    tops, idxs, hots = [], [], []
    for _ in range(TOP_K):
        mx = jnp.max(work, axis=-1, keepdims=True)
        ix = jnp.min(jnp.where(work == mx, lane, LANES), axis=-1, keepdims=True)
        hot = lane == ix
        work = jnp.where(hot, -jnp.inf, work)
        tops.append(mx)
        idxs.append(ix)
        hots.append(hot)
    es = [jnp.exp(t - tops[0]) for t in tops]
    den = es[0] + es[1] + es[2] + es[3]
    multi = jnp.zeros((tm, LANES), F32)
    for hot in hots:
        multi = multi + hot.astype(F32)
    ri = lax.broadcasted_iota(jnp.int32, (tm, tm), 0)
    ci = lax.broadcasted_iota(jnp.int32, (tm, tm), 1)
    tri = (ci < ri).astype(BF16)
    cum = jnp.dot(tri, multi.astype(BF16), preferred_element_type=F32) + carry_ref[...]
    carry_new = carry_ref[...] + jnp.sum(multi, axis=0, keepdims=True)
    carry_ref[...] = carry_new
    cnt_ref[...] = carry_new
    out_i = jnp.zeros((tm, LANES), jnp.int32)
    out_g = jnp.zeros((tm, LANES), F32)
    for kk in range(TOP_K):
        rank = jnp.sum(jnp.where(hots[kk], cum, 0.0), axis=-1, keepdims=True).astype(jnp.int32)
        out_i = jnp.where(lane == kk, idxs[kk], out_i)
        out_i = jnp.where(lane == TOP_K + kk, rank, out_i)
        out_g = jnp.where(lane == kk, es[kk] / den, out_g)
    ri_ref[...] = out_i
    rg_ref[...] = out_g


def moe_router(x2, gain, modarr, sh_row, sc_row, n_lat_tiles, tiles_per_batch, router_w, router_b):
    n, d = x2.shape
    tm = ROUTER_TM
    w_pad = jnp.zeros((d, LANES), F32).at[:, :N_EXPERTS].set(router_w)
    b_pad = jnp.full((1, LANES), NEG, F32).at[0, :N_EXPERTS].set(router_b)

    def mod_idx(which):
        def f(i):
            bidx = i // tiles_per_batch
            is_ctx = (i % tiles_per_batch) >= n_lat_tiles
            return (bidx * MOD_ROWS + is_ctx.astype(jnp.int32) * CTX_ROW0 + which, 0, 0)
        return f

    return pl.pallas_call(
        _router_kernel,
        grid=(n // tm,),
        in_specs=[pl.BlockSpec((tm, d), lambda i: (i, 0)),
                  pl.BlockSpec((1, d), lambda i: (0, 0)),
                  pl.BlockSpec((1, 1, d), mod_idx(sh_row)),
                  pl.BlockSpec((1, 1, d), mod_idx(sc_row)),
                  pl.BlockSpec((d, LANES), lambda i: (0, 0)),
                  pl.BlockSpec((1, LANES), lambda i: (0, 0))],
        out_specs=[pl.BlockSpec((tm, d), lambda i: (i, 0)),
                   pl.BlockSpec((tm, LANES), lambda i: (i, 0)),
                   pl.BlockSpec((tm, LANES), lambda i: (i, 0)),
                   pl.BlockSpec((1, LANES), lambda i: (0, 0))],
        out_shape=[jax.ShapeDtypeStruct((n, d), BF16),
                   jax.ShapeDtypeStruct((n, LANES), jnp.int32),
                   jax.ShapeDtypeStruct((n, LANES), F32),
                   jax.ShapeDtypeStruct((1, LANES), F32)],
        scratch_shapes=[pltpu.VMEM((1, LANES), F32)],
        compiler_params=_cparams(("arbitrary",)),
        name="moe_router",
    )(x2, gain.reshape(1, d), modarr, modarr, w_pad, b_pad)


def _expert_kernel(be_ref, nu_ref, x_ref, wgu_ref, bgu_ref, wd_ref, bd_ref, perm_ref, o_ref, wgu_bf, wd_bf):
    j = pl.program_id(0)
    used = j < nu_ref[0]
    changed = (j == 0) | (be_ref[j] != be_ref[jnp.maximum(j - 1, 0)])

    @pl.when(used & changed)
    def _():
        wd_bf[...] = wd_ref[0].astype(BF16)
        _deinterleave_into(wgu_ref, perm_ref[...], wgu_bf)

    @pl.when(used)
    def _():
        gu = jnp.dot(x_ref[...], wgu_bf[...], preferred_element_type=F32) + bgu_ref[0]
        acts = []
        for blk in range(gu.shape[1] // GU_BLK):
            gate = jnp.minimum(gu[:, blk * GU_BLK:blk * GU_BLK + LANES], SWIGLU_LIMIT)
            up = jnp.clip(gu[:, blk * GU_BLK + LANES:(blk + 1) * GU_BLK], -SWIGLU_LIMIT, SWIGLU_LIMIT)
            acts.append((gate * jax.nn.sigmoid(SWIGLU_ALPHA * gate) * (up + 1.0)).astype(BF16))
        act = jnp.concatenate(acts, axis=1)
        o_ref[...] = (jnp.dot(act, wd_bf[...], preferred_element_type=F32) + bd_ref[0]).astype(o_ref.dtype)

    @pl.when(jnp.logical_not(used))
    def _():
        o_ref[...] = jnp.zeros_like(o_ref)


def moe_experts(xs, block_expert, n_used, wgu, bgu, wd, bd, e_off):
    cap, d = xs.shape
    de = wd.shape[1]
    nblk = cap // MOE_TM
    wmap = lambda j, be, nu: (e_off + be[j], 0, 0)
    grid_spec = pltpu.PrefetchScalarGridSpec(
        num_scalar_prefetch=2,
        grid=(nblk,),
        in_specs=[pl.BlockSpec((MOE_TM, d), lambda j, be, nu: (j, 0)),
                  pl.BlockSpec((1, d, 2 * de), wmap),
                  pl.BlockSpec((1, 1, 2 * de), wmap),
                  pl.BlockSpec((1, de, d), wmap),
                  pl.BlockSpec((1, 1, d), wmap),
                  pl.BlockSpec((GU_BLK, GU_BLK), lambda j, be, nu: (0, 0))],
        out_specs=pl.BlockSpec((MOE_TM, d), lambda j, be, nu: (j, 0)),
        scratch_shapes=[pltpu.VMEM((d, 2 * de), BF16), pltpu.VMEM((de, d), BF16)],
    )
    return pl.pallas_call(
        _expert_kernel,
        grid_spec=grid_spec,
        out_shape=jax.ShapeDtypeStruct((cap, d), BF16),
        compiler_params=_cparams(("arbitrary",)),
        name="moe_experts",
    )(block_expert, n_used, xs, wgu, bgu, wd, bd, _gate_up_perm())


def moe_ffn(x2, gain, modarr, sh_row, sc_row, n_lat_tiles, tiles_per_batch, router_w, router_b,
            wgu, bgu, wd, bd, e_off):
    n, d = x2.shape
    h_bf, r_i, r_g, cnt = moe_router(x2, gain, modarr, sh_row, sc_row, n_lat_tiles, tiles_per_batch,
                                     router_w, router_b)
    top_idx = r_i[:, :TOP_K]
    rank = r_i[:, TOP_K:2 * TOP_K]
    gates = r_g[:, :TOP_K]
    counts = cnt[0, :N_EXPERTS].astype(jnp.int32)
    padded = (counts + MOE_TM - 1) // MOE_TM * MOE_TM
    pad_end = jnp.cumsum(padded)
    pad_start = pad_end - padded
    dest = (pad_start[top_idx] + rank).T
    nblk = -(-(n * TOP_K + N_EXPERTS * (MOE_TM - 1)) // MOE_TM)
    cap = nblk * MOE_TM
    tok = jnp.broadcast_to(jnp.arange(n, dtype=jnp.int32)[None, :], (TOP_K, n))
    slot_tok = jnp.zeros((cap,), jnp.int32).at[dest.reshape(-1)].set(tok.reshape(-1), unique_indices=True)
    block_expert = jnp.minimum(jnp.searchsorted(pad_end, jnp.arange(nblk) * MOE_TM, side="right"),
                               N_EXPERTS - 1).astype(jnp.int32)
    n_used = (pad_end[-1:] // MOE_TM).astype(jnp.int32)
    xs = h_bf[slot_tok]
    y = moe_experts(xs, block_expert, n_used, wgu, bgu, wd, bd, e_off)
    yk = y[dest]
    return jnp.sum(yk.astype(F32) * gates.T[:, :, None], axis=0)


def _rms(x, g):
    return x * lax.rsqrt(jnp.mean(x * x, axis=-1, keepdims=True) + EPS) * g


def _axial_rope(n_tok):
    pos = jnp.arange(n_tok)
    rows = (pos // GRID_W).astype(F32)
    cols = (pos % GRID_W).astype(F32)
    quarter = HEAD_DIM // 4
    inv_freq = ROPE_BASE ** (-jnp.arange(quarter, dtype=F32) / quarter)
    ar = rows[:, None] * inv_freq
    ac = cols[:, None] * inv_freq
    ang = jnp.concatenate([ar, ar, ac, ac], axis=-1)
    return jnp.cos(ang), jnp.sin(ang)


def _pick_tn(n):
    for tn in (1536, 1280, 1024, 768, 512, 256, 128):
        if n % tn == 0:
            return tn
    raise ValueError(n)


def _even_mixer(xa, mix_g, mod16, n_lat, w_in, w_out, conv_w, conv_b, a_log, dt_bias, d_skip, norm_g, sc_conv_w):
    b, l, d = xa.shape
    gn = SSD_GROUPS * SSD_STATE
    conv_dim = d + 2 * gn
    n_in = w_in.shape[1]
    n_pad = -(-n_in // 256) * 256
    o_dt = d + conv_dim
    w_perm = jnp.concatenate([w_in[:, :o_dt], w_in[:, o_dt + 2 * SSD_HEADS:], w_in[:, o_dt:o_dt + 2 * SSD_HEADS],
                              jnp.zeros((d, n_pad - n_in), w_in.dtype)], axis=1).astype(BF16)
    proj2 = proj_modulated(xa.reshape(b * l, d), mix_g, mod16, w_perm, l, n_lat, 0, 1, _pick_tn(n_pad))
    dtr = proj2[:, n_in - 2 * SSD_HEADS:n_in]
    dt = jax.nn.softplus(dtr.reshape(b, l, 2, SSD_HEADS) + dt_bias)
    a_neg = -jnp.exp(a_log)
    da = jnp.concatenate([dt, dt * a_neg], axis=-1)
    da = jnp.moveaxis(da, 2, 1)
    dat = jnp.swapaxes(da, 2, 3)
    xs_bf, bm, cm, sc_bf = even_prep(proj2, conv_w, conv_b, sc_conv_w, l, n_lat, d)
    xs_bf = xs_bf.reshape(b, l, d)
    bm = bm.reshape(b, l, gn)
    cm = cm.reshape(b, l, gn)
    bt = jnp.swapaxes(bm, 1, 2)
    yf = ssd_scan(xs_bf, bt, bm, cm, da, dat, n_lat, rev=False)
    post = (yf, proj2, jnp.repeat(d_skip, HEAD_DIM).reshape(1, d), norm_g.reshape(1, d))
    y_bf = ssd_scan(xs_bf, bt, bm, cm, da, dat, n_lat, rev=True, post=post)
    return matmul_residual([y_bf.reshape(b * l, d), sc_bf], w_out.astype(BF16), xa.reshape(b * l, d), mod16,
                           l, n_lat, 2).reshape(b, l, d)


def _odd_mixer(xa, mix_g, mod16, n_lat, w_in, w_out, na_qk_g, na_rel_bias, df_qk_g, df_lambda, df_subln_g,
               lambda_init, rope_tabs):
    b, l, d = xa.shape
    d_na = NA_HEADS * HEAD_DIM
    dq_w = DF_HEADS * 2 * HEAD_DIM
    n_ctx = l - n_lat
    reps = ODD_TN // HEAD_DIM
    ones = jnp.ones((ODD_TN,), F32)
    gains6 = jnp.stack([jnp.tile(na_qk_g[0], reps), jnp.tile(na_qk_g[1], reps), ones,
                        jnp.tile(df_qk_g[0], reps), jnp.tile(df_qk_g[1], reps), ones])[:, None, :]
    qkv = proj_odd(xa.reshape(b * l, d), mix_g, mod16, w_in.astype(BF16), gains6, *rope_tabs, l, n_lat, 0, 1)
    qkv = qkv.reshape(b, l, -1)
    lam_p = df_lambda.astype(F32)
    na_slabs = d_na // LANES
    df_slabs = dq_w // LANES
    na_lat = na_attention(qkv, na_bias_table(na_rel_bias), n_lat)
    na_ctx = flash_slabs(qkv, lam_p, "pair", n_ctx, n_ctx, n_ctx, n_ctx, n_lat // n_ctx, n_lat // n_ctx,
                         0, na_slabs, 2 * na_slabs, na_slabs)
    tk = next(t for t in (1408, 1280, 1024, 768, 512, 256) if l % t == 0)
    dq0 = 3 * na_slabs
    df_lat = flash_slabs(qkv, lam_p, "diff", 256, tk, n_lat, l, 0, 0, dq0, dq0 + df_slabs, dq0 + 2 * df_slabs,
                         df_slabs, lambda_init)
    df_ctx = flash_slabs(qkv, lam_p, "diff", n_ctx, n_ctx, n_ctx, n_ctx, n_lat // n_ctx, n_lat // n_ctx,
                         dq0, dq0 + df_slabs, dq0 + 2 * df_slabs, df_slabs, lambda_init)
    na_o = jnp.concatenate([na_lat, na_ctx], axis=1)
    df_o = jnp.concatenate([df_lat, df_ctx], axis=1).reshape(b, l, DF_HEADS, 2 * HEAD_DIM)
    df_o = (_rms(df_o, df_subln_g) * (1.0 - lambda_init)).reshape(b, l, dq_w)
    parts = [na_o.reshape(b * l, d_na).astype(BF16), df_o.reshape(b * l, dq_w).astype(BF16)]
    return matmul_residual(parts, w_out.astype(BF16), xa.reshape(b * l, d), mod16, l, n_lat, 2).reshape(b, l, d)


def kernel(x, c, ctx, c_ctx, ada_w, ada_b, mix_norm_g, ffn_norm_g, router_w, router_b, moe_w_gu, moe_b_gu,
           moe_w_down, moe_b_down, ev_w_in, ev_w_out, ssd_conv_w, ssd_conv_b, ssd_a_log, ssd_dt_bias, ssd_d,
           ssd_norm_g, sc_conv_w, od_w_in, od_w_out, na_qk_g, na_rel_bias, df_qk_g, df_lambda, df_subln_g):
    b, s, d = x.shape
    n_ctx = ctx.shape[1]
    l = s + n_ctx
    depth = ada_w.shape[0]
    de = moe_w_down.shape[2]
    ne = moe_w_gu.shape[1]
    xa = jnp.concatenate([x, ctx], axis=1)
    is_ctx = (jnp.arange(l) >= s)[None, :, None]
    cos_l, sin_l = _axial_rope(s)
    cos_t = jnp.concatenate([cos_l, jnp.ones((n_ctx, HEAD_DIM), F32)], axis=0)
    sin_t = jnp.concatenate([sin_l, jnp.zeros((n_ctx, HEAD_DIM), F32)], axis=0)
    even_q = ((jnp.arange(HEAD_DIM) // (HEAD_DIM // 4)) % 2 == 0)[None, :]
    rope_tabs = tuple(jnp.tile(t, (1, LANES // HEAD_DIM))
                      for t in (cos_t, jnp.where(even_q, -sin_t, 0.0), jnp.where(even_q, 0.0, sin_t)))
    cond = jnp.concatenate([jax.nn.silu(c), jax.nn.silu(c_ctx)[None, :]], axis=0)
    cond_pad = jnp.zeros((16, d), F32).at[:b + 1].set(cond).astype(BF16)
    tiles_per_batch = l // ROUTER_TM
    n_lat_tiles = s // ROUTER_TM
    wgu_all = moe_w_gu.reshape(depth * ne, d, 2 * de)
    bgu_all = moe_b_gu.reshape(depth * ne, 2 * de // GU_BLK, LANES, 2).swapaxes(-1, -2).reshape(depth * ne, 1, 2 * de)
    wd_all = moe_w_down.reshape(depth * ne, de, d)
    bd_all = moe_b_down.reshape(depth * ne, 1, d)

    for i in range(depth):
        j = i // 2
        mod = matmul(cond_pad, ada_w[i].astype(BF16), 16, 6 * d // 4)[:b + 1] + ada_b[i]
        mod6 = mod.reshape(b + 1, 6, d)
        pad2 = jnp.zeros((b, CTX_ROW0 - 6, d), F32)
        mod16 = jnp.concatenate([mod6[:b], pad2, jnp.broadcast_to(mod6[b], (b, 6, d)), pad2], axis=1)

        def tokmod(which):
            return jnp.where(is_ctx, mod16[:, CTX_ROW0 + which][:, None, :], mod16[:, which][:, None, :])

        if i % 2 == 0:
            xa = _even_mixer(xa, mix_norm_g[i], mod16, s, ev_w_in[j], ev_w_out[j], ssd_conv_w[j], ssd_conv_b[j],
                             ssd_a_log[j], ssd_dt_bias[j], ssd_d[j], ssd_norm_g[j], sc_conv_w[j])
        else:
            lambda_init = 0.8 - 0.6 * math.exp(-0.3 * i)
            xa = _odd_mixer(xa, mix_norm_g[i], mod16, s, od_w_in[j], od_w_out[j], na_qk_g[j], na_rel_bias[j],
                            df_qk_g[j], df_lambda[j], df_subln_g[j], lambda_init, rope_tabs)
        f = moe_ffn(xa.reshape(b * l, d), ffn_norm_g[i], mod16.reshape(b * MOD_ROWS, 1, d), 3, 4,
                    n_lat_tiles, tiles_per_batch, router_w[i], router_b[i],
                    wgu_all, bgu_all, wd_all, bd_all, i * ne)
        xa = xa + tokmod(5) * f.reshape(b, l, d)
    return xa[:, :s]
```

```python
import functools
import math

import jax
import jax.numpy as jnp
from jax import lax
from jax.experimental import pallas as pl
from jax.experimental.pallas import tpu as pltpu

F32 = jnp.float32
BF16 = jnp.bfloat16

GRID_W = 64
HEAD_DIM = 64
EPS = 1e-6
SSD_HEADS = 16
SSD_GROUPS = 4
SSD_STATE = 128
NA_ROWS = 8
NA_COLS = 16
NA_HEADS = 8
DF_HEADS = 4
N_EXPERTS = 32
TOP_K = 4
SWIGLU_LIMIT = 7.0
SWIGLU_ALPHA = 1.702
ROPE_BASE = 10000.0

LANES = 128
SSD_Q = 128
MOE_TM = 512
ROUTER_TM = 256
N_STREAMS = 2
VMEM_LIMIT = 56 * 1024 * 1024
NEG = -1e30


def _cparams(sem):
    return pltpu.CompilerParams(dimension_semantics=sem, vmem_limit_bytes=VMEM_LIMIT)


def _mm_kernel(a_ref, w_ref, o_ref):
    o_ref[...] = jnp.dot(a_ref[...], w_ref[...], preferred_element_type=F32).astype(o_ref.dtype)


def matmul(a, w, tm, tn, out_dtype=F32):
    m, k = a.shape
    n = w.shape[1]
    return pl.pallas_call(
        _mm_kernel,
        grid=(n // tn, m // tm),
        in_specs=[pl.BlockSpec((tm, k), lambda j, i: (i, 0)),
                  pl.BlockSpec((k, tn), lambda j, i: (0, j))],
        out_specs=pl.BlockSpec((tm, tn), lambda j, i: (i, j)),
        out_shape=jax.ShapeDtypeStruct((m, n), out_dtype),
        compiler_params=_cparams(("parallel", "parallel")),
        name="matmul",
    )(a, w)


MOD_ROWS = 16
CTX_ROW0 = 8
ODD_TN = 512


def _row_is_ctx(i, tm, tiles_per_batch, n_lat):
    t = i % tiles_per_batch
    row = t * tm + lax.broadcasted_iota(jnp.int32, (tm, 1), 0)
    return row >= n_lat


def _mod_row(mod, is_ctx, which):
    return jnp.where(is_ctx, mod[CTX_ROW0 + which:CTX_ROW0 + which + 1], mod[which:which + 1])


def _modulated_rows(x_ref, g_ref, mod_ref, is_ctx, sh_row, sc_row):
    x = x_ref[...]
    y = x * lax.rsqrt(jnp.mean(x * x, axis=-1, keepdims=True) + EPS) * g_ref[...]
    mod = mod_ref[0]
    return y * (1.0 + _mod_row(mod, is_ctx, sc_row)) + _mod_row(mod, is_ctx, sh_row)


def _proj_kernel(x_ref, g_ref, mod_ref, w_ref, o_ref, h_ref, *, tiles_per_batch, n_lat, sh_row, sc_row):
    i = pl.program_id(0)

    @pl.when(pl.program_id(1) == 0)
    def _():
        is_ctx = _row_is_ctx(i, x_ref.shape[0], tiles_per_batch, n_lat)
        h_ref[...] = _modulated_rows(x_ref, g_ref, mod_ref, is_ctx, sh_row, sc_row).astype(BF16)

    o_ref[...] = jnp.dot(h_ref[...], w_ref[...], preferred_element_type=F32)


def _group_rms(a, gsum, gain):
    outs = []
    half = gsum.shape[0]
    for hf in range(a.shape[1] // half):
        ah = a[:, hf * half:(hf + 1) * half]
        ms = jnp.dot((ah * ah).astype(BF16), gsum, preferred_element_type=F32) * (1.0 / HEAD_DIM)
        outs.append(ah * lax.rsqrt(ms + EPS))
    return jnp.concatenate(outs, axis=1) * gain


def _proj_odd_kernel(x_ref, g_ref, mod_ref, w_ref, gain_ref, gsum_ref, cos_ref, sina_ref, sinb_ref, o_ref, h_ref, *,
                     tiles_per_batch, n_lat, sh_row, sc_row):
    i = pl.program_id(0)
    j = pl.program_id(1)

    @pl.when(j == 0)
    def _():
        is_ctx = _row_is_ctx(i, x_ref.shape[0], tiles_per_batch, n_lat)
        h_ref[...] = _modulated_rows(x_ref, g_ref, mod_ref, is_ctx, sh_row, sc_row).astype(BF16)

    acc = jnp.dot(h_ref[...], w_ref[...], preferred_element_type=F32)
    is_norm = (j == 0) | (j == 1) | (j == 3) | (j == 4)
    is_rope = (j == 3) | (j == 4)

    @pl.when(jnp.logical_not(is_norm))
    def _():
        o_ref[...] = acc.astype(BF16)

    @pl.when(is_norm & jnp.logical_not(is_rope))
    def _():
        o_ref[...] = _group_rms(acc, gsum_ref[...], gain_ref[0]).astype(BF16)

    @pl.when(is_rope)
    def _():
        xn = _group_rms(acc, gsum_ref[...], gain_ref[0])
        reps = xn.shape[1] // LANES
        cos = jnp.concatenate([cos_ref[...]] * reps, axis=1)
        sina = jnp.concatenate([sina_ref[...]] * reps, axis=1)
        sinb = jnp.concatenate([sinb_ref[...]] * reps, axis=1)
        quarter = HEAD_DIM // 4
        up = pltpu.roll(xn, xn.shape[1] - quarter, axis=1)
        dn = pltpu.roll(xn, quarter, axis=1)
        o_ref[...] = (xn * cos + up * sina + dn * sinb).astype(BF16)


def _pick_rows(l):
    for tm in (768, 512, 256):
        if l % tm == 0:
            return tm
    raise ValueError(l)


def proj_modulated(x2, gain, mod16, w_bf, l, n_lat, sh_row, sc_row, tn):
    m, d = x2.shape
    n = w_bf.shape[1]
    tm = _pick_rows(l)
    tpb = l // tm
    return pl.pallas_call(
        functools.partial(_proj_kernel, tiles_per_batch=tpb, n_lat=n_lat, sh_row=sh_row, sc_row=sc_row),
        grid=(m // tm, n // tn),
        in_specs=[pl.BlockSpec((tm, d), lambda i, j: (i, 0)),
                  pl.BlockSpec((1, d), lambda i, j: (0, 0)),
                  pl.BlockSpec((1, MOD_ROWS, d), lambda i, j: (i // tpb, 0, 0)),
                  pl.BlockSpec((d, tn), lambda i, j: (0, j))],
        out_specs=pl.BlockSpec((tm, tn), lambda i, j: (i, j)),
        out_shape=jax.ShapeDtypeStruct((m, n), F32),
        scratch_shapes=[pltpu.VMEM((tm, d), BF16)],
        compiler_params=_cparams(("parallel", "arbitrary")),
        name="proj_even",
    )(x2, gain.reshape(1, d), mod16, w_bf)


def proj_odd(x2, gain, mod16, w_bf, gains6, cos2, sina2, sinb2, l, n_lat, sh_row, sc_row):
    m, d = x2.shape
    n = w_bf.shape[1]
    tn = ODD_TN
    tm = _pick_rows(l)
    tpb = l // tm
    half = 256
    gi = jnp.arange(half) // HEAD_DIM
    gsum = (gi[:, None] == gi[None, :]).astype(BF16)
    tab = pl.BlockSpec((tm, LANES), lambda i, j: (i % tpb, 0))
    return pl.pallas_call(
        functools.partial(_proj_odd_kernel, tiles_per_batch=tpb, n_lat=n_lat, sh_row=sh_row, sc_row=sc_row),
        grid=(m // tm, n // tn),
        in_specs=[pl.BlockSpec((tm, d), lambda i, j: (i, 0)),
                  pl.BlockSpec((1, d), lambda i, j: (0, 0)),
                  pl.BlockSpec((1, MOD_ROWS, d), lambda i, j: (i // tpb, 0, 0)),
                  pl.BlockSpec((d, tn), lambda i, j: (0, j)),
                  pl.BlockSpec((1, 1, tn), lambda i, j: (j, 0, 0)),
                  pl.BlockSpec((half, half), lambda i, j: (0, 0)),
                  tab, tab, tab],
        out_specs=pl.BlockSpec((tm, tn), lambda i, j: (i, j)),
        out_shape=jax.ShapeDtypeStruct((m, n), BF16),
        scratch_shapes=[pltpu.VMEM((tm, d), BF16)],
        compiler_params=_cparams(("parallel", "arbitrary")),
        name="proj_odd",
    )(x2, gain.reshape(1, d), mod16, w_bf, gains6, gsum, cos2, sina2, sinb2)


def _mm_res_kernel(*refs, n_parts, tiles_per_batch, n_lat, gate_row):
    a_refs = refs[:n_parts]
    w_ref, r_ref, mod_ref, o_ref = refs[n_parts:]
    is_ctx = _row_is_ctx(pl.program_id(0), r_ref.shape[0], tiles_per_batch, n_lat)
    gate = _mod_row(mod_ref[0], is_ctx, gate_row)
    acc = None
    k0 = 0
    for a_ref in a_refs:
        k1 = k0 + a_ref.shape[1]
        t = jnp.dot(a_ref[...], w_ref[k0:k1, :], preferred_element_type=F32)
        acc = t if acc is None else acc + t
        k0 = k1
    o_ref[...] = r_ref[...] + gate * acc


def matmul_residual(parts, w_bf, res, mod16, l, n_lat, gate_row):
    m = res.shape[0]
    k, d = w_bf.shape
    tm = _pick_rows(l)
    tpb = l // tm
    return pl.pallas_call(
        functools.partial(_mm_res_kernel, n_parts=len(parts), tiles_per_batch=tpb, n_lat=n_lat, gate_row=gate_row),
        grid=(m // tm,),
        in_specs=[pl.BlockSpec((tm, a.shape[1]), lambda i: (i, 0)) for a in parts]
                 + [pl.BlockSpec((k, d), lambda i: (0, 0)),
                    pl.BlockSpec((tm, d), lambda i: (i, 0)),
                    pl.BlockSpec((1, MOD_ROWS, d), lambda i: (i // tpb, 0, 0))],
        out_specs=pl.BlockSpec((tm, d), lambda i: (i, 0)),
        out_shape=jax.ShapeDtypeStruct((m, d), F32),
        compiler_params=_cparams(("parallel",)),
        name="out_proj",
    )(*parts, w_bf, res, mod16)


EV_TM = 256
HALO = 8


def _even_prep_kernel(xa_ref, xa_p, xa_n, xb_ref, xb_p, xb_n, gb_ref, gc_ref, gc_p, gc_n, gh_ref, gh_p, gh_n,
                      cw_ref, cb_ref, sw_ref, xs_ref, bm_ref, cm_ref, sc_ref, scr, *, tiles_per_batch, n_lat, l):
    tm = xa_ref.shape[0]
    r0 = (pl.program_id(0) % tiles_per_batch) * tm
    has_prev = (r0 != 0) & (r0 != n_lat)
    has_next = (r0 + tm != n_lat) & (r0 + tm != l)

    def conv(main, prev, nxt, w):
        width = w.shape[0]
        scr[0:HALO] = jnp.where(has_prev, prev, 0.0)
        scr[HALO:HALO + tm] = main
        scr[HALO + tm:2 * HALO + tm] = jnp.where(has_next, nxt, 0.0)
        out = None
        for kk in range(width):
            o = HALO + kk - width // 2
            t = scr[o:o + tm] * w[kk:kk + 1]
            out = t if out is None else out + t
        return out

    d = xa_ref.shape[1]
    cw = cw_ref[...]
    cb = cb_ref[...]
    xs = conv(xa_ref[...], xa_p[...], xa_n[...], cw[:, :d]) + cb[:, :d]
    xs_ref[...] = (xs * jax.nn.sigmoid(xs)).astype(BF16)
    bc = conv(xb_ref[...], xb_p[...], xb_n[...], cw[:, d:]) + cb[:, d:]
    bc = (bc * jax.nn.sigmoid(bc)).astype(BF16)
    gn = bm_ref.shape[1]
    bm_ref[...] = bc[:, :gn]
    cm_ref[...] = bc[:, gn:]
    sc = gb_ref[...] * conv(gc_ref[...] * gh_ref[...], gc_p[...] * gh_p[...], gc_n[...] * gh_n[...], sw_ref[...])
    sc_ref[...] = sc.astype(BF16)


def even_prep(proj2, conv_w, conv_b, sc_conv_w, l, n_lat, d):
    m = proj2.shape[0]
    tm = EV_TM
    tpb = l // tm
    nhb = m // HALO
    gn = SSD_GROUPS * SSD_STATE

    def main(cblk):
        return pl.BlockSpec((tm, d), lambda i: (i, cblk))

    def prev(cblk):
        return pl.BlockSpec((HALO, d), lambda i: (jnp.maximum(i * (tm // HALO) - 1, 0), cblk))

    def nxt(cblk):
        return pl.BlockSpec((HALO, d), lambda i: (jnp.minimum((i + 1) * (tm // HALO), nhb - 1), cblk))

    full = lambda a: pl.BlockSpec(a.shape, lambda i: (0, 0))
    cb2 = conv_b.reshape(1, -1)
    return pl.pallas_call(
        functools.partial(_even_prep_kernel, tiles_per_batch=tpb, n_lat=n_lat, l=l),
        grid=(m // tm,),
        in_specs=[main(1), prev(1), nxt(1), main(2), prev(2), nxt(2), main(3),
                  main(4), prev(4), nxt(4), main(5), prev(5), nxt(5),
                  full(conv_w), full(cb2), full(sc_conv_w)],
        out_specs=[pl.BlockSpec((tm, d), lambda i: (i, 0)), pl.BlockSpec((tm, gn), lambda i: (i, 0)),
                   pl.BlockSpec((tm, gn), lambda i: (i, 0)), pl.BlockSpec((tm, d), lambda i: (i, 0))],
        out_shape=[jax.ShapeDtypeStruct((m, d), BF16), jax.ShapeDtypeStruct((m, gn), BF16),
                   jax.ShapeDtypeStruct((m, gn), BF16), jax.ShapeDtypeStruct((m, d), BF16)],
        scratch_shapes=[pltpu.VMEM((tm + 2 * HALO, d), F32)],
        compiler_params=_cparams(("parallel",)),
        name="even_prep",
    )(proj2, proj2, proj2, proj2, proj2, proj2, proj2, proj2, proj2, proj2, proj2, proj2, proj2,
      conv_w, cb2, sc_conv_w)


GU_BLK = 2 * LANES


def _gate_up_perm():
    src = jnp.arange(GU_BLK)
    dst = jnp.where(src % 2 == 0, src // 2, LANES + src // 2)
    return (dst[:, None] == jnp.arange(GU_BLK)[None, :]).astype(BF16)


def _deinterleave_into(w_ref, perm, o_ref):
    for blk in range(o_ref.shape[1] // GU_BLK):
        cs = slice(blk * GU_BLK, (blk + 1) * GU_BLK)
        o_ref[:, cs] = jnp.dot(w_ref[0, :, cs].astype(BF16), perm, preferred_element_type=F32).astype(BF16)


def _split3_dot(tri, a, dims):
    a1 = a.astype(BF16)
    r1 = a - a1.astype(F32)
    a2 = r1.astype(BF16)
    a3 = (r1 - a2.astype(F32)).astype(BF16)
    out = None
    for piece in (a1, a2, a3):
        if dims == "tri_a":
            t = jnp.dot(tri, piece, preferred_element_type=F32)
        else:
            t = jnp.dot(piece, tri, preferred_element_type=F32)
        out = t if out is None else out + t
    return out


def _ssd_kernel(*refs, rev, post):
    if post:
        x_ref, bt_ref, b_ref, c_ref, da_ref, dat_ref, yo_ref, z_ref, dsk_ref, ng_ref, y_ref, st_ref, yacc = refs
    else:
        x_ref, bt_ref, b_ref, c_ref, da_ref, dat_ref, y_ref, st_ref = refs
    q = SSD_Q
    step = pl.program_id(1)

    @pl.when(step == 0)
    def _():
        st_ref[...] = jnp.zeros_like(st_ref)

    da = da_ref[0, 0]
    dat = dat_ref[0, 0]
    a_c = da[:, SSD_HEADS:2 * SSD_HEADS]
    dt_r = dat[0:SSD_HEADS]
    a_r = dat[SSD_HEADS:2 * SSD_HEADS]
    ri = lax.broadcasted_iota(jnp.int32, (q, q), 0)
    ci = lax.broadcasted_iota(jnp.int32, (q, q), 1)
    tri = (ci <= ri).astype(BF16)
    tri_t = (ri <= ci).astype(BF16)
    cum_c = _split3_dot(tri, a_c, "tri_a")
    cum_r = _split3_dot(tri_t, a_r, "a_tri")
    tot_r = cum_r[:, q - 1:q]
    if rev:
        pos_c = cum_c - a_c
        pos_r = cum_r - a_r
        mask = ci >= ri
    else:
        pos_c = cum_c
        pos_r = cum_r
        mask = ri >= ci
    lane = lax.broadcasted_iota(jnp.int32, (q, LANES), 1)
    lane_n = lax.broadcasted_iota(jnp.int32, (SSD_STATE, LANES), 1)
    heads_per_group = SSD_HEADS // SSD_GROUPS

    for g in range(SSD_GROUPS):
        gs = slice(g * SSD_STATE, (g + 1) * SSD_STATE)
        bg = b_ref[0, :, gs]
        cg = c_ref[0, :, gs]
        btg = bt_ref[0, gs, :].astype(F32)
        cb = lax.dot_general(cg, bg, (((1,), (1,)), ((), ())), preferred_element_type=F32)
        cg32 = cg.astype(F32)
        for pp in range(heads_per_group // 2):
            p = g * (heads_per_group // 2) + pp
            xp = x_ref[0, :, p * LANES:(p + 1) * LANES]
            st = st_ref[p]
            rhs = jnp.concatenate([xp, st.astype(BF16)], axis=0)
            ys, ds, decs = [], [], []
            for hh in range(2):
                h = 2 * p + hh
                colb = jnp.broadcast_to(pos_c[:, h:h + 1], (q, LANES))
                row = pos_r[h:h + 1, :]
                dt_row = dt_r[h:h + 1, :]
                tot = tot_r[h:h + 1, :]
                if rev:
                    seg = row - colb
                    coff = jnp.exp(tot - colb)
                    w_row = dt_row * jnp.exp(row)
                else:
                    seg = colb - row
                    coff = jnp.exp(colb)
                    w_row = dt_row * jnp.exp(tot - row)
                decay = jnp.exp(jnp.where(mask, seg, NEG))
                m_h = (cb * decay * dt_row).astype(BF16)
                c_h = (cg32 * coff).astype(BF16)
                lhs = jnp.concatenate([m_h, c_h], axis=1)
                ys.append(jnp.dot(lhs, rhs, preferred_element_type=F32))
                btw = (btg * w_row).astype(BF16)
                ds.append(jnp.dot(btw, xp, preferred_element_type=F32))
                decs.append(jnp.exp(tot))
            y_pair = jnp.where(lane < HEAD_DIM, ys[0], ys[1])
            if post:
                yacc[:, p * LANES:(p + 1) * LANES] = y_pair
            else:
                y_ref[0, :, p * LANES:(p + 1) * LANES] = y_pair
            st_ref[p] = jnp.where(lane_n < HEAD_DIM, decs[0] * st + ds[0], decs[1] * st + ds[1])

    if post:
        z = z_ref[...]
        y = (yacc[...] + yo_ref[0] + dsk_ref[...] * x_ref[0].astype(F32)) * (z * jax.nn.sigmoid(z))
        y = y * lax.rsqrt(jnp.mean(y * y, axis=-1, keepdims=True) + EPS) * ng_ref[...]
        y_ref[0] = y.astype(y_ref.dtype)


def ssd_scan(xs, bt, bm, cm, da, dat, n_lat, rev, post=None):
    b, l, d = xs.shape
    nch = l // SSD_Q
    nlat = n_lat // SSD_Q
    nctx = nch - nlat
    d_idx = 1 if rev else 0
    if rev:
        def chunk(j):
            return nch - 1 - j
    else:
        def chunk(j):
            return jnp.where(j < nctx, nlat + j, j - nctx)
    gn = SSD_GROUPS * SSD_STATE
    tok = pl.BlockSpec((1, SSD_Q, d), lambda i, j: (i, chunk(j), 0))
    in_specs = [tok,
                pl.BlockSpec((1, gn, SSD_Q), lambda i, j: (i, 0, chunk(j))),
                pl.BlockSpec((1, SSD_Q, gn), lambda i, j: (i, chunk(j), 0)),
                pl.BlockSpec((1, SSD_Q, gn), lambda i, j: (i, chunk(j), 0)),
                pl.BlockSpec((1, 1, SSD_Q, 2 * SSD_HEADS), lambda i, j: (i, d_idx, chunk(j), 0)),
                pl.BlockSpec((1, 1, 2 * SSD_HEADS, SSD_Q), lambda i, j: (i, d_idx, 0, chunk(j)))]
    scratch = [pltpu.VMEM((SSD_HEADS // 2, SSD_STATE, LANES), F32)]
    args = (xs, bt, bm, cm, da, dat)
    if post is not None:
        row = pl.BlockSpec((1, d), lambda i, j: (0, 0))
        in_specs += [tok, pl.BlockSpec((SSD_Q, d), lambda i, j: (i * nch + chunk(j), 0)), row, row]
        scratch.append(pltpu.VMEM((SSD_Q, d), F32))
        args += tuple(post)
    return pl.pallas_call(
        functools.partial(_ssd_kernel, rev=rev, post=post is not None),
        grid=(b, nch),
        in_specs=in_specs,
        out_specs=tok,
        out_shape=jax.ShapeDtypeStruct((b, l, d), F32 if post is None else BF16),
        scratch_shapes=scratch,
        compiler_params=_cparams(("parallel", "arbitrary")),
        name="ssd_bwd" if rev else "ssd_fwd",
    )(*args)


def _stack_halves(qv):
    lane = lax.broadcasted_iota(jnp.int32, qv.shape, 1)
    zero = jnp.zeros_like(qv)
    q1 = jnp.where(lane < HEAD_DIM, qv, zero)
    q2 = jnp.where(lane >= HEAD_DIM, qv, zero)
    return jnp.concatenate([q1, q2], axis=0) * jnp.asarray(HEAD_DIM ** -0.5, qv.dtype)


def _flash_kernel(lam_ref, q_ref, k_ref, v_ref, o_ref, *, mode, tk, lambda_init):
    tq = q_ref.shape[1]
    nk = k_ref.shape[1] // tk
    nt = tk // LANES
    qs = _stack_halves(q_ref[0])

    def scores(u):
        return lax.dot_general(qs, k_ref[0, u * tk:(u + 1) * tk, :], (((1,), (1,)), ((), ())),
                               preferred_element_type=F32)

    m_old = jnp.full((2 * tq, LANES), -jnp.inf, F32)
    l_run = jnp.zeros((2 * tq, LANES), F32)
    acc = jnp.zeros((2 * tq, LANES), F32)
    s_next = scores(0)
    for u in range(nk):
        s = s_next
        if u + 1 < nk:
            s_next = scores(u + 1)
        tiles = [s[:, t * LANES:(t + 1) * LANES] for t in range(nt)]
        smax = tiles[0]
        for t in tiles[1:]:
            smax = jnp.maximum(smax, t)
        m_new = jnp.maximum(m_old, jnp.max(smax, axis=-1, keepdims=True))
        alpha = jnp.exp(m_old - m_new)
        ps = [jnp.exp(t - m_new) for t in tiles]
        psum = ps[0]
        for t in ps[1:]:
            psum = psum + t
        l_run = alpha * l_run + jnp.sum(psum, axis=-1, keepdims=True)
        p = jnp.concatenate([t.astype(BF16) for t in ps], axis=1)
        acc = alpha * acc + jnp.dot(p, v_ref[0, u * tk:(u + 1) * tk, :], preferred_element_type=F32)
        m_old = m_new
    o = acc / l_run
    o1, o2 = o[:tq], o[tq:]
    if mode == "diff":
        lp = lam_ref[...]
        s01 = jnp.sum(lp[0:1] * lp[1:2], axis=-1, keepdims=True)
        s23 = jnp.sum(lp[2:3] * lp[3:4], axis=-1, keepdims=True)
        lam = jnp.exp(s01) - jnp.exp(s23) + lambda_init
        out = o1 - lam * o2
    else:
        lane = lax.broadcasted_iota(jnp.int32, o1.shape, 1)
        out = jnp.where(lane < HEAD_DIM, o1, o2)
    o_ref[0] = out.astype(o_ref.dtype)


def flash_slabs(qkv, lam_p, mode, tq, tk, lq, lk, q_blk0, k_blk, q_slab0, k_slab0, v_slab0, n_slabs,
                lambda_init=0.0):
    b = qkv.shape[0]
    return pl.pallas_call(
        functools.partial(_flash_kernel, mode=mode, tk=tk, lambda_init=lambda_init),
        grid=(b, n_slabs, lq // tq),
        in_specs=[pl.BlockSpec(lam_p.shape, lambda i, h, j: (0, 0)),
                  pl.BlockSpec((1, tq, LANES), lambda i, h, j: (i, q_blk0 + j, q_slab0 + h)),
                  pl.BlockSpec((1, lk, LANES), lambda i, h, j: (i, k_blk, k_slab0 + h)),
                  pl.BlockSpec((1, lk, LANES), lambda i, h, j: (i, k_blk, v_slab0 + h))],
        out_specs=pl.BlockSpec((1, tq, LANES), lambda i, h, j: (i, j, h)),
        out_shape=jax.ShapeDtypeStruct((b, lq, n_slabs * LANES), F32),
        compiler_params=_cparams(("parallel", "parallel", "arbitrary")),
        name="flash_" + mode,
    )(lam_p, qkv, qkv, qkv)


NA_RB = 8
NA_BLK = NA_RB * GRID_W
NA_WIN = NA_ROWS * GRID_W


def _na_kernel(q_ref, kp_ref, kc_ref, kn_ref, vp_ref, vc_ref, vn_ref, kx_ref, vx_ref, bias_ref, o_ref,
               kbuf, vbuf, *, rows):
    rb = pl.program_id(1)
    kbuf[0:NA_BLK] = kp_ref[0]
    kbuf[NA_BLK:2 * NA_BLK] = kc_ref[0]
    kbuf[2 * NA_BLK:3 * NA_BLK] = kn_ref[0]
    vbuf[0:NA_BLK] = vp_ref[0]
    vbuf[NA_BLK:2 * NA_BLK] = vc_ref[0]
    vbuf[2 * NA_BLK:3 * NA_BLK] = vn_ref[0]
    npairs = q_ref.shape[2] // LANES
    lane = lax.broadcasted_iota(jnp.int32, (GRID_W, LANES), 1)

    def row_body(rl, carry):
        r = rb * NA_RB + rl
        r_start = jnp.clip(r - NA_ROWS // 2, 0, rows - NA_ROWS)
        off = r_start - (rb * NA_RB - NA_RB)
        di0 = r_start - r + NA_ROWS - 1
        tok0 = pl.multiple_of(off * GRID_W, GRID_W)
        q0 = pl.multiple_of(rl * GRID_W, GRID_W)
        for p in range(npairs):
            ls = slice(p * LANES, (p + 1) * LANES)
            qs = _stack_halves(q_ref[0, pl.ds(q0, GRID_W), ls])
            kw = kbuf[pl.ds(tok0, NA_WIN), ls]
            vw = vbuf[pl.ds(tok0, NA_WIN), ls]
            s_nb = lax.dot_general(qs, kw, (((1,), (1,)), ((), ())), preferred_element_type=F32)
            bias = jnp.concatenate([bias_ref[p, di0 + 2 * j] for j in range(NA_ROWS // 2)], axis=1)
            s_nb = s_nb + bias
            s_cx = lax.dot_general(qs, kx_ref[0, :, ls], (((1,), (1,)), ((), ())), preferred_element_type=F32)
            m = jnp.maximum(jnp.max(s_nb, axis=-1, keepdims=True), jnp.max(s_cx, axis=-1, keepdims=True))
            p_nb = jnp.exp(s_nb - m)
            p_cx = jnp.exp(s_cx - m)
            l = jnp.sum(p_nb, axis=-1, keepdims=True) + jnp.sum(p_cx, axis=-1, keepdims=True)
            o = (jnp.dot(p_nb.astype(BF16), vw, preferred_element_type=F32)
                 + jnp.dot(p_cx.astype(BF16), vx_ref[0, :, ls], preferred_element_type=F32)) / l
            o_ref[0, pl.ds(q0, GRID_W), ls] = jnp.where(lane < HEAD_DIM, o[:GRID_W], o[GRID_W:])
        return carry

    lax.fori_loop(0, NA_RB, row_body, 0)


def na_bias_table(rel_bias):
    cols = jnp.arange(GRID_W)
    c_start = jnp.clip(cols - NA_COLS // 2, 0, GRID_W - NA_COLS)
    kc = jnp.arange(GRID_W)
    valid = (kc[None, :] >= c_start[:, None]) & (kc[None, :] < c_start[:, None] + NA_COLS)
    idx = jnp.clip(kc[None, :] - cols[:, None] + NA_COLS - 1, 0, 2 * NA_COLS - 2)
    t = jnp.where(valid[None, None], rel_bias[:, :, idx], NEG)
    t2 = jnp.concatenate([t[:, :-1], t[:, 1:]], axis=-1)
    nh, nd = t2.shape[0], t2.shape[1]
    t2 = t2.reshape(nh // 2, 2, nd, GRID_W, LANES).transpose(0, 2, 1, 3, 4)
    return t2.reshape(nh // 2, nd, 2 * GRID_W, LANES).astype(F32)


def na_attention(qkv, bias_tab, n_lat):
    b, l, _ = qkv.shape
    w = NA_HEADS * HEAD_DIM
    rows = n_lat // GRID_W
    nrb = rows // NA_RB
    n_ctx = l - n_lat
    ctx_blk = n_lat // n_ctx

    def blk(step, stream):
        if step < 0:
            return pl.BlockSpec((1, NA_BLK, w), lambda i, j: (i, jnp.maximum(j - 1, 0), stream))
        if step > 0:
            return pl.BlockSpec((1, NA_BLK, w), lambda i, j: (i, jnp.minimum(j + 1, nrb - 1), stream))
        return pl.BlockSpec((1, NA_BLK, w), lambda i, j: (i, j, stream))

    def ctx(stream):
        return pl.BlockSpec((1, n_ctx, w), lambda i, j: (i, ctx_blk, stream))

    return pl.pallas_call(
        functools.partial(_na_kernel, rows=rows),
        grid=(b, nrb),
        in_specs=[blk(0, 0), blk(-1, 1), blk(0, 1), blk(1, 1), blk(-1, 2), blk(0, 2), blk(1, 2), ctx(1), ctx(2),
                  pl.BlockSpec(bias_tab.shape, lambda i, j: (0, 0, 0, 0))],
        out_specs=pl.BlockSpec((1, NA_BLK, w), lambda i, j: (i, j, 0)),
        out_shape=jax.ShapeDtypeStruct((b, n_lat, w), F32),
        scratch_shapes=[pltpu.VMEM((3 * NA_BLK, w), BF16), pltpu.VMEM((3 * NA_BLK, w), BF16)],
        compiler_params=_cparams(("parallel", "parallel")),
        name="na_attention",
    )(qkv, qkv, qkv, qkv, qkv, qkv, qkv, qkv, qkv, bias_tab)


def _router_kernel(x_ref, g_ref, sh_ref, sc_ref, w_ref, b_ref, h_ref, ri_ref, rg_ref, cnt_ref, carry_ref):
    step = pl.program_id(0)

    @pl.when(step == 0)
    def _():
        carry_ref[...] = jnp.zeros_like(carry_ref)

    tm = x_ref.shape[0]
    x = x_ref[...]
    y = x * lax.rsqrt(jnp.mean(x * x, axis=-1, keepdims=True) + EPS) * g_ref[...]
    h = y * (1.0 + sc_ref[0]) + sh_ref[0]
    h_ref[...] = h.astype(BF16)
    w = w_ref[...]
    h1 = h.astype(BF16)
    h2 = (h - h1.astype(F32)).astype(BF16)
    w1 = w.astype(BF16)
    w2 = (w - w1.astype(F32)).astype(BF16)
    logits = (jnp.dot(h1, w1, preferred_element_type=F32) + jnp.dot(h2, w1, preferred_element_type=F32)
              + jnp.dot(h1, w2, preferred_element_type=F32)) + b_ref[...]
    lane = lax.broadcasted_iota(jnp.int32, (tm, LANES), 1)
    work = logits
    tops, idxs, hots = [], [], []
    for _ in range(TOP_K):
        mx = jnp.max(work, axis=-1, keepdims=True)
        ix = jnp.min(jnp.where(work == mx, lane, LANES), axis=-1, keepdims=True)
        hot = lane == ix
        work = jnp.where(hot, -jnp.inf, work)
        tops.append(mx)
        idxs.append(ix)
        hots.append(hot)
    es = [jnp.exp(t - tops[0]) for t in tops]
    den = es[0] + es[1] + es[2] + es[3]
    multi = jnp.zeros((tm, LANES), F32)
    for hot in hots:
        multi = multi + hot.astype(F32)
    ri = lax.broadcasted_iota(jnp.int32, (tm, tm), 0)
    ci = lax.broadcasted_iota(jnp.int32, (tm, tm), 1)
    tri = (ci < ri).astype(BF16)
    cum = jnp.dot(tri, multi.astype(BF16), preferred_element_type=F32) + carry_ref[...]
    carry_new = carry_ref[...] + jnp.sum(multi, axis=0, keepdims=True)
    carry_ref[...] = carry_new
    cnt_ref[...] = carry_new
    out_i = jnp.zeros((tm, LANES), jnp.int32)
    out_g = jnp.zeros((tm, LANES), F32)
    for kk in range(TOP_K):
        rank = jnp.sum(jnp.where(hots[kk], cum, 0.0), axis=-1, keepdims=True).astype(jnp.int32)
        out_i = jnp.where(lane == kk, idxs[kk], out_i)
        out_i = jnp.where(lane == TOP_K + kk, rank, out_i)
        out_g = jnp.where(lane == kk, es[kk] / den, out_g)
    ri_ref[...] = out_i
    rg_ref[...] = out_g


def moe_router(x2, gain, modarr, sh_row, sc_row, n_lat_tiles, tiles_per_batch, router_w, router_b):
    n, d = x2.shape
    tm = ROUTER_TM
    w_pad = jnp.zeros((d, LANES), F32).at[:, :N_EXPERTS].set(router_w)
    b_pad = jnp.full((1, LANES), NEG, F32).at[0, :N_EXPERTS].set(router_b)

    def mod_idx(which):
        def f(i):
            bidx = i // tiles_per_batch
            is_ctx = (i % tiles_per_batch) >= n_lat_tiles
            return (bidx * MOD_ROWS + is_ctx.astype(jnp.int32) * CTX_ROW0 + which, 0, 0)
        return f

    return pl.pallas_call(
        _router_kernel,
        grid=(n // tm,),
        in_specs=[pl.BlockSpec((tm, d), lambda i: (i, 0)),
                  pl.BlockSpec((1, d), lambda i: (0, 0)),
                  pl.BlockSpec((1, 1, d), mod_idx(sh_row)),
                  pl.BlockSpec((1, 1, d), mod_idx(sc_row)),
                  pl.BlockSpec((d, LANES), lambda i: (0, 0)),
                  pl.BlockSpec((1, LANES), lambda i: (0, 0))],
        out_specs=[pl.BlockSpec((tm, d), lambda i: (i, 0)),
                   pl.BlockSpec((tm, LANES), lambda i: (i, 0)),
                   pl.BlockSpec((tm, LANES), lambda i: (i, 0)),
                   pl.BlockSpec((1, LANES), lambda i: (0, 0))],
        out_shape=[jax.ShapeDtypeStruct((n, d), BF16),
                   jax.ShapeDtypeStruct((n, LANES), jnp.int32),
                   jax.ShapeDtypeStruct((n, LANES), F32),
                   jax.ShapeDtypeStruct((1, LANES), F32)],
        scratch_shapes=[pltpu.VMEM((1, LANES), F32)],
        compiler_params=_cparams(("arbitrary",)),
        name="moe_router",
    )(x2, gain.reshape(1, d), modarr, modarr, w_pad, b_pad)


def _expert_kernel(be_ref, nu_ref, x_ref, wgu_ref, bgu_ref, wd_ref, bd_ref, perm_ref, o_ref, wgu_bf, wd_bf):
    j = pl.program_id(0)
    used = j < nu_ref[0]
    changed = (j == 0) | (be_ref[j] != be_ref[jnp.maximum(j - 1, 0)])

    @pl.when(used & changed)
    def _():
        wd_bf[...] = wd_ref[0].astype(BF16)
        _deinterleave_into(wgu_ref, perm_ref[...], wgu_bf)

    @pl.when(used)
    def _():
        gu = jnp.dot(x_ref[...], wgu_bf[...], preferred_element_type=F32) + bgu_ref[0]
        acts = []
        for blk in range(gu.shape[1] // GU_BLK):
            gate = jnp.minimum(gu[:, blk * GU_BLK:blk * GU_BLK + LANES], SWIGLU_LIMIT)
            up = jnp.clip(gu[:, blk * GU_BLK + LANES:(blk + 1) * GU_BLK], -SWIGLU_LIMIT, SWIGLU_LIMIT)
            acts.append((gate * jax.nn.sigmoid(SWIGLU_ALPHA * gate) * (up + 1.0)).astype(BF16))
        act = jnp.concatenate(acts, axis=1)
        o_ref[...] = (jnp.dot(act, wd_bf[...], preferred_element_type=F32) + bd_ref[0]).astype(o_ref.dtype)

    @pl.when(jnp.logical_not(used))
    def _():
        o_ref[...] = jnp.zeros_like(o_ref)


def moe_experts(xs, block_expert, n_used, wgu, bgu, wd, bd, e_off):
    cap, d = xs.shape
    de = wd.shape[1]
    nblk = cap // MOE_TM
    wmap = lambda j, be, nu: (e_off + be[j], 0, 0)
    grid_spec = pltpu.PrefetchScalarGridSpec(
        num_scalar_prefetch=2,
        grid=(nblk,),
        in_specs=[pl.BlockSpec((MOE_TM, d), lambda j, be, nu: (j, 0)),
                  pl.BlockSpec((1, d, 2 * de), wmap),
                  pl.BlockSpec((1, 1, 2 * de), wmap),
                  pl.BlockSpec((1, de, d), wmap),
                  pl.BlockSpec((1, 1, d), wmap),
                  pl.BlockSpec((GU_BLK, GU_BLK), lambda j, be, nu: (0, 0))],
        out_specs=pl.BlockSpec((MOE_TM, d), lambda j, be, nu: (j, 0)),
        scratch_shapes=[pltpu.VMEM((d, 2 * de), BF16), pltpu.VMEM((de, d), BF16)],
    )
    return pl.pallas_call(
        _expert_kernel,
        grid_spec=grid_spec,
        out_shape=jax.ShapeDtypeStruct((cap, d), BF16),
        compiler_params=_cparams(("arbitrary",)),
        name="moe_experts",
    )(block_expert, n_used, xs, wgu, bgu, wd, bd, _gate_up_perm())


def moe_ffn(x2, gain, mod16, l, n_lat, router_w, router_b, wgu, bgu, wd, bd, e_off):
    n, d = x2.shape
    sh_row, sc_row, gate_row = 3, 4, 5
    h_bf, r_i, r_g, cnt = moe_router(x2, gain, mod16.reshape(-1, 1, d), sh_row, sc_row, n_lat // ROUTER_TM,
                                     l // ROUTER_TM, router_w, router_b)
    top_idx = r_i[:, :TOP_K]
    rank = r_i[:, TOP_K:2 * TOP_K]
    gates = r_g[:, :TOP_K]
    counts = cnt[0, :N_EXPERTS].astype(jnp.int32)
    padded = (counts + MOE_TM - 1) // MOE_TM * MOE_TM
    pad_end = jnp.cumsum(padded)
    pad_start = pad_end - padded
    dest = (pad_start[top_idx] + rank).T
    nblk = -(-(n * TOP_K + N_EXPERTS * (MOE_TM - 1)) // MOE_TM)
    cap = nblk * MOE_TM
    tok = jnp.broadcast_to(jnp.arange(n, dtype=jnp.int32)[None, :], (TOP_K, n))
    slot_tok = jnp.zeros((cap,), jnp.int32).at[dest.reshape(-1)].set(tok.reshape(-1), unique_indices=True)
    blk_start = jnp.arange(nblk, dtype=jnp.int32)[:, None] * MOE_TM
    block_expert = jnp.minimum(jnp.sum((pad_end[None, :] <= blk_start).astype(jnp.int32), axis=1), N_EXPERTS - 1)
    n_used = (pad_end[-1:] // MOE_TM).astype(jnp.int32)
    xs = h_bf[slot_tok]
    y = moe_experts(xs, block_expert, n_used, wgu, bgu, wd, bd, e_off)
    yk = y[dest]
    return moe_combine(yk, r_g, x2, mod16, l, n_lat, gate_row)


def _combine_kernel(y_ref, g_ref, r_ref, mod_ref, o_ref, *, tiles_per_batch, n_lat, gate_row):
    is_ctx = _row_is_ctx(pl.program_id(0), r_ref.shape[0], tiles_per_batch, n_lat)
    gate = _mod_row(mod_ref[0], is_ctx, gate_row)
    g = g_ref[...]
    acc = y_ref[0].astype(F32) * g[:, 0:1]
    for kk in range(1, TOP_K):
        acc = acc + y_ref[kk].astype(F32) * g[:, kk:kk + 1]
    o_ref[...] = r_ref[...] + gate * acc


def moe_combine(yk, r_g, res, mod16, l, n_lat, gate_row):
    _, n, d = yk.shape
    tm = _pick_rows(l)
    tpb = l // tm
    return pl.pallas_call(
        functools.partial(_combine_kernel, tiles_per_batch=tpb, n_lat=n_lat, gate_row=gate_row),
        grid=(n // tm,),
        in_specs=[pl.BlockSpec((TOP_K, tm, d), lambda i: (0, i, 0)),
                  pl.BlockSpec((tm, LANES), lambda i: (i, 0)),
                  pl.BlockSpec((tm, d), lambda i: (i, 0)),
                  pl.BlockSpec((1, MOD_ROWS, d), lambda i: (i // tpb, 0, 0))],
        out_specs=pl.BlockSpec((tm, d), lambda i: (i, 0)),
        out_shape=jax.ShapeDtypeStruct((n, d), F32),
        compiler_params=_cparams(("parallel",)),
        name="moe_combine",
    )(yk, r_g, res, mod16)


def _rms(x, g):
    return x * lax.rsqrt(jnp.mean(x * x, axis=-1, keepdims=True) + EPS) * g


def _axial_rope(n_tok):
    pos = jnp.arange(n_tok)
    rows = (pos // GRID_W).astype(F32)
    cols = (pos % GRID_W).astype(F32)
    quarter = HEAD_DIM // 4
    inv_freq = ROPE_BASE ** (-jnp.arange(quarter, dtype=F32) / quarter)
    ar = rows[:, None] * inv_freq
    ac = cols[:, None] * inv_freq
    ang = jnp.concatenate([ar, ar, ac, ac], axis=-1)
    return jnp.cos(ang), jnp.sin(ang)


def _pick_tn(n):
    for tn in (1536, 1280, 1024, 768, 512, 256, 128):
        if n % tn == 0:
            return tn
    raise ValueError(n)


def _even_mixer(xa, mix_g, mod16, n_lat, w_in, w_out, conv_w, conv_b, a_log, dt_bias, d_skip, norm_g, sc_conv_w):
    b, l, d = xa.shape
    gn = SSD_GROUPS * SSD_STATE
    conv_dim = d + 2 * gn
    n_in = w_in.shape[1]
    n_pad = -(-n_in // 256) * 256
    o_dt = d + conv_dim
    w_perm = jnp.concatenate([w_in[:, :o_dt], w_in[:, o_dt + 2 * SSD_HEADS:], w_in[:, o_dt:o_dt + 2 * SSD_HEADS],
                              jnp.zeros((d, n_pad - n_in), w_in.dtype)], axis=1).astype(BF16)
    proj2 = proj_modulated(xa.reshape(b * l, d), mix_g, mod16, w_perm, l, n_lat, 0, 1, _pick_tn(n_pad))
    dtr = proj2[:, n_in - 2 * SSD_HEADS:n_in]
    dt = jax.nn.softplus(dtr.reshape(b, l, 2, SSD_HEADS) + dt_bias)
    a_neg = -jnp.exp(a_log)
    da = jnp.concatenate([dt, dt * a_neg], axis=-1)
    da = jnp.moveaxis(da, 2, 1)
    dat = jnp.swapaxes(da, 2, 3)
    xs_bf, bm, cm, sc_bf = even_prep(proj2, conv_w, conv_b, sc_conv_w, l, n_lat, d)
    xs_bf = xs_bf.reshape(b, l, d)
    bm = bm.reshape(b, l, gn)
    cm = cm.reshape(b, l, gn)
    bt = jnp.swapaxes(bm, 1, 2)
    yf = ssd_scan(xs_bf, bt, bm, cm, da, dat, n_lat, rev=False)
    post = (yf, proj2, jnp.repeat(d_skip, HEAD_DIM).reshape(1, d), norm_g.reshape(1, d))
    y_bf = ssd_scan(xs_bf, bt, bm, cm, da, dat, n_lat, rev=True, post=post)
    return matmul_residual([y_bf.reshape(b * l, d), sc_bf], w_out.astype(BF16), xa.reshape(b * l, d), mod16,
                           l, n_lat, 2).reshape(b, l, d)


def _odd_mixer(xa, mix_g, mod16, n_lat, w_in, w_out, na_qk_g, na_rel_bias, df_qk_g, df_lambda, df_subln_g,
               lambda_init, rope_tabs):
    b, l, d = xa.shape
    d_na = NA_HEADS * HEAD_DIM
    dq_w = DF_HEADS * 2 * HEAD_DIM
    n_ctx = l - n_lat
    reps = ODD_TN // HEAD_DIM
    ones = jnp.ones((ODD_TN,), F32)
    gains6 = jnp.stack([jnp.tile(na_qk_g[0], reps), jnp.tile(na_qk_g[1], reps), ones,
                        jnp.tile(df_qk_g[0], reps), jnp.tile(df_qk_g[1], reps), ones])[:, None, :]
    qkv = proj_odd(xa.reshape(b * l, d), mix_g, mod16, w_in.astype(BF16), gains6, *rope_tabs, l, n_lat, 0, 1)
    qkv = qkv.reshape(b, l, -1)
    lam_p = df_lambda.astype(F32)
    na_slabs = d_na // LANES
    df_slabs = dq_w // LANES
    na_lat = na_attention(qkv, na_bias_table(na_rel_bias), n_lat)
    na_ctx = flash_slabs(qkv, lam_p, "pair", n_ctx, n_ctx, n_ctx, n_ctx, n_lat // n_ctx, n_lat // n_ctx,
                         0, na_slabs, 2 * na_slabs, na_slabs)
    tk = next(t for t in (1408, 1280, 1024, 768, 512, 256) if l % t == 0)
    dq0 = 3 * na_slabs
    df_lat = flash_slabs(qkv, lam_p, "diff", 256, tk, n_lat, l, 0, 0, dq0, dq0 + df_slabs, dq0 + 2 * df_slabs,
                         df_slabs, lambda_init)
    df_ctx = flash_slabs(qkv, lam_p, "diff", n_ctx, n_ctx, n_ctx, n_ctx, n_lat // n_ctx, n_lat // n_ctx,
                         dq0, dq0 + df_slabs, dq0 + 2 * df_slabs, df_slabs, lambda_init)
    na_o = jnp.concatenate([na_lat, na_ctx], axis=1)
    df_o = jnp.concatenate([df_lat, df_ctx], axis=1).reshape(b, l, DF_HEADS, 2 * HEAD_DIM)
    df_o = (_rms(df_o, df_subln_g) * (1.0 - lambda_init)).reshape(b, l, dq_w)
    parts = [na_o.reshape(b * l, d_na).astype(BF16), df_o.reshape(b * l, dq_w).astype(BF16)]
    return matmul_residual(parts, w_out.astype(BF16), xa.reshape(b * l, d), mod16, l, n_lat, 2).reshape(b, l, d)


def kernel(x, c, ctx, c_ctx, ada_w, ada_b, mix_norm_g, ffn_norm_g, router_w, router_b, moe_w_gu, moe_b_gu,
           moe_w_down, moe_b_down, ev_w_in, ev_w_out, ssd_conv_w, ssd_conv_b, ssd_a_log, ssd_dt_bias, ssd_d,
           ssd_norm_g, sc_conv_w, od_w_in, od_w_out, na_qk_g, na_rel_bias, df_qk_g, df_lambda, df_subln_g):
    b, s, d = x.shape
    n_ctx = ctx.shape[1]
    l = s + n_ctx
    depth = ada_w.shape[0]
    de = moe_w_down.shape[2]
    ne = moe_w_gu.shape[1]
    n_streams = N_STREAMS if b % N_STREAMS == 0 else 1
    bs = b // n_streams
    xas = [jnp.concatenate([x[k * bs:(k + 1) * bs], ctx[k * bs:(k + 1) * bs]], axis=1)
           for k in range(n_streams)]
    cos_l, sin_l = _axial_rope(s)
    cos_t = jnp.concatenate([cos_l, jnp.ones((n_ctx, HEAD_DIM), F32)], axis=0)
    sin_t = jnp.concatenate([sin_l, jnp.zeros((n_ctx, HEAD_DIM), F32)], axis=0)
    even_q = ((jnp.arange(HEAD_DIM) // (HEAD_DIM // 4)) % 2 == 0)[None, :]
    rope_tabs = tuple(jnp.tile(t, (1, LANES // HEAD_DIM))
                      for t in (cos_t, jnp.where(even_q, -sin_t, 0.0), jnp.where(even_q, 0.0, sin_t)))
    cond = jnp.concatenate([jax.nn.silu(c), jax.nn.silu(c_ctx)[None, :]], axis=0)
    cond_pad = jnp.zeros((16, d), F32).at[:b + 1].set(cond).astype(BF16)
    wgu_all = moe_w_gu.reshape(depth * ne, d, 2 * de)
    bgu_all = moe_b_gu.reshape(depth * ne, 2 * de // GU_BLK, LANES, 2).swapaxes(-1, -2).reshape(depth * ne, 1, 2 * de)
    wd_all = moe_w_down.reshape(depth * ne, de, d)
    bd_all = moe_b_down.reshape(depth * ne, 1, d)

    for i in range(depth):
        j = i // 2
        mod = matmul(cond_pad, ada_w[i].astype(BF16), 16, 6 * d // 4)[:b + 1] + ada_b[i]
        mod6 = mod.reshape(b + 1, 6, d)
        pad2 = jnp.zeros((b, CTX_ROW0 - 6, d), F32)
        mod16_all = jnp.concatenate([mod6[:b], pad2, jnp.broadcast_to(mod6[b], (b, 6, d)), pad2], axis=1)
        for k in range(n_streams):
            xa = xas[k]
            mod16 = mod16_all[k * bs:(k + 1) * bs]
            if i % 2 == 0:
                xa = _even_mixer(xa, mix_norm_g[i], mod16, s, ev_w_in[j], ev_w_out[j], ssd_conv_w[j],
                                 ssd_conv_b[j], ssd_a_log[j], ssd_dt_bias[j], ssd_d[j], ssd_norm_g[j], sc_conv_w[j])
            else:
                lambda_init = 0.8 - 0.6 * math.exp(-0.3 * i)
                xa = _odd_mixer(xa, mix_norm_g[i], mod16, s, od_w_in[j], od_w_out[j], na_qk_g[j], na_rel_bias[j],
                                df_qk_g[j], df_lambda[j], df_subln_g[j], lambda_init, rope_tabs)
            xas[k] = moe_ffn(xa.reshape(bs * l, d), ffn_norm_g[i], mod16, l, s, router_w[i], router_b[i],
                             wgu_all, bgu_all, wd_all, bd_all, i * ne).reshape(bs, l, d)
    return jnp.concatenate([xa[:, :s] for xa in xas], axis=0)
```

```python
import functools
import math

import jax
import jax.numpy as jnp
from jax import lax
from jax.experimental import pallas as pl
from jax.experimental.pallas import tpu as pltpu

F32 = jnp.float32
BF16 = jnp.bfloat16

GRID_W = 64
HEAD_DIM = 64
EPS = 1e-6
SSD_HEADS = 16
SSD_GROUPS = 4
SSD_STATE = 128
NA_ROWS = 8
NA_COLS = 16
NA_HEADS = 8
DF_HEADS = 4
N_EXPERTS = 32
TOP_K = 4
SWIGLU_LIMIT = 7.0
SWIGLU_ALPHA = 1.702
ROPE_BASE = 10000.0

LANES = 128
SSD_Q = 128
MOE_TM = 512
ROUTER_TM = 256
N_STREAMS = 2
VMEM_BYTES = 64 * 1024 * 1024
VMEM_LIMIT = 56 * 1024 * 1024
NEG = -1e30


def _cparams(sem):
    return pltpu.CompilerParams(dimension_semantics=sem, vmem_limit_bytes=VMEM_LIMIT)


def _mm_kernel(a_ref, w_ref, o_ref):
    o_ref[...] = jnp.dot(a_ref[...], w_ref[...], preferred_element_type=F32).astype(o_ref.dtype)


def matmul(a, w, tm, tn, out_dtype=F32):
    m, k = a.shape
    n = w.shape[1]
    return pl.pallas_call(
        _mm_kernel,
        grid=(n // tn, m // tm),
        in_specs=[pl.BlockSpec((tm, k), lambda j, i: (i, 0)),
                  pl.BlockSpec((k, tn), lambda j, i: (0, j))],
        out_specs=pl.BlockSpec((tm, tn), lambda j, i: (i, j)),
        out_shape=jax.ShapeDtypeStruct((m, n), out_dtype),
        compiler_params=_cparams(("parallel", "parallel")),
        name="matmul",
    )(a, w)


MOD_ROWS = 16
CTX_ROW0 = 8
ODD_TN = 512


def _row_is_ctx(i, tm, tiles_per_batch, n_lat):
    t = i % tiles_per_batch
    row = t * tm + lax.broadcasted_iota(jnp.int32, (tm, 1), 0)
    return row >= n_lat


def _mod_row(mod, is_ctx, which):
    return jnp.where(is_ctx, mod[CTX_ROW0 + which:CTX_ROW0 + which + 1], mod[which:which + 1])


def _modulated_rows(x_ref, g_ref, mod_ref, is_ctx, sh_row, sc_row):
    x = x_ref[...]
    y = x * lax.rsqrt(jnp.mean(x * x, axis=-1, keepdims=True) + EPS) * g_ref[...]
    mod = mod_ref[0]
    return y * (1.0 + _mod_row(mod, is_ctx, sc_row)) + _mod_row(mod, is_ctx, sh_row)


def _proj_kernel(x_ref, g_ref, mod_ref, w_ref, o_ref, h_ref, *, tiles_per_batch, n_lat, sh_row, sc_row):
    i = pl.program_id(0)

    @pl.when(pl.program_id(1) == 0)
    def _():
        is_ctx = _row_is_ctx(i, x_ref.shape[0], tiles_per_batch, n_lat)
        h_ref[...] = _modulated_rows(x_ref, g_ref, mod_ref, is_ctx, sh_row, sc_row).astype(BF16)

    o_ref[...] = jnp.dot(h_ref[...], w_ref[...], preferred_element_type=F32)


def _group_rms(a, gsum, gain):
    outs = []
    half = gsum.shape[0]
    for hf in range(a.shape[1] // half):
        ah = a[:, hf * half:(hf + 1) * half]
        ms = jnp.dot((ah * ah).astype(BF16), gsum, preferred_element_type=F32) * (1.0 / HEAD_DIM)
        outs.append(ah * lax.rsqrt(ms + EPS))
    return jnp.concatenate(outs, axis=1) * gain


def _proj_odd_kernel(x_ref, g_ref, mod_ref, w_ref, gain_ref, gsum_ref, cos_ref, sina_ref, sinb_ref, o_ref, h_ref, *,
                     tiles_per_batch, n_lat, sh_row, sc_row):
    i = pl.program_id(0)
    j = pl.program_id(1)

    @pl.when(j == 0)
    def _():
        is_ctx = _row_is_ctx(i, x_ref.shape[0], tiles_per_batch, n_lat)
        h_ref[...] = _modulated_rows(x_ref, g_ref, mod_ref, is_ctx, sh_row, sc_row).astype(BF16)

    acc = jnp.dot(h_ref[...], w_ref[...], preferred_element_type=F32)
    is_norm = (j == 0) | (j == 1) | (j == 3) | (j == 4)
    is_rope = (j == 3) | (j == 4)

    @pl.when(jnp.logical_not(is_norm))
    def _():
        o_ref[...] = acc.astype(BF16)

    @pl.when(is_norm & jnp.logical_not(is_rope))
    def _():
        o_ref[...] = _group_rms(acc, gsum_ref[...], gain_ref[0]).astype(BF16)

    @pl.when(is_rope)
    def _():
        xn = _group_rms(acc, gsum_ref[...], gain_ref[0])
        reps = xn.shape[1] // LANES
        cos = jnp.concatenate([cos_ref[...]] * reps, axis=1)
        sina = jnp.concatenate([sina_ref[...]] * reps, axis=1)
        sinb = jnp.concatenate([sinb_ref[...]] * reps, axis=1)
        quarter = HEAD_DIM // 4
        up = pltpu.roll(xn, xn.shape[1] - quarter, axis=1)
        dn = pltpu.roll(xn, quarter, axis=1)
        o_ref[...] = (xn * cos + up * sina + dn * sinb).astype(BF16)


def _pick_rows(l):
    for tm in (768, 512, 256):
        if l % tm == 0:
            return tm
    raise ValueError(l)


def proj_modulated(x2, gain, mod16, w_bf, l, n_lat, sh_row, sc_row, tn):
    m, d = x2.shape
    n = w_bf.shape[1]
    tm = _pick_rows(l)
    tpb = l // tm
    return pl.pallas_call(
        functools.partial(_proj_kernel, tiles_per_batch=tpb, n_lat=n_lat, sh_row=sh_row, sc_row=sc_row),
        grid=(m // tm, n // tn),
        in_specs=[pl.BlockSpec((tm, d), lambda i, j: (i, 0)),
                  pl.BlockSpec((1, d), lambda i, j: (0, 0)),
                  pl.BlockSpec((1, MOD_ROWS, d), lambda i, j: (i // tpb, 0, 0)),
                  pl.BlockSpec((d, tn), lambda i, j: (0, j))],
        out_specs=pl.BlockSpec((tm, tn), lambda i, j: (i, j)),
        out_shape=jax.ShapeDtypeStruct((m, n), F32),
        scratch_shapes=[pltpu.VMEM((tm, d), BF16)],
        compiler_params=_cparams(("parallel", "arbitrary")),
        name="proj_even",
    )(x2, gain.reshape(1, d), mod16, w_bf)


def proj_odd(x2, gain, mod16, w_bf, gains6, cos2, sina2, sinb2, l, n_lat, sh_row, sc_row):
    m, d = x2.shape
    n = w_bf.shape[1]
    tn = ODD_TN
    tm = _pick_rows(l)
    tpb = l // tm
    half = 256
    gi = jnp.arange(half) // HEAD_DIM
    gsum = (gi[:, None] == gi[None, :]).astype(BF16)
    tab = pl.BlockSpec((tm, LANES), lambda i, j: (i % tpb, 0))
    return pl.pallas_call(
        functools.partial(_proj_odd_kernel, tiles_per_batch=tpb, n_lat=n_lat, sh_row=sh_row, sc_row=sc_row),
        grid=(m // tm, n // tn),
        in_specs=[pl.BlockSpec((tm, d), lambda i, j: (i, 0)),
                  pl.BlockSpec((1, d), lambda i, j: (0, 0)),
                  pl.BlockSpec((1, MOD_ROWS, d), lambda i, j: (i // tpb, 0, 0)),
                  pl.BlockSpec((d, tn), lambda i, j: (0, j)),
                  pl.BlockSpec((1, 1, tn), lambda i, j: (j, 0, 0)),
                  pl.BlockSpec((half, half), lambda i, j: (0, 0)),
                  tab, tab, tab],
        out_specs=pl.BlockSpec((tm, tn), lambda i, j: (i, j)),
        out_shape=jax.ShapeDtypeStruct((m, n), BF16),
        scratch_shapes=[pltpu.VMEM((tm, d), BF16)],
        compiler_params=_cparams(("parallel", "arbitrary")),
        name="proj_odd",
    )(x2, gain.reshape(1, d), mod16, w_bf, gains6, gsum, cos2, sina2, sinb2)


def _mm_res_kernel(*refs, n_parts, tiles_per_batch, n_lat, gate_row):
    a_refs = refs[:n_parts]
    w_ref, r_ref, mod_ref, o_ref = refs[n_parts:]
    is_ctx = _row_is_ctx(pl.program_id(0), r_ref.shape[0], tiles_per_batch, n_lat)
    gate = _mod_row(mod_ref[0], is_ctx, gate_row)
    acc = None
    k0 = 0
    for a_ref in a_refs:
        k1 = k0 + a_ref.shape[1]
        t = jnp.dot(a_ref[...], w_ref[k0:k1, :], preferred_element_type=F32)
        acc = t if acc is None else acc + t
        k0 = k1
    o_ref[...] = r_ref[...] + gate * acc


def matmul_residual(parts, w_bf, res, mod16, l, n_lat, gate_row):
    m = res.shape[0]
    k, d = w_bf.shape
    tm = _pick_rows(l)
    tpb = l // tm
    return pl.pallas_call(
        functools.partial(_mm_res_kernel, n_parts=len(parts), tiles_per_batch=tpb, n_lat=n_lat, gate_row=gate_row),
        grid=(m // tm,),
        in_specs=[pl.BlockSpec((tm, a.shape[1]), lambda i: (i, 0)) for a in parts]
                 + [pl.BlockSpec((k, d), lambda i: (0, 0)),
                    pl.BlockSpec((tm, d), lambda i: (i, 0)),
                    pl.BlockSpec((1, MOD_ROWS, d), lambda i: (i // tpb, 0, 0))],
        out_specs=pl.BlockSpec((tm, d), lambda i: (i, 0)),
        out_shape=jax.ShapeDtypeStruct((m, d), F32),
        compiler_params=_cparams(("parallel",)),
        name="out_proj",
    )(*parts, w_bf, res, mod16)


EV_TM = 256
HALO = 8


def _even_prep_kernel(xa_ref, xa_p, xa_n, xb_ref, xb_p, xb_n, gb_ref, gc_ref, gc_p, gc_n, gh_ref, gh_p, gh_n,
                      cw_ref, cb_ref, sw_ref, xs_ref, bm_ref, cm_ref, sc_ref, scr, *, tiles_per_batch, n_lat, l):
    tm = xa_ref.shape[0]
    r0 = (pl.program_id(0) % tiles_per_batch) * tm
    has_prev = (r0 != 0) & (r0 != n_lat)
    has_next = (r0 + tm != n_lat) & (r0 + tm != l)

    def conv(main, prev, nxt, w):
        width = w.shape[0]
        scr[0:HALO] = jnp.where(has_prev, prev, 0.0)
        scr[HALO:HALO + tm] = main
        scr[HALO + tm:2 * HALO + tm] = jnp.where(has_next, nxt, 0.0)
        out = None
        for kk in range(width):
            o = HALO + kk - width // 2
            t = scr[o:o + tm] * w[kk:kk + 1]
            out = t if out is None else out + t
        return out

    d = xa_ref.shape[1]
    cw = cw_ref[...]
    cb = cb_ref[...]
    xs = conv(xa_ref[...], xa_p[...], xa_n[...], cw[:, :d]) + cb[:, :d]
    xs_ref[...] = (xs * jax.nn.sigmoid(xs)).astype(BF16)
    bc = conv(xb_ref[...], xb_p[...], xb_n[...], cw[:, d:]) + cb[:, d:]
    bc = (bc * jax.nn.sigmoid(bc)).astype(BF16)
    gn = bm_ref.shape[1]
    bm_ref[...] = bc[:, :gn]
    cm_ref[...] = bc[:, gn:]
    sc = gb_ref[...] * conv(gc_ref[...] * gh_ref[...], gc_p[...] * gh_p[...], gc_n[...] * gh_n[...], sw_ref[...])
    sc_ref[...] = sc.astype(BF16)


def even_prep(proj2, conv_w, conv_b, sc_conv_w, l, n_lat, d):
    m = proj2.shape[0]
    tm = EV_TM
    tpb = l // tm
    nhb = m // HALO
    gn = SSD_GROUPS * SSD_STATE

    def main(cblk):
        return pl.BlockSpec((tm, d), lambda i: (i, cblk))

    def prev(cblk):
        return pl.BlockSpec((HALO, d), lambda i: (jnp.maximum(i * (tm // HALO) - 1, 0), cblk))

    def nxt(cblk):
        return pl.BlockSpec((HALO, d), lambda i: (jnp.minimum((i + 1) * (tm // HALO), nhb - 1), cblk))

    full = lambda a: pl.BlockSpec(a.shape, lambda i: (0, 0))
    cb2 = conv_b.reshape(1, -1)
    return pl.pallas_call(
        functools.partial(_even_prep_kernel, tiles_per_batch=tpb, n_lat=n_lat, l=l),
        grid=(m // tm,),
        in_specs=[main(1), prev(1), nxt(1), main(2), prev(2), nxt(2), main(3),
                  main(4), prev(4), nxt(4), main(5), prev(5), nxt(5),
                  full(conv_w), full(cb2), full(sc_conv_w)],
        out_specs=[pl.BlockSpec((tm, d), lambda i: (i, 0)), pl.BlockSpec((tm, gn), lambda i: (i, 0)),
                   pl.BlockSpec((tm, gn), lambda i: (i, 0)), pl.BlockSpec((tm, d), lambda i: (i, 0))],
        out_shape=[jax.ShapeDtypeStruct((m, d), BF16), jax.ShapeDtypeStruct((m, gn), BF16),
                   jax.ShapeDtypeStruct((m, gn), BF16), jax.ShapeDtypeStruct((m, d), BF16)],
        scratch_shapes=[pltpu.VMEM((tm + 2 * HALO, d), F32)],
        compiler_params=_cparams(("parallel",)),
        name="even_prep",
    )(proj2, proj2, proj2, proj2, proj2, proj2, proj2, proj2, proj2, proj2, proj2, proj2, proj2,
      conv_w, cb2, sc_conv_w)


GU_BLK = 2 * LANES


def _gate_up_perm():
    src = jnp.arange(GU_BLK)
    dst = jnp.where(src % 2 == 0, src // 2, LANES + src // 2)
    return (dst[:, None] == jnp.arange(GU_BLK)[None, :]).astype(BF16)


def _deinterleave_into(w_ref, perm, o_ref):
    for blk in range(o_ref.shape[1] // GU_BLK):
        cs = slice(blk * GU_BLK, (blk + 1) * GU_BLK)
        o_ref[:, cs] = jnp.dot(w_ref[0, :, cs].astype(BF16), perm, preferred_element_type=F32).astype(BF16)


def _split3_dot(tri, a, dims):
    a1 = a.astype(BF16)
    r1 = a - a1.astype(F32)
    a2 = r1.astype(BF16)
    a3 = (r1 - a2.astype(F32)).astype(BF16)
    out = None
    for piece in (a1, a2, a3):
        if dims == "tri_a":
            t = jnp.dot(tri, piece, preferred_element_type=F32)
        else:
            t = jnp.dot(piece, tri, preferred_element_type=F32)
        out = t if out is None else out + t
    return out


def _ssd_kernel(*refs, rev, post):
    if post:
        x_ref, bt_ref, b_ref, c_ref, da_ref, dat_ref, yo_ref, z_ref, dsk_ref, ng_ref, y_ref, st_ref, yacc = refs
    else:
        x_ref, bt_ref, b_ref, c_ref, da_ref, dat_ref, y_ref, st_ref = refs
    q = SSD_Q
    step = pl.program_id(1)

    @pl.when(step == 0)
    def _():
        st_ref[...] = jnp.zeros_like(st_ref)

    da = da_ref[0, 0]
    dat = dat_ref[0, 0]
    a_c = da[:, SSD_HEADS:2 * SSD_HEADS]
    dt_r = dat[0:SSD_HEADS]
    a_r = dat[SSD_HEADS:2 * SSD_HEADS]
    ri = lax.broadcasted_iota(jnp.int32, (q, q), 0)
    ci = lax.broadcasted_iota(jnp.int32, (q, q), 1)
    tri = (ci <= ri).astype(BF16)
    tri_t = (ri <= ci).astype(BF16)
    cum_c = _split3_dot(tri, a_c, "tri_a")
    cum_r = _split3_dot(tri_t, a_r, "a_tri")
    tot_r = cum_r[:, q - 1:q]
    if rev:
        pos_c = cum_c - a_c
        pos_r = cum_r - a_r
        mask = ci >= ri
    else:
        pos_c = cum_c
        pos_r = cum_r
        mask = ri >= ci
    lane = lax.broadcasted_iota(jnp.int32, (q, LANES), 1)
    lane_n = lax.broadcasted_iota(jnp.int32, (SSD_STATE, LANES), 1)
    heads_per_group = SSD_HEADS // SSD_GROUPS

    for g in range(SSD_GROUPS):
        gs = slice(g * SSD_STATE, (g + 1) * SSD_STATE)
        bg = b_ref[0, :, gs]
        cg = c_ref[0, :, gs]
        btg = bt_ref[0, gs, :].astype(F32)
        cb = lax.dot_general(cg, bg, (((1,), (1,)), ((), ())), preferred_element_type=F32)
        cg32 = cg.astype(F32)
        for pp in range(heads_per_group // 2):
            p = g * (heads_per_group // 2) + pp
            xp = x_ref[0, :, p * LANES:(p + 1) * LANES]
            st = st_ref[p]
            rhs = jnp.concatenate([xp, st.astype(BF16)], axis=0)
            ys, ds, decs = [], [], []
            for hh in range(2):
                h = 2 * p + hh
                colb = jnp.broadcast_to(pos_c[:, h:h + 1], (q, LANES))
                row = pos_r[h:h + 1, :]
                dt_row = dt_r[h:h + 1, :]
                tot = tot_r[h:h + 1, :]
                if rev:
                    seg = row - colb
                    coff = jnp.exp(tot - colb)
                    w_row = dt_row * jnp.exp(row)
                else:
                    seg = colb - row
                    coff = jnp.exp(colb)
                    w_row = dt_row * jnp.exp(tot - row)
                decay = jnp.exp(jnp.where(mask, seg, NEG))
                m_h = (cb * decay * dt_row).astype(BF16)
                c_h = (cg32 * coff).astype(BF16)
                lhs = jnp.concatenate([m_h, c_h], axis=1)
                ys.append(jnp.dot(lhs, rhs, preferred_element_type=F32))
                btw = (btg * w_row).astype(BF16)
                ds.append(jnp.dot(btw, xp, preferred_element_type=F32))
                decs.append(jnp.exp(tot))
            y_pair = jnp.where(lane < HEAD_DIM, ys[0], ys[1])
            if post:
                yacc[:, p * LANES:(p + 1) * LANES] = y_pair
            else:
                y_ref[0, :, p * LANES:(p + 1) * LANES] = y_pair
            st_ref[p] = jnp.where(lane_n < HEAD_DIM, decs[0] * st + ds[0], decs[1] * st + ds[1])

    if post:
        z = z_ref[...]
        y = (yacc[...] + yo_ref[0] + dsk_ref[...] * x_ref[0].astype(F32)) * (z * jax.nn.sigmoid(z))
        y = y * lax.rsqrt(jnp.mean(y * y, axis=-1, keepdims=True) + EPS) * ng_ref[...]
        y_ref[0] = y.astype(y_ref.dtype)


def ssd_scan(xs, bt, bm, cm, da, dat, n_lat, rev, post=None):
    b, l, d = xs.shape
    nch = l // SSD_Q
    nlat = n_lat // SSD_Q
    nctx = nch - nlat
    d_idx = 1 if rev else 0
    if rev:
        def chunk(j):
            return nch - 1 - j
    else:
        def chunk(j):
            return jnp.where(j < nctx, nlat + j, j - nctx)
    gn = SSD_GROUPS * SSD_STATE
    tok = pl.BlockSpec((1, SSD_Q, d), lambda i, j: (i, chunk(j), 0))
    in_specs = [tok,
                pl.BlockSpec((1, gn, SSD_Q), lambda i, j: (i, 0, chunk(j))),
                pl.BlockSpec((1, SSD_Q, gn), lambda i, j: (i, chunk(j), 0)),
                pl.BlockSpec((1, SSD_Q, gn), lambda i, j: (i, chunk(j), 0)),
                pl.BlockSpec((1, 1, SSD_Q, 2 * SSD_HEADS), lambda i, j: (i, d_idx, chunk(j), 0)),
                pl.BlockSpec((1, 1, 2 * SSD_HEADS, SSD_Q), lambda i, j: (i, d_idx, 0, chunk(j)))]
    scratch = [pltpu.VMEM((SSD_HEADS // 2, SSD_STATE, LANES), F32)]
    args = (xs, bt, bm, cm, da, dat)
    if post is not None:
        row = pl.BlockSpec((1, d), lambda i, j: (0, 0))
        in_specs += [tok, pl.BlockSpec((SSD_Q, d), lambda i, j: (i * nch + chunk(j), 0)), row, row]
        scratch.append(pltpu.VMEM((SSD_Q, d), F32))
        args += tuple(post)
    return pl.pallas_call(
        functools.partial(_ssd_kernel, rev=rev, post=post is not None),
        grid=(b, nch),
        in_specs=in_specs,
        out_specs=tok,
        out_shape=jax.ShapeDtypeStruct((b, l, d), F32 if post is None else BF16),
        scratch_shapes=scratch,
        compiler_params=_cparams(("parallel", "arbitrary")),
        name="ssd_bwd" if rev else "ssd_fwd",
    )(*args)


def _stack_halves(qv):
    lane = lax.broadcasted_iota(jnp.int32, qv.shape, 1)
    zero = jnp.zeros_like(qv)
    q1 = jnp.where(lane < HEAD_DIM, qv, zero)
    q2 = jnp.where(lane >= HEAD_DIM, qv, zero)
    return jnp.concatenate([q1, q2], axis=0) * jnp.asarray(HEAD_DIM ** -0.5, qv.dtype)


def _flash_kernel(lam_ref, q_ref, k_ref, v_ref, o_ref, *, mode, tk, lambda_init):
    tq = q_ref.shape[1]
    nk = k_ref.shape[1] // tk
    nt = tk // LANES
    qs = _stack_halves(q_ref[0])

    def scores(u):
        return lax.dot_general(qs, k_ref[0, u * tk:(u + 1) * tk, :], (((1,), (1,)), ((), ())),
                               preferred_element_type=F32)

    m_old = jnp.full((2 * tq, LANES), -jnp.inf, F32)
    l_run = jnp.zeros((2 * tq, LANES), F32)
    acc = jnp.zeros((2 * tq, LANES), F32)
    s_next = scores(0)
    for u in range(nk):
        s = s_next
        if u + 1 < nk:
            s_next = scores(u + 1)
        tiles = [s[:, t * LANES:(t + 1) * LANES] for t in range(nt)]
        smax = tiles[0]
        for t in tiles[1:]:
            smax = jnp.maximum(smax, t)
        m_new = jnp.maximum(m_old, jnp.max(smax, axis=-1, keepdims=True))
        alpha = jnp.exp(m_old - m_new)
        ps = [jnp.exp(t - m_new) for t in tiles]
        psum = ps[0]
        for t in ps[1:]:
            psum = psum + t
        l_run = alpha * l_run + jnp.sum(psum, axis=-1, keepdims=True)
        p = jnp.concatenate([t.astype(BF16) for t in ps], axis=1)
        acc = alpha * acc + jnp.dot(p, v_ref[0, u * tk:(u + 1) * tk, :], preferred_element_type=F32)
        m_old = m_new
    o = acc / l_run
    o1, o2 = o[:tq], o[tq:]
    if mode == "diff":
        lp = lam_ref[...]
        s01 = jnp.sum(lp[0:1] * lp[1:2], axis=-1, keepdims=True)
        s23 = jnp.sum(lp[2:3] * lp[3:4], axis=-1, keepdims=True)
        lam = jnp.exp(s01) - jnp.exp(s23) + lambda_init
        out = o1 - lam * o2
    else:
        lane = lax.broadcasted_iota(jnp.int32, o1.shape, 1)
        out = jnp.where(lane < HEAD_DIM, o1, o2)
    o_ref[0] = out.astype(o_ref.dtype)


def flash_slabs(qkv, lam_p, mode, tq, tk, lq, lk, q_blk0, k_blk, q_slab0, k_slab0, v_slab0, n_slabs,
                lambda_init=0.0):
    b = qkv.shape[0]
    return pl.pallas_call(
        functools.partial(_flash_kernel, mode=mode, tk=tk, lambda_init=lambda_init),
        grid=(b, n_slabs, lq // tq),
        in_specs=[pl.BlockSpec(lam_p.shape, lambda i, h, j: (0, 0)),
                  pl.BlockSpec((1, tq, LANES), lambda i, h, j: (i, q_blk0 + j, q_slab0 + h)),
                  pl.BlockSpec((1, lk, LANES), lambda i, h, j: (i, k_blk, k_slab0 + h)),
                  pl.BlockSpec((1, lk, LANES), lambda i, h, j: (i, k_blk, v_slab0 + h))],
        out_specs=pl.BlockSpec((1, tq, LANES), lambda i, h, j: (i, j, h)),
        out_shape=jax.ShapeDtypeStruct((b, lq, n_slabs * LANES), F32),
        compiler_params=_cparams(("parallel", "parallel", "arbitrary")),
        name="flash_" + mode,
    )(lam_p, qkv, qkv, qkv)


NA_RB = 8
NA_BLK = NA_RB * GRID_W
NA_WIN = NA_ROWS * GRID_W


def _na_kernel(q_ref, kp_ref, kc_ref, kn_ref, vp_ref, vc_ref, vn_ref, kx_ref, vx_ref, bias_ref, o_ref,
               kbuf, vbuf, *, rows):
    rb = pl.program_id(1)
    kbuf[0:NA_BLK] = kp_ref[0]
    kbuf[NA_BLK:2 * NA_BLK] = kc_ref[0]
    kbuf[2 * NA_BLK:3 * NA_BLK] = kn_ref[0]
    vbuf[0:NA_BLK] = vp_ref[0]
    vbuf[NA_BLK:2 * NA_BLK] = vc_ref[0]
    vbuf[2 * NA_BLK:3 * NA_BLK] = vn_ref[0]
    npairs = q_ref.shape[2] // LANES
    lane = lax.broadcasted_iota(jnp.int32, (GRID_W, LANES), 1)

    def row_body(rl, carry):
        r = rb * NA_RB + rl
        r_start = jnp.clip(r - NA_ROWS // 2, 0, rows - NA_ROWS)
        off = r_start - (rb * NA_RB - NA_RB)
        di0 = r_start - r + NA_ROWS - 1
        tok0 = pl.multiple_of(off * GRID_W, GRID_W)
        q0 = pl.multiple_of(rl * GRID_W, GRID_W)
        for p in range(npairs):
            ls = slice(p * LANES, (p + 1) * LANES)
            qs = _stack_halves(q_ref[0, pl.ds(q0, GRID_W), ls])
            kw = kbuf[pl.ds(tok0, NA_WIN), ls]
            vw = vbuf[pl.ds(tok0, NA_WIN), ls]
            s_nb = lax.dot_general(qs, kw, (((1,), (1,)), ((), ())), preferred_element_type=F32)
            bias = jnp.concatenate([bias_ref[p, di0 + 2 * j] for j in range(NA_ROWS // 2)], axis=1)
            s_nb = s_nb + bias
            s_cx = lax.dot_general(qs, kx_ref[0, :, ls], (((1,), (1,)), ((), ())), preferred_element_type=F32)
            m = jnp.maximum(jnp.max(s_nb, axis=-1, keepdims=True), jnp.max(s_cx, axis=-1, keepdims=True))
            p_nb = jnp.exp(s_nb - m)
            p_cx = jnp.exp(s_cx - m)
            l = jnp.sum(p_nb, axis=-1, keepdims=True) + jnp.sum(p_cx, axis=-1, keepdims=True)
            o = (jnp.dot(p_nb.astype(BF16), vw, preferred_element_type=F32)
                 + jnp.dot(p_cx.astype(BF16), vx_ref[0, :, ls], preferred_element_type=F32)) / l
            o_ref[0, pl.ds(q0, GRID_W), ls] = jnp.where(lane < HEAD_DIM, o[:GRID_W], o[GRID_W:])
        return carry

    lax.fori_loop(0, NA_RB, row_body, 0)


def na_bias_table(rel_bias):
    cols = jnp.arange(GRID_W)
    c_start = jnp.clip(cols - NA_COLS // 2, 0, GRID_W - NA_COLS)
    kc = jnp.arange(GRID_W)
    valid = (kc[None, :] >= c_start[:, None]) & (kc[None, :] < c_start[:, None] + NA_COLS)
    idx = jnp.clip(kc[None, :] - cols[:, None] + NA_COLS - 1, 0, 2 * NA_COLS - 2)
    t = jnp.where(valid[None, None], rel_bias[:, :, idx], NEG)
    t2 = jnp.concatenate([t[:, :-1], t[:, 1:]], axis=-1)
    nh, nd = t2.shape[0], t2.shape[1]
    t2 = t2.reshape(nh // 2, 2, nd, GRID_W, LANES).transpose(0, 2, 1, 3, 4)
    return t2.reshape(nh // 2, nd, 2 * GRID_W, LANES).astype(F32)


def na_attention(qkv, bias_tab, n_lat):
    b, l, _ = qkv.shape
    w = NA_HEADS * HEAD_DIM
    rows = n_lat // GRID_W
    nrb = rows // NA_RB
    n_ctx = l - n_lat
    ctx_blk = n_lat // n_ctx

    def blk(step, stream):
        if step < 0:
            return pl.BlockSpec((1, NA_BLK, w), lambda i, j: (i, jnp.maximum(j - 1, 0), stream))
        if step > 0:
            return pl.BlockSpec((1, NA_BLK, w), lambda i, j: (i, jnp.minimum(j + 1, nrb - 1), stream))
        return pl.BlockSpec((1, NA_BLK, w), lambda i, j: (i, j, stream))

    def ctx(stream):
        return pl.BlockSpec((1, n_ctx, w), lambda i, j: (i, ctx_blk, stream))

    return pl.pallas_call(
        functools.partial(_na_kernel, rows=rows),
        grid=(b, nrb),
        in_specs=[blk(0, 0), blk(-1, 1), blk(0, 1), blk(1, 1), blk(-1, 2), blk(0, 2), blk(1, 2), ctx(1), ctx(2),
                  pl.BlockSpec(bias_tab.shape, lambda i, j: (0, 0, 0, 0))],
        out_specs=pl.BlockSpec((1, NA_BLK, w), lambda i, j: (i, j, 0)),
        out_shape=jax.ShapeDtypeStruct((b, n_lat, w), F32),
        scratch_shapes=[pltpu.VMEM((3 * NA_BLK, w), BF16), pltpu.VMEM((3 * NA_BLK, w), BF16)],
        compiler_params=_cparams(("parallel", "parallel")),
        name="na_attention",
    )(qkv, qkv, qkv, qkv, qkv, qkv, qkv, qkv, qkv, bias_tab)


def _router_kernel(x_ref, g_ref, sh_ref, sc_ref, w_ref, b_ref, h_ref, ri_ref, rg_ref, cnt_ref, carry_ref):
    step = pl.program_id(0)

    @pl.when(step == 0)
    def _():
        carry_ref[...] = jnp.zeros_like(carry_ref)

    tm = x_ref.shape[0]
    x = x_ref[...]
    y = x * lax.rsqrt(jnp.mean(x * x, axis=-1, keepdims=True) + EPS) * g_ref[...]
    h = y * (1.0 + sc_ref[0]) + sh_ref[0]
    h_ref[...] = h.astype(BF16)
    w = w_ref[...]
    h1 = h.astype(BF16)
    h2 = (h - h1.astype(F32)).astype(BF16)
    w1 = w.astype(BF16)
    w2 = (w - w1.astype(F32)).astype(BF16)
    logits = (jnp.dot(h1, w1, preferred_element_type=F32) + jnp.dot(h2, w1, preferred_element_type=F32)
              + jnp.dot(h1, w2, preferred_element_type=F32)) + b_ref[...]
    lane = lax.broadcasted_iota(jnp.int32, (tm, LANES), 1)
    work = logits
    tops, idxs, hots = [], [], []
    for _ in range(TOP_K):
        mx = jnp.max(work, axis=-1, keepdims=True)
        ix = jnp.min(jnp.where(work == mx, lane, LANES), axis=-1, keepdims=True)
        hot = lane == ix
        work = jnp.where(hot, -jnp.inf, work)
        tops.append(mx)
        idxs.append(ix)
        hots.append(hot)
    es = [jnp.exp(t - tops[0]) for t in tops]
    den = es[0] + es[1] + es[2] + es[3]
    multi = jnp.zeros((tm, LANES), F32)
    for hot in hots:
        multi = multi + hot.astype(F32)
    ri = lax.broadcasted_iota(jnp.int32, (tm, tm), 0)
    ci = lax.broadcasted_iota(jnp.int32, (tm, tm), 1)
    tri = (ci < ri).astype(BF16)
    cum = jnp.dot(tri, multi.astype(BF16), preferred_element_type=F32) + carry_ref[...]
    carry_new = carry_ref[...] + jnp.sum(multi, axis=0, keepdims=True)
    carry_ref[...] = carry_new
    cnt_ref[...] = carry_new
    out_i = jnp.zeros((tm, LANES), jnp.int32)
    out_g = jnp.zeros((tm, LANES), F32)
    for kk in range(TOP_K):
        rank = jnp.sum(jnp.where(hots[kk], cum, 0.0), axis=-1, keepdims=True).astype(jnp.int32)
        out_i = jnp.where(lane == kk, idxs[kk], out_i)
        out_i = jnp.where(lane == TOP_K + kk, rank, out_i)
        out_g = jnp.where(lane == kk, es[kk] / den, out_g)
    ri_ref[...] = out_i
    rg_ref[...] = out_g


def moe_router(x2, gain, modarr, sh_row, sc_row, n_lat_tiles, tiles_per_batch, router_w, router_b):
    n, d = x2.shape
    tm = ROUTER_TM
    w_pad = jnp.zeros((d, LANES), F32).at[:, :N_EXPERTS].set(router_w)
    b_pad = jnp.full((1, LANES), NEG, F32).at[0, :N_EXPERTS].set(router_b)

    def mod_idx(which):
        def f(i):
            bidx = i // tiles_per_batch
            is_ctx = (i % tiles_per_batch) >= n_lat_tiles
            return (bidx * MOD_ROWS + is_ctx.astype(jnp.int32) * CTX_ROW0 + which, 0, 0)
        return f

    return pl.pallas_call(
        _router_kernel,
        grid=(n // tm,),
        in_specs=[pl.BlockSpec((tm, d), lambda i: (i, 0)),
                  pl.BlockSpec((1, d), lambda i: (0, 0)),
                  pl.BlockSpec((1, 1, d), mod_idx(sh_row)),
                  pl.BlockSpec((1, 1, d), mod_idx(sc_row)),
                  pl.BlockSpec((d, LANES), lambda i: (0, 0)),
                  pl.BlockSpec((1, LANES), lambda i: (0, 0))],
        out_specs=[pl.BlockSpec((tm, d), lambda i: (i, 0)),
                   pl.BlockSpec((tm, LANES), lambda i: (i, 0)),
                   pl.BlockSpec((tm, LANES), lambda i: (i, 0)),
                   pl.BlockSpec((1, LANES), lambda i: (0, 0))],
        out_shape=[jax.ShapeDtypeStruct((max(n, VMEM_BYTES // (2 * d)), d), BF16),
                   jax.ShapeDtypeStruct((n, LANES), jnp.int32),
                   jax.ShapeDtypeStruct((n, LANES), F32),
                   jax.ShapeDtypeStruct((1, LANES), F32)],
        scratch_shapes=[pltpu.VMEM((1, LANES), F32)],
        compiler_params=_cparams(("arbitrary",)),
        name="moe_router",
    )(x2, gain.reshape(1, d), modarr, modarr, w_pad, b_pad)


def _expert_kernel(be_ref, nu_ref, x_ref, wgu_ref, bgu_ref, wd_ref, bd_ref, perm_ref, o_ref, wgu_bf, wd_bf):
    j = pl.program_id(0)
    used = j < nu_ref[0]
    changed = (j == 0) | (be_ref[j] != be_ref[jnp.maximum(j - 1, 0)])

    @pl.when(used & changed)
    def _():
        wd_bf[...] = wd_ref[0].astype(BF16)
        _deinterleave_into(wgu_ref, perm_ref[...], wgu_bf)

    @pl.when(used)
    def _():
        gu = jnp.dot(x_ref[...], wgu_bf[...], preferred_element_type=F32) + bgu_ref[0]
        acts = []
        for blk in range(gu.shape[1] // GU_BLK):
            gate = jnp.minimum(gu[:, blk * GU_BLK:blk * GU_BLK + LANES], SWIGLU_LIMIT)
            up = jnp.clip(gu[:, blk * GU_BLK + LANES:(blk + 1) * GU_BLK], -SWIGLU_LIMIT, SWIGLU_LIMIT)
            acts.append((gate * jax.nn.sigmoid(SWIGLU_ALPHA * gate) * (up + 1.0)).astype(BF16))
        act = jnp.concatenate(acts, axis=1)
        o_ref[...] = (jnp.dot(act, wd_bf[...], preferred_element_type=F32) + bd_ref[0]).astype(o_ref.dtype)

    @pl.when(jnp.logical_not(used))
    def _():
        o_ref[...] = jnp.zeros_like(o_ref)


def moe_experts(xs, block_expert, n_used, wgu, bgu, wd, bd, e_off):
    cap, d = xs.shape
    de = wd.shape[1]
    nblk = cap // MOE_TM
    wmap = lambda j, be, nu: (e_off + be[j], 0, 0)
    grid_spec = pltpu.PrefetchScalarGridSpec(
        num_scalar_prefetch=2,
        grid=(nblk,),
        in_specs=[pl.BlockSpec((MOE_TM, d), lambda j, be, nu: (j, 0)),
                  pl.BlockSpec((1, d, 2 * de), wmap),
                  pl.BlockSpec((1, 1, 2 * de), wmap),
                  pl.BlockSpec((1, de, d), wmap),
                  pl.BlockSpec((1, 1, d), wmap),
                  pl.BlockSpec((GU_BLK, GU_BLK), lambda j, be, nu: (0, 0))],
        out_specs=pl.BlockSpec((MOE_TM, d), lambda j, be, nu: (j, 0)),
        scratch_shapes=[pltpu.VMEM((d, 2 * de), BF16), pltpu.VMEM((de, d), BF16)],
    )
    return pl.pallas_call(
        _expert_kernel,
        grid_spec=grid_spec,
        out_shape=jax.ShapeDtypeStruct((cap, d), BF16),
        compiler_params=_cparams(("arbitrary",)),
        name="moe_experts",
    )(block_expert, n_used, xs, wgu, bgu, wd, bd, _gate_up_perm())


def moe_ffn(x2, gain, mod16, l, n_lat, router_w, router_b, wgu, bgu, wd, bd, e_off):
    n, d = x2.shape
    sh_row, sc_row, gate_row = 3, 4, 5
    h_bf, r_i, r_g, cnt = moe_router(x2, gain, mod16.reshape(-1, 1, d), sh_row, sc_row, n_lat // ROUTER_TM,
                                     l // ROUTER_TM, router_w, router_b)
    top_idx = r_i[:, :TOP_K]
    rank = r_i[:, TOP_K:2 * TOP_K]
    gates = r_g[:, :TOP_K]
    counts = cnt[0, :N_EXPERTS].astype(jnp.int32)
    padded = (counts + MOE_TM - 1) // MOE_TM * MOE_TM
    pad_end = jnp.cumsum(padded)
    pad_start = pad_end - padded
    dest = (pad_start[top_idx] + rank).T
    nblk = -(-(n * TOP_K + N_EXPERTS * (MOE_TM - 1)) // MOE_TM)
    cap = nblk * MOE_TM
    tok = jnp.broadcast_to(jnp.arange(n, dtype=jnp.int32)[None, :], (TOP_K, n))
    slot_tok = jnp.zeros((cap,), jnp.int32).at[dest.reshape(-1)].set(tok.reshape(-1), unique_indices=True)
    blk_start = jnp.arange(nblk, dtype=jnp.int32)[:, None] * MOE_TM
    block_expert = jnp.minimum(jnp.sum((pad_end[None, :] <= blk_start).astype(jnp.int32), axis=1), N_EXPERTS - 1)
    n_used = (pad_end[-1:] // MOE_TM).astype(jnp.int32)
    xs = h_bf[slot_tok]
    y = moe_experts(xs, block_expert, n_used, wgu, bgu, wd, bd, e_off)
    yk = y[dest.reshape(-1)].reshape(TOP_K, n, d)
    return moe_combine(yk, r_g, x2, mod16, l, n_lat, gate_row)


def _combine_kernel(y_ref, g_ref, r_ref, mod_ref, o_ref, *, tiles_per_batch, n_lat, gate_row):
    is_ctx = _row_is_ctx(pl.program_id(0), r_ref.shape[0], tiles_per_batch, n_lat)
    gate = _mod_row(mod_ref[0], is_ctx, gate_row)
    g = g_ref[...]
    acc = y_ref[0].astype(F32) * g[:, 0:1]
    for kk in range(1, TOP_K):
        acc = acc + y_ref[kk].astype(F32) * g[:, kk:kk + 1]
    o_ref[...] = r_ref[...] + gate * acc


def moe_combine(yk, r_g, res, mod16, l, n_lat, gate_row):
    _, n, d = yk.shape
    tm = _pick_rows(l)
    tpb = l // tm
    return pl.pallas_call(
        functools.partial(_combine_kernel, tiles_per_batch=tpb, n_lat=n_lat, gate_row=gate_row),
        grid=(n // tm,),
        in_specs=[pl.BlockSpec((TOP_K, tm, d), lambda i: (0, i, 0)),
                  pl.BlockSpec((tm, LANES), lambda i: (i, 0)),
                  pl.BlockSpec((tm, d), lambda i: (i, 0)),
                  pl.BlockSpec((1, MOD_ROWS, d), lambda i: (i // tpb, 0, 0))],
        out_specs=pl.BlockSpec((tm, d), lambda i: (i, 0)),
        out_shape=jax.ShapeDtypeStruct((n, d), F32),
        compiler_params=_cparams(("parallel",)),
        name="moe_combine",
    )(yk, r_g, res, mod16)


def _rms(x, g):
    return x * lax.rsqrt(jnp.mean(x * x, axis=-1, keepdims=True) + EPS) * g


def _axial_rope(n_tok):
    pos = jnp.arange(n_tok)
    rows = (pos // GRID_W).astype(F32)
    cols = (pos % GRID_W).astype(F32)
    quarter = HEAD_DIM // 4
    inv_freq = ROPE_BASE ** (-jnp.arange(quarter, dtype=F32) / quarter)
    ar = rows[:, None] * inv_freq
    ac = cols[:, None] * inv_freq
    ang = jnp.concatenate([ar, ar, ac, ac], axis=-1)
    return jnp.cos(ang), jnp.sin(ang)


def _pick_tn(n):
    for tn in (1536, 1280, 1024, 768, 512, 256, 128):
        if n % tn == 0:
            return tn
    raise ValueError(n)


def _even_mixer(xa, mix_g, mod16, n_lat, w_in, w_out, conv_w, conv_b, a_log, dt_bias, d_skip, norm_g, sc_conv_w):
    b, l, d = xa.shape
    gn = SSD_GROUPS * SSD_STATE
    conv_dim = d + 2 * gn
    n_in = w_in.shape[1]
    n_pad = -(-n_in // 256) * 256
    o_dt = d + conv_dim
    w_perm = jnp.concatenate([w_in[:, :o_dt], w_in[:, o_dt + 2 * SSD_HEADS:], w_in[:, o_dt:o_dt + 2 * SSD_HEADS],
                              jnp.zeros((d, n_pad - n_in), w_in.dtype)], axis=1).astype(BF16)
    proj2 = proj_modulated(xa.reshape(b * l, d), mix_g, mod16, w_perm, l, n_lat, 0, 1, _pick_tn(n_pad))
    dtr = proj2[:, n_in - 2 * SSD_HEADS:n_in]
    dt = jax.nn.softplus(dtr.reshape(b, l, 2, SSD_HEADS) + dt_bias)
    a_neg = -jnp.exp(a_log)
    da = jnp.concatenate([dt, dt * a_neg], axis=-1)
    da = jnp.moveaxis(da, 2, 1)
    dat = jnp.swapaxes(da, 2, 3)
    xs_bf, bm, cm, sc_bf = even_prep(proj2, conv_w, conv_b, sc_conv_w, l, n_lat, d)
    xs_bf = xs_bf.reshape(b, l, d)
    bm = bm.reshape(b, l, gn)
    cm = cm.reshape(b, l, gn)
    bt = jnp.swapaxes(bm, 1, 2)
    yf = ssd_scan(xs_bf, bt, bm, cm, da, dat, n_lat, rev=False)
    post = (yf, proj2, jnp.repeat(d_skip, HEAD_DIM).reshape(1, d), norm_g.reshape(1, d))
    y_bf = ssd_scan(xs_bf, bt, bm, cm, da, dat, n_lat, rev=True, post=post)
    return matmul_residual([y_bf.reshape(b * l, d), sc_bf], w_out.astype(BF16), xa.reshape(b * l, d), mod16,
                           l, n_lat, 2).reshape(b, l, d)


def _odd_mixer(xa, mix_g, mod16, n_lat, w_in, w_out, na_qk_g, na_rel_bias, df_qk_g, df_lambda, df_subln_g,
               lambda_init, rope_tabs):
    b, l, d = xa.shape
    d_na = NA_HEADS * HEAD_DIM
    dq_w = DF_HEADS * 2 * HEAD_DIM
    n_ctx = l - n_lat
    reps = ODD_TN // HEAD_DIM
    ones = jnp.ones((ODD_TN,), F32)
    gains6 = jnp.stack([jnp.tile(na_qk_g[0], reps), jnp.tile(na_qk_g[1], reps), ones,
                        jnp.tile(df_qk_g[0], reps), jnp.tile(df_qk_g[1], reps), ones])[:, None, :]
    qkv = proj_odd(xa.reshape(b * l, d), mix_g, mod16, w_in.astype(BF16), gains6, *rope_tabs, l, n_lat, 0, 1)
    qkv = qkv.reshape(b, l, -1)
    lam_p = df_lambda.astype(F32)
    na_slabs = d_na // LANES
    df_slabs = dq_w // LANES
    na_lat = na_attention(qkv, na_bias_table(na_rel_bias), n_lat)
    na_ctx = flash_slabs(qkv, lam_p, "pair", n_ctx, n_ctx, n_ctx, n_ctx, n_lat // n_ctx, n_lat // n_ctx,
                         0, na_slabs, 2 * na_slabs, na_slabs)
    tk = next(t for t in (1408, 1280, 1024, 768, 512, 256) if l % t == 0)
    dq0 = 3 * na_slabs
    df_lat = flash_slabs(qkv, lam_p, "diff", 256, tk, n_lat, l, 0, 0, dq0, dq0 + df_slabs, dq0 + 2 * df_slabs,
                         df_slabs, lambda_init)
    df_ctx = flash_slabs(qkv, lam_p, "diff", n_ctx, n_ctx, n_ctx, n_ctx, n_lat // n_ctx, n_lat // n_ctx,
                         dq0, dq0 + df_slabs, dq0 + 2 * df_slabs, df_slabs, lambda_init)
    na_o = jnp.concatenate([na_lat, na_ctx], axis=1)
    df_o = jnp.concatenate([df_lat, df_ctx], axis=1).reshape(b, l, DF_HEADS, 2 * HEAD_DIM)
    df_o = (_rms(df_o, df_subln_g) * (1.0 - lambda_init)).reshape(b, l, dq_w)
    parts = [na_o.reshape(b * l, d_na).astype(BF16), df_o.reshape(b * l, dq_w).astype(BF16)]
    return matmul_residual(parts, w_out.astype(BF16), xa.reshape(b * l, d), mod16, l, n_lat, 2).reshape(b, l, d)


def kernel(x, c, ctx, c_ctx, ada_w, ada_b, mix_norm_g, ffn_norm_g, router_w, router_b, moe_w_gu, moe_b_gu,
           moe_w_down, moe_b_down, ev_w_in, ev_w_out, ssd_conv_w, ssd_conv_b, ssd_a_log, ssd_dt_bias, ssd_d,
           ssd_norm_g, sc_conv_w, od_w_in, od_w_out, na_qk_g, na_rel_bias, df_qk_g, df_lambda, df_subln_g):
    b, s, d = x.shape
    n_ctx = ctx.shape[1]
    l = s + n_ctx
    depth = ada_w.shape[0]
    de = moe_w_down.shape[2]
    ne = moe_w_gu.shape[1]
    n_streams = N_STREAMS if b % N_STREAMS == 0 else 1
    bs = b // n_streams
    xas = [jnp.concatenate([x[k * bs:(k + 1) * bs], ctx[k * bs:(k + 1) * bs]], axis=1)
           for k in range(n_streams)]
    cos_l, sin_l = _axial_rope(s)
    cos_t = jnp.concatenate([cos_l, jnp.ones((n_ctx, HEAD_DIM), F32)], axis=0)
    sin_t = jnp.concatenate([sin_l, jnp.zeros((n_ctx, HEAD_DIM), F32)], axis=0)
    even_q = ((jnp.arange(HEAD_DIM) // (HEAD_DIM // 4)) % 2 == 0)[None, :]
    rope_tabs = tuple(jnp.tile(t, (1, LANES // HEAD_DIM))
                      for t in (cos_t, jnp.where(even_q, -sin_t, 0.0), jnp.where(even_q, 0.0, sin_t)))
    cond = jnp.concatenate([jax.nn.silu(c), jax.nn.silu(c_ctx)[None, :]], axis=0)
    cond_pad = jnp.zeros((16, d), F32).at[:b + 1].set(cond).astype(BF16)
    wgu_all = moe_w_gu.reshape(depth * ne, d, 2 * de)
    bgu_all = moe_b_gu.reshape(depth * ne, 2 * de // GU_BLK, LANES, 2).swapaxes(-1, -2).reshape(depth * ne, 1, 2 * de)
    wd_all = moe_w_down.reshape(depth * ne, de, d)
    bd_all = moe_b_down.reshape(depth * ne, 1, d)

    for i in range(depth):
        j = i // 2
        mod = matmul(cond_pad, ada_w[i].astype(BF16), 16, 6 * d // 4)[:b + 1] + ada_b[i]
        mod6 = mod.reshape(b + 1, 6, d)
        pad2 = jnp.zeros((b, CTX_ROW0 - 6, d), F32)
        mod16_all = jnp.concatenate([mod6[:b], pad2, jnp.broadcast_to(mod6[b], (b, 6, d)), pad2], axis=1)
        for k in range(n_streams):
            xa = xas[k]
            mod16 = mod16_all[k * bs:(k + 1) * bs]
            if i % 2 == 0:
                xa = _even_mixer(xa, mix_norm_g[i], mod16, s, ev_w_in[j], ev_w_out[j], ssd_conv_w[j],
                                 ssd_conv_b[j], ssd_a_log[j], ssd_dt_bias[j], ssd_d[j], ssd_norm_g[j], sc_conv_w[j])
            else:
                lambda_init = 0.8 - 0.6 * math.exp(-0.3 * i)
                xa = _odd_mixer(xa, mix_norm_g[i], mod16, s, od_w_in[j], od_w_out[j], na_qk_g[j], na_rel_bias[j],
                                df_qk_g[j], df_lambda[j], df_subln_g[j], lambda_init, rope_tabs)
            xas[k] = moe_ffn(xa.reshape(bs * l, d), ffn_norm_g[i], mod16, l, s, router_w[i], router_b[i],
                             wgu_all, bgu_all, wd_all, bd_all, i * ne).reshape(bs, l, d)
    return jnp.concatenate([xa[:, :s] for xa in xas], axis=0)
```

```python
import functools
import math

import jax
import jax.numpy as jnp
from jax import lax
from jax.experimental import pallas as pl
from jax.experimental.pallas import tpu as pltpu

F32 = jnp.float32
BF16 = jnp.bfloat16

GRID_W = 64
HEAD_DIM = 64
EPS = 1e-6
SSD_HEADS = 16
SSD_GROUPS = 4
SSD_STATE = 128
NA_ROWS = 8
NA_COLS = 16
NA_HEADS = 8
DF_HEADS = 4
N_EXPERTS = 32
TOP_K = 4
SWIGLU_LIMIT = 7.0
SWIGLU_ALPHA = 1.702
ROPE_BASE = 10000.0

LANES = 128
SSD_Q = 128
MOE_TM = 512
ROUTER_TM = 256
VMEM_BYTES = 64 * 1024 * 1024
VMEM_LIMIT = 56 * 1024 * 1024
NEG = -1e30


def _cparams(sem):
    return pltpu.CompilerParams(dimension_semantics=sem, vmem_limit_bytes=VMEM_LIMIT)


def _mm_kernel(a_ref, w_ref, o_ref):
    o_ref[...] = jnp.dot(a_ref[...], w_ref[...], preferred_element_type=F32).astype(o_ref.dtype)


def matmul(a, w, tm, tn, out_dtype=F32):
    m, k = a.shape
    n = w.shape[1]
    return pl.pallas_call(
        _mm_kernel,
        grid=(n // tn, m // tm),
        in_specs=[pl.BlockSpec((tm, k), lambda j, i: (i, 0)),
                  pl.BlockSpec((k, tn), lambda j, i: (0, j))],
        out_specs=pl.BlockSpec((tm, tn), lambda j, i: (i, j)),
        out_shape=jax.ShapeDtypeStruct((m, n), out_dtype),
        compiler_params=_cparams(("parallel", "parallel")),
        name="matmul",
    )(a, w)


MOD_ROWS = 16
CTX_ROW0 = 8
ODD_TN = 512


def _row_is_ctx(i, tm, tiles_per_batch, n_lat):
    t = i % tiles_per_batch
    row = t * tm + lax.broadcasted_iota(jnp.int32, (tm, 1), 0)
    return row >= n_lat


def _mod_row(mod, is_ctx, which):
    return jnp.where(is_ctx, mod[CTX_ROW0 + which:CTX_ROW0 + which + 1], mod[which:which + 1])


def _modulated_rows(x_ref, g_ref, mod_ref, is_ctx, sh_row, sc_row):
    x = x_ref[...]
    y = x * lax.rsqrt(jnp.mean(x * x, axis=-1, keepdims=True) + EPS) * g_ref[...]
    mod = mod_ref[0]
    return y * (1.0 + _mod_row(mod, is_ctx, sc_row)) + _mod_row(mod, is_ctx, sh_row)


def _proj_kernel(x_ref, g_ref, mod_ref, w_ref, o_ref, h_ref, *, tiles_per_batch, n_lat, sh_row, sc_row):
    i = pl.program_id(0)

    @pl.when(pl.program_id(1) == 0)
    def _():
        is_ctx = _row_is_ctx(i, x_ref.shape[0], tiles_per_batch, n_lat)
        h_ref[...] = _modulated_rows(x_ref, g_ref, mod_ref, is_ctx, sh_row, sc_row).astype(BF16)

    o_ref[...] = jnp.dot(h_ref[...], w_ref[...], preferred_element_type=F32)


def _group_rms(a, gsum, gain):
    outs = []
    half = gsum.shape[0]
    for hf in range(a.shape[1] // half):
        ah = a[:, hf * half:(hf + 1) * half]
        ms = jnp.dot((ah * ah).astype(BF16), gsum, preferred_element_type=F32) * (1.0 / HEAD_DIM)
        outs.append(ah * lax.rsqrt(ms + EPS))
    return jnp.concatenate(outs, axis=1) * gain


def _proj_odd_kernel(x_ref, g_ref, mod_ref, w_ref, gain_ref, gsum_ref, cos_ref, sina_ref, sinb_ref, o_ref, h_ref, *,
                     tiles_per_batch, n_lat, sh_row, sc_row):
    i = pl.program_id(0)
    j = pl.program_id(1)

    @pl.when(j == 0)
    def _():
        is_ctx = _row_is_ctx(i, x_ref.shape[0], tiles_per_batch, n_lat)
        h_ref[...] = _modulated_rows(x_ref, g_ref, mod_ref, is_ctx, sh_row, sc_row).astype(BF16)

    acc = jnp.dot(h_ref[...], w_ref[...], preferred_element_type=F32)
    is_norm = (j == 0) | (j == 1) | (j == 3) | (j == 4)
    is_rope = (j == 3) | (j == 4)

    @pl.when(jnp.logical_not(is_norm))
    def _():
        o_ref[...] = acc.astype(BF16)

    @pl.when(is_norm & jnp.logical_not(is_rope))
    def _():
        o_ref[...] = _group_rms(acc, gsum_ref[...], gain_ref[0]).astype(BF16)

    @pl.when(is_rope)
    def _():
        xn = _group_rms(acc, gsum_ref[...], gain_ref[0])
        reps = xn.shape[1] // LANES
        cos = jnp.concatenate([cos_ref[...]] * reps, axis=1)
        sina = jnp.concatenate([sina_ref[...]] * reps, axis=1)
        sinb = jnp.concatenate([sinb_ref[...]] * reps, axis=1)
        quarter = HEAD_DIM // 4
        up = pltpu.roll(xn, xn.shape[1] - quarter, axis=1)
        dn = pltpu.roll(xn, quarter, axis=1)
        o_ref[...] = (xn * cos + up * sina + dn * sinb).astype(BF16)


def _pick_rows(l):
    for tm in (768, 512, 256):
        if l % tm == 0:
            return tm
    raise ValueError(l)


def proj_modulated(x2, gain, mod16, w_bf, l, n_lat, sh_row, sc_row, tn):
    m, d = x2.shape
    n = w_bf.shape[1]
    tm = _pick_rows(l)
    tpb = l // tm
    return pl.pallas_call(
        functools.partial(_proj_kernel, tiles_per_batch=tpb, n_lat=n_lat, sh_row=sh_row, sc_row=sc_row),
        grid=(m // tm, n // tn),
        in_specs=[pl.BlockSpec((tm, d), lambda i, j: (i, 0)),
                  pl.BlockSpec((1, d), lambda i, j: (0, 0)),
                  pl.BlockSpec((1, MOD_ROWS, d), lambda i, j: (i // tpb, 0, 0)),
                  pl.BlockSpec((d, tn), lambda i, j: (0, j))],
        out_specs=pl.BlockSpec((tm, tn), lambda i, j: (i, j)),
        out_shape=jax.ShapeDtypeStruct((m, n), F32),
        scratch_shapes=[pltpu.VMEM((tm, d), BF16)],
        compiler_params=_cparams(("parallel", "arbitrary")),
        name="proj_even",
    )(x2, gain.reshape(1, d), mod16, w_bf)


def proj_odd(x2, gain, mod16, w_bf, gains6, cos2, sina2, sinb2, l, n_lat, sh_row, sc_row):
    m, d = x2.shape
    n = w_bf.shape[1]
    tn = ODD_TN
    tm = _pick_rows(l)
    tpb = l // tm
    half = 256
    gi = jnp.arange(half) // HEAD_DIM
    gsum = (gi[:, None] == gi[None, :]).astype(BF16)
    tab = pl.BlockSpec((tm, LANES), lambda i, j: (i % tpb, 0))
    return pl.pallas_call(
        functools.partial(_proj_odd_kernel, tiles_per_batch=tpb, n_lat=n_lat, sh_row=sh_row, sc_row=sc_row),
        grid=(m // tm, n // tn),
        in_specs=[pl.BlockSpec((tm, d), lambda i, j: (i, 0)),
                  pl.BlockSpec((1, d), lambda i, j: (0, 0)),
                  pl.BlockSpec((1, MOD_ROWS, d), lambda i, j: (i // tpb, 0, 0)),
                  pl.BlockSpec((d, tn), lambda i, j: (0, j)),
                  pl.BlockSpec((1, 1, tn), lambda i, j: (j, 0, 0)),
                  pl.BlockSpec((half, half), lambda i, j: (0, 0)),
                  tab, tab, tab],
        out_specs=pl.BlockSpec((tm, tn), lambda i, j: (i, j)),
        out_shape=jax.ShapeDtypeStruct((m, n), BF16),
        scratch_shapes=[pltpu.VMEM((tm, d), BF16)],
        compiler_params=_cparams(("parallel", "arbitrary")),
        name="proj_odd",
    )(x2, gain.reshape(1, d), mod16, w_bf, gains6, gsum, cos2, sina2, sinb2)


def _mm_res_kernel(*refs, n_parts, tiles_per_batch, n_lat, gate_row):
    a_refs = refs[:n_parts]
    w_ref, r_ref, mod_ref, o_ref = refs[n_parts:]
    is_ctx = _row_is_ctx(pl.program_id(0), r_ref.shape[0], tiles_per_batch, n_lat)
    gate = _mod_row(mod_ref[0], is_ctx, gate_row)
    acc = None
    k0 = 0
    for a_ref in a_refs:
        k1 = k0 + a_ref.shape[1]
        t = jnp.dot(a_ref[...], w_ref[k0:k1, :], preferred_element_type=F32)
        acc = t if acc is None else acc + t
        k0 = k1
    o_ref[...] = r_ref[...] + gate * acc


def matmul_residual(parts, w_bf, res, mod16, l, n_lat, gate_row):
    m = res.shape[0]
    k, d = w_bf.shape
    tm = _pick_rows(l)
    tpb = l // tm
    return pl.pallas_call(
        functools.partial(_mm_res_kernel, n_parts=len(parts), tiles_per_batch=tpb, n_lat=n_lat, gate_row=gate_row),
        grid=(m // tm,),
        in_specs=[pl.BlockSpec((tm, a.shape[1]), lambda i: (i, 0)) for a in parts]
                 + [pl.BlockSpec((k, d), lambda i: (0, 0)),
                    pl.BlockSpec((tm, d), lambda i: (i, 0)),
                    pl.BlockSpec((1, MOD_ROWS, d), lambda i: (i // tpb, 0, 0))],
        out_specs=pl.BlockSpec((tm, d), lambda i: (i, 0)),
        out_shape=jax.ShapeDtypeStruct((m, d), F32),
        compiler_params=_cparams(("parallel",)),
        name="out_proj",
    )(*parts, w_bf, res, mod16)


EV_TM = 256
HALO = 8


def _even_prep_kernel(xa_ref, xa_p, xa_n, xb_ref, xb_p, xb_n, gb_ref, gc_ref, gc_p, gc_n, gh_ref, gh_p, gh_n,
                      cw_ref, cb_ref, sw_ref, xs_ref, bm_ref, cm_ref, sc_ref, scr, *, tiles_per_batch, n_lat, l):
    tm = xa_ref.shape[0]
    r0 = (pl.program_id(0) % tiles_per_batch) * tm
    has_prev = (r0 != 0) & (r0 != n_lat)
    has_next = (r0 + tm != n_lat) & (r0 + tm != l)

    def conv(main, prev, nxt, w):
        width = w.shape[0]
        scr[0:HALO] = jnp.where(has_prev, prev, 0.0)
        scr[HALO:HALO + tm] = main
        scr[HALO + tm:2 * HALO + tm] = jnp.where(has_next, nxt, 0.0)
        out = None
        for kk in range(width):
            o = HALO + kk - width // 2
            t = scr[o:o + tm] * w[kk:kk + 1]
            out = t if out is None else out + t
        return out

    d = xa_ref.shape[1]
    cw = cw_ref[...]
    cb = cb_ref[...]
    xs = conv(xa_ref[...], xa_p[...], xa_n[...], cw[:, :d]) + cb[:, :d]
    xs_ref[...] = (xs * jax.nn.sigmoid(xs)).astype(BF16)
    bc = conv(xb_ref[...], xb_p[...], xb_n[...], cw[:, d:]) + cb[:, d:]
    bc = (bc * jax.nn.sigmoid(bc)).astype(BF16)
    gn = bm_ref.shape[1]
    bm_ref[...] = bc[:, :gn]
    cm_ref[...] = bc[:, gn:]
    sc = gb_ref[...] * conv(gc_ref[...] * gh_ref[...], gc_p[...] * gh_p[...], gc_n[...] * gh_n[...], sw_ref[...])
    sc_ref[...] = sc.astype(BF16)


def even_prep(proj2, conv_w, conv_b, sc_conv_w, l, n_lat, d):
    m = proj2.shape[0]
    tm = EV_TM
    tpb = l // tm
    nhb = m // HALO
    gn = SSD_GROUPS * SSD_STATE

    def main(cblk):
        return pl.BlockSpec((tm, d), lambda i: (i, cblk))

    def prev(cblk):
        return pl.BlockSpec((HALO, d), lambda i: (jnp.maximum(i * (tm // HALO) - 1, 0), cblk))

    def nxt(cblk):
        return pl.BlockSpec((HALO, d), lambda i: (jnp.minimum((i + 1) * (tm // HALO), nhb - 1), cblk))

    full = lambda a: pl.BlockSpec(a.shape, lambda i: (0, 0))
    cb2 = conv_b.reshape(1, -1)
    return pl.pallas_call(
        functools.partial(_even_prep_kernel, tiles_per_batch=tpb, n_lat=n_lat, l=l),
        grid=(m // tm,),
        in_specs=[main(1), prev(1), nxt(1), main(2), prev(2), nxt(2), main(3),
                  main(4), prev(4), nxt(4), main(5), prev(5), nxt(5),
                  full(conv_w), full(cb2), full(sc_conv_w)],
        out_specs=[pl.BlockSpec((tm, d), lambda i: (i, 0)), pl.BlockSpec((tm, gn), lambda i: (i, 0)),
                   pl.BlockSpec((tm, gn), lambda i: (i, 0)), pl.BlockSpec((tm, d), lambda i: (i, 0))],
        out_shape=[jax.ShapeDtypeStruct((m, d), BF16), jax.ShapeDtypeStruct((m, gn), BF16),
                   jax.ShapeDtypeStruct((m, gn), BF16), jax.ShapeDtypeStruct((m, d), BF16)],
        scratch_shapes=[pltpu.VMEM((tm + 2 * HALO, d), F32)],
        compiler_params=_cparams(("parallel",)),
        name="even_prep",
    )(proj2, proj2, proj2, proj2, proj2, proj2, proj2, proj2, proj2, proj2, proj2, proj2, proj2,
      conv_w, cb2, sc_conv_w)


GU_BLK = 2 * LANES


def _gate_up_perm():
    src = jnp.arange(GU_BLK)
    dst = jnp.where(src % 2 == 0, src // 2, LANES + src // 2)
    return (dst[:, None] == jnp.arange(GU_BLK)[None, :]).astype(BF16)


def _deinterleave_into(w_ref, perm, o_ref):
    for blk in range(o_ref.shape[1] // GU_BLK):
        cs = slice(blk * GU_BLK, (blk + 1) * GU_BLK)
        o_ref[:, cs] = jnp.dot(w_ref[0, :, cs].astype(BF16), perm, preferred_element_type=F32).astype(BF16)


def _split3_dot(tri, a, dims):
    a1 = a.astype(BF16)
    r1 = a - a1.astype(F32)
    a2 = r1.astype(BF16)
    a3 = (r1 - a2.astype(F32)).astype(BF16)
    out = None
    for piece in (a1, a2, a3):
        if dims == "tri_a":
            t = jnp.dot(tri, piece, preferred_element_type=F32)
        else:
            t = jnp.dot(piece, tri, preferred_element_type=F32)
        out = t if out is None else out + t
    return out


def _ssd_kernel(*refs, rev, post):
    if post:
        x_ref, bt_ref, b_ref, c_ref, da_ref, dat_ref, yo_ref, z_ref, dsk_ref, ng_ref, y_ref, st_ref, yacc = refs
    else:
        x_ref, bt_ref, b_ref, c_ref, da_ref, dat_ref, y_ref, st_ref = refs
    q = SSD_Q
    step = pl.program_id(1)

    @pl.when(step == 0)
    def _():
        st_ref[...] = jnp.zeros_like(st_ref)

    da = da_ref[0, 0]
    dat = dat_ref[0, 0]
    a_c = da[:, SSD_HEADS:2 * SSD_HEADS]
    dt_r = dat[0:SSD_HEADS]
    a_r = dat[SSD_HEADS:2 * SSD_HEADS]
    ri = lax.broadcasted_iota(jnp.int32, (q, q), 0)
    ci = lax.broadcasted_iota(jnp.int32, (q, q), 1)
    tri = (ci <= ri).astype(BF16)
    tri_t = (ri <= ci).astype(BF16)
    cum_c = _split3_dot(tri, a_c, "tri_a")
    cum_r = _split3_dot(tri_t, a_r, "a_tri")
    tot_r = cum_r[:, q - 1:q]
    if rev:
        pos_c = cum_c - a_c
        pos_r = cum_r - a_r
        mask = ci >= ri
    else:
        pos_c = cum_c
        pos_r = cum_r
        mask = ri >= ci
    lane = lax.broadcasted_iota(jnp.int32, (q, LANES), 1)
    lane_n = lax.broadcasted_iota(jnp.int32, (SSD_STATE, LANES), 1)
    heads_per_group = SSD_HEADS // SSD_GROUPS

    for g in range(SSD_GROUPS):
        gs = slice(g * SSD_STATE, (g + 1) * SSD_STATE)
        bg = b_ref[0, :, gs]
        cg = c_ref[0, :, gs]
        btg = bt_ref[0, gs, :].astype(F32)
        cb = lax.dot_general(cg, bg, (((1,), (1,)), ((), ())), preferred_element_type=F32)
        cg32 = cg.astype(F32)
        for pp in range(heads_per_group // 2):
            p = g * (heads_per_group // 2) + pp
            xp = x_ref[0, :, p * LANES:(p + 1) * LANES]
            st = st_ref[p]
            rhs = jnp.concatenate([xp, st.astype(BF16)], axis=0)
            ys, ds, decs = [], [], []
            for hh in range(2):
                h = 2 * p + hh
                colb = jnp.broadcast_to(pos_c[:, h:h + 1], (q, LANES))
                row = pos_r[h:h + 1, :]
                dt_row = dt_r[h:h + 1, :]
                tot = tot_r[h:h + 1, :]
                if rev:
                    seg = row - colb
                    coff = jnp.exp(tot - colb)
                    w_row = dt_row * jnp.exp(row)
                else:
                    seg = colb - row
                    coff = jnp.exp(colb)
                    w_row = dt_row * jnp.exp(tot - row)
                decay = jnp.exp(jnp.where(mask, seg, NEG))
                m_h = (cb * decay * dt_row).astype(BF16)
                c_h = (cg32 * coff).astype(BF16)
                lhs = jnp.concatenate([m_h, c_h], axis=1)
                ys.append(jnp.dot(lhs, rhs, preferred_element_type=F32))
                btw = (btg * w_row).astype(BF16)
                ds.append(jnp.dot(btw, xp, preferred_element_type=F32))
                decs.append(jnp.exp(tot))
            y_pair = jnp.where(lane < HEAD_DIM, ys[0], ys[1])
            if post:
                yacc[:, p * LANES:(p + 1) * LANES] = y_pair
            else:
                y_ref[0, :, p * LANES:(p + 1) * LANES] = y_pair
            st_ref[p] = jnp.where(lane_n < HEAD_DIM, decs[0] * st + ds[0], decs[1] * st + ds[1])

    if post:
        z = z_ref[...]
        y = (yacc[...] + yo_ref[0] + dsk_ref[...] * x_ref[0].astype(F32)) * (z * jax.nn.sigmoid(z))
        y = y * lax.rsqrt(jnp.mean(y * y, axis=-1, keepdims=True) + EPS) * ng_ref[...]
        y_ref[0] = y.astype(y_ref.dtype)


def ssd_scan(xs, bt, bm, cm, da, dat, n_lat, rev, post=None):
    b, l, d = xs.shape
    nch = l // SSD_Q
    nlat = n_lat // SSD_Q
    nctx = nch - nlat
    d_idx = 1 if rev else 0
    if rev:
        def chunk(j):
            return nch - 1 - j
    else:
        def chunk(j):
            return jnp.where(j < nctx, nlat + j, j - nctx)
    gn = SSD_GROUPS * SSD_STATE
    tok = pl.BlockSpec((1, SSD_Q, d), lambda i, j: (i, chunk(j), 0))
    in_specs = [tok,
                pl.BlockSpec((1, gn, SSD_Q), lambda i, j: (i, 0, chunk(j))),
                pl.BlockSpec((1, SSD_Q, gn), lambda i, j: (i, chunk(j), 0)),
                pl.BlockSpec((1, SSD_Q, gn), lambda i, j: (i, chunk(j), 0)),
                pl.BlockSpec((1, 1, SSD_Q, 2 * SSD_HEADS), lambda i, j: (i, d_idx, chunk(j), 0)),
                pl.BlockSpec((1, 1, 2 * SSD_HEADS, SSD_Q), lambda i, j: (i, d_idx, 0, chunk(j)))]
    scratch = [pltpu.VMEM((SSD_HEADS // 2, SSD_STATE, LANES), F32)]
    args = (xs, bt, bm, cm, da, dat)
    if post is not None:
        row = pl.BlockSpec((1, d), lambda i, j: (0, 0))
        in_specs += [tok, pl.BlockSpec((SSD_Q, d), lambda i, j: (i * nch + chunk(j), 0)), row, row]
        scratch.append(pltpu.VMEM((SSD_Q, d), F32))
        args += tuple(post)
    return pl.pallas_call(
        functools.partial(_ssd_kernel, rev=rev, post=post is not None),
        grid=(b, nch),
        in_specs=in_specs,
        out_specs=tok,
        out_shape=jax.ShapeDtypeStruct((b, l, d), F32 if post is None else BF16),
        scratch_shapes=scratch,
        compiler_params=_cparams(("parallel", "arbitrary")),
        name="ssd_bwd" if rev else "ssd_fwd",
    )(*args)


def _stack_halves(qv):
    lane = lax.broadcasted_iota(jnp.int32, qv.shape, 1)
    zero = jnp.zeros_like(qv)
    q1 = jnp.where(lane < HEAD_DIM, qv, zero)
    q2 = jnp.where(lane >= HEAD_DIM, qv, zero)
    return jnp.concatenate([q1, q2], axis=0) * jnp.asarray(HEAD_DIM ** -0.5, qv.dtype)


def _flash_kernel(lam_ref, q_ref, k_ref, v_ref, o_ref, *, mode, tk, lambda_init):
    tq = q_ref.shape[1]
    nk = k_ref.shape[1] // tk
    nt = tk // LANES
    qs = _stack_halves(q_ref[0])

    def scores(u):
        return lax.dot_general(qs, k_ref[0, u * tk:(u + 1) * tk, :], (((1,), (1,)), ((), ())),
                               preferred_element_type=F32)

    m_old = jnp.full((2 * tq, LANES), -jnp.inf, F32)
    l_run = jnp.zeros((2 * tq, LANES), F32)
    acc = jnp.zeros((2 * tq, LANES), F32)
    s_next = scores(0)
    for u in range(nk):
        s = s_next
        if u + 1 < nk:
            s_next = scores(u + 1)
        tiles = [s[:, t * LANES:(t + 1) * LANES] for t in range(nt)]
        smax = tiles[0]
        for t in tiles[1:]:
            smax = jnp.maximum(smax, t)
        m_new = jnp.maximum(m_old, jnp.max(smax, axis=-1, keepdims=True))
        alpha = jnp.exp(m_old - m_new)
        ps = [jnp.exp(t - m_new) for t in tiles]
        psum = ps[0]
        for t in ps[1:]:
            psum = psum + t
        l_run = alpha * l_run + jnp.sum(psum, axis=-1, keepdims=True)
        p = jnp.concatenate([t.astype(BF16) for t in ps], axis=1)
        acc = alpha * acc + jnp.dot(p, v_ref[0, u * tk:(u + 1) * tk, :], preferred_element_type=F32)
        m_old = m_new
    o = acc / l_run
    o1, o2 = o[:tq], o[tq:]
    if mode == "diff":
        lp = lam_ref[...]
        s01 = jnp.sum(lp[0:1] * lp[1:2], axis=-1, keepdims=True)
        s23 = jnp.sum(lp[2:3] * lp[3:4], axis=-1, keepdims=True)
        lam = jnp.exp(s01) - jnp.exp(s23) + lambda_init
        out = o1 - lam * o2
    else:
        lane = lax.broadcasted_iota(jnp.int32, o1.shape, 1)
        out = jnp.where(lane < HEAD_DIM, o1, o2)
    o_ref[0] = out.astype(o_ref.dtype)


def flash_slabs(qkv, lam_p, mode, tq, tk, lq, lk, q_blk0, k_blk, q_slab0, k_slab0, v_slab0, n_slabs,
                lambda_init=0.0):
    b = qkv.shape[0]
    return pl.pallas_call(
        functools.partial(_flash_kernel, mode=mode, tk=tk, lambda_init=lambda_init),
        grid=(b, n_slabs, lq // tq),
        in_specs=[pl.BlockSpec(lam_p.shape, lambda i, h, j: (0, 0)),
                  pl.BlockSpec((1, tq, LANES), lambda i, h, j: (i, q_blk0 + j, q_slab0 + h)),
                  pl.BlockSpec((1, lk, LANES), lambda i, h, j: (i, k_blk, k_slab0 + h)),
                  pl.BlockSpec((1, lk, LANES), lambda i, h, j: (i, k_blk, v_slab0 + h))],
        out_specs=pl.BlockSpec((1, tq, LANES), lambda i, h, j: (i, j, h)),
        out_shape=jax.ShapeDtypeStruct((b, lq, n_slabs * LANES), F32),
        compiler_params=_cparams(("parallel", "parallel", "arbitrary")),
        name="flash_" + mode,
    )(lam_p, qkv, qkv, qkv)


NA_RB = 8
NA_BLK = NA_RB * GRID_W
NA_WIN = NA_ROWS * GRID_W


def _na_kernel(q_ref, kp_ref, kc_ref, kn_ref, vp_ref, vc_ref, vn_ref, kx_ref, vx_ref, bias_ref, o_ref,
               kbuf, vbuf, *, rows):
    rb = pl.program_id(1)
    kbuf[0:NA_BLK] = kp_ref[0]
    kbuf[NA_BLK:2 * NA_BLK] = kc_ref[0]
    kbuf[2 * NA_BLK:3 * NA_BLK] = kn_ref[0]
    vbuf[0:NA_BLK] = vp_ref[0]
    vbuf[NA_BLK:2 * NA_BLK] = vc_ref[0]
    vbuf[2 * NA_BLK:3 * NA_BLK] = vn_ref[0]
    npairs = q_ref.shape[2] // LANES
    lane = lax.broadcasted_iota(jnp.int32, (GRID_W, LANES), 1)

    def row_body(rl, carry):
        r = rb * NA_RB + rl
        r_start = jnp.clip(r - NA_ROWS // 2, 0, rows - NA_ROWS)
        off = r_start - (rb * NA_RB - NA_RB)
        di0 = r_start - r + NA_ROWS - 1
        tok0 = pl.multiple_of(off * GRID_W, GRID_W)
        q0 = pl.multiple_of(rl * GRID_W, GRID_W)
        for p in range(npairs):
            ls = slice(p * LANES, (p + 1) * LANES)
            qs = _stack_halves(q_ref[0, pl.ds(q0, GRID_W), ls])
            kw = kbuf[pl.ds(tok0, NA_WIN), ls]
            vw = vbuf[pl.ds(tok0, NA_WIN), ls]
            s_nb = lax.dot_general(qs, kw, (((1,), (1,)), ((), ())), preferred_element_type=F32)
            bias = jnp.concatenate([bias_ref[p, di0 + 2 * j] for j in range(NA_ROWS // 2)], axis=1)
            s_nb = s_nb + bias
            s_cx = lax.dot_general(qs, kx_ref[0, :, ls], (((1,), (1,)), ((), ())), preferred_element_type=F32)
            m = jnp.maximum(jnp.max(s_nb, axis=-1, keepdims=True), jnp.max(s_cx, axis=-1, keepdims=True))
            p_nb = jnp.exp(s_nb - m)
            p_cx = jnp.exp(s_cx - m)
            l = jnp.sum(p_nb, axis=-1, keepdims=True) + jnp.sum(p_cx, axis=-1, keepdims=True)
            o = (jnp.dot(p_nb.astype(BF16), vw, preferred_element_type=F32)
                 + jnp.dot(p_cx.astype(BF16), vx_ref[0, :, ls], preferred_element_type=F32)) / l
            o_ref[0, pl.ds(q0, GRID_W), ls] = jnp.where(lane < HEAD_DIM, o[:GRID_W], o[GRID_W:])
        return carry

    lax.fori_loop(0, NA_RB, row_body, 0)


def na_bias_table(rel_bias):
    cols = jnp.arange(GRID_W)
    c_start = jnp.clip(cols - NA_COLS // 2, 0, GRID_W - NA_COLS)
    kc = jnp.arange(GRID_W)
    valid = (kc[None, :] >= c_start[:, None]) & (kc[None, :] < c_start[:, None] + NA_COLS)
    idx = jnp.clip(kc[None, :] - cols[:, None] + NA_COLS - 1, 0, 2 * NA_COLS - 2)
    t = jnp.where(valid[None, None], rel_bias[:, :, idx], NEG)
    t2 = jnp.concatenate([t[:, :-1], t[:, 1:]], axis=-1)
    nh, nd = t2.shape[0], t2.shape[1]
    t2 = t2.reshape(nh // 2, 2, nd, GRID_W, LANES).transpose(0, 2, 1, 3, 4)
    return t2.reshape(nh // 2, nd, 2 * GRID_W, LANES).astype(F32)


def na_attention(qkv, bias_tab, n_lat):
    b, l, _ = qkv.shape
    w = NA_HEADS * HEAD_DIM
    rows = n_lat // GRID_W
    nrb = rows // NA_RB
    n_ctx = l - n_lat
    ctx_blk = n_lat // n_ctx

    def blk(step, stream):
        if step < 0:
            return pl.BlockSpec((1, NA_BLK, w), lambda i, j: (i, jnp.maximum(j - 1, 0), stream))
        if step > 0:
            return pl.BlockSpec((1, NA_BLK, w), lambda i, j: (i, jnp.minimum(j + 1, nrb - 1), stream))
        return pl.BlockSpec((1, NA_BLK, w), lambda i, j: (i, j, stream))

    def ctx(stream):
        return pl.BlockSpec((1, n_ctx, w), lambda i, j: (i, ctx_blk, stream))

    return pl.pallas_call(
        functools.partial(_na_kernel, rows=rows),
        grid=(b, nrb),
        in_specs=[blk(0, 0), blk(-1, 1), blk(0, 1), blk(1, 1), blk(-1, 2), blk(0, 2), blk(1, 2), ctx(1), ctx(2),
                  pl.BlockSpec(bias_tab.shape, lambda i, j: (0, 0, 0, 0))],
        out_specs=pl.BlockSpec((1, NA_BLK, w), lambda i, j: (i, j, 0)),
        out_shape=jax.ShapeDtypeStruct((b, n_lat, w), F32),
        scratch_shapes=[pltpu.VMEM((3 * NA_BLK, w), BF16), pltpu.VMEM((3 * NA_BLK, w), BF16)],
        compiler_params=_cparams(("parallel", "parallel")),
        name="na_attention",
    )(qkv, qkv, qkv, qkv, qkv, qkv, qkv, qkv, qkv, bias_tab)


def _router_kernel(x_ref, g_ref, sh_ref, sc_ref, w_ref, b_ref, h_ref, ri_ref, rg_ref, cnt_ref, carry_ref):
    step = pl.program_id(0)

    @pl.when(step == 0)
    def _():
        carry_ref[...] = jnp.zeros_like(carry_ref)

    tm = x_ref.shape[0]
    x = x_ref[...]
    y = x * lax.rsqrt(jnp.mean(x * x, axis=-1, keepdims=True) + EPS) * g_ref[...]
    h = y * (1.0 + sc_ref[0]) + sh_ref[0]
    h_ref[...] = h.astype(BF16)
    w = w_ref[...]
    h1 = h.astype(BF16)
    h2 = (h - h1.astype(F32)).astype(BF16)
    w1 = w.astype(BF16)
    w2 = (w - w1.astype(F32)).astype(BF16)
    logits = (jnp.dot(h1, w1, preferred_element_type=F32) + jnp.dot(h2, w1, preferred_element_type=F32)
              + jnp.dot(h1, w2, preferred_element_type=F32)) + b_ref[...]
    lane = lax.broadcasted_iota(jnp.int32, (tm, LANES), 1)
    work = logits
    tops, idxs, hots = [], [], []
    for _ in range(TOP_K):
        mx = jnp.max(work, axis=-1, keepdims=True)
        ix = jnp.min(jnp.where(work == mx, lane, LANES), axis=-1, keepdims=True)
        hot = lane == ix
        work = jnp.where(hot, -jnp.inf, work)
        tops.append(mx)
        idxs.append(ix)
        hots.append(hot)
    es = [jnp.exp(t - tops[0]) for t in tops]
    den = es[0] + es[1] + es[2] + es[3]
    multi = jnp.zeros((tm, LANES), F32)
    for hot in hots:
        multi = multi + hot.astype(F32)
    ri = lax.broadcasted_iota(jnp.int32, (tm, tm), 0)
    ci = lax.broadcasted_iota(jnp.int32, (tm, tm), 1)
    tri = (ci < ri).astype(BF16)
    cum = jnp.dot(tri, multi.astype(BF16), preferred_element_type=F32) + carry_ref[...]
    carry_new = carry_ref[...] + jnp.sum(multi, axis=0, keepdims=True)
    carry_ref[...] = carry_new
    cnt_ref[...] = carry_new
    out_i = jnp.zeros((tm, LANES), jnp.int32)
    out_g = jnp.zeros((tm, LANES), F32)
    for kk in range(TOP_K):
        rank = jnp.sum(jnp.where(hots[kk], cum, 0.0), axis=-1, keepdims=True).astype(jnp.int32)
        out_i = jnp.where(lane == kk, idxs[kk], out_i)
        out_i = jnp.where(lane == TOP_K + kk, rank, out_i)
        out_g = jnp.where(lane == kk, es[kk] / den, out_g)
    ri_ref[...] = out_i
    rg_ref[...] = out_g


def moe_router(x2, gain, modarr, sh_row, sc_row, n_lat_tiles, tiles_per_batch, router_w, router_b):
    n, d = x2.shape
    tm = ROUTER_TM
    w_pad = jnp.zeros((d, LANES), F32).at[:, :N_EXPERTS].set(router_w)
    b_pad = jnp.full((1, LANES), NEG, F32).at[0, :N_EXPERTS].set(router_b)

    def mod_idx(which):
        def f(i):
            bidx = i // tiles_per_batch
            is_ctx = (i % tiles_per_batch) >= n_lat_tiles
            return (bidx * MOD_ROWS + is_ctx.astype(jnp.int32) * CTX_ROW0 + which, 0, 0)
        return f

    return pl.pallas_call(
        _router_kernel,
        grid=(n // tm,),
        in_specs=[pl.BlockSpec((tm, d), lambda i: (i, 0)),
                  pl.BlockSpec((1, d), lambda i: (0, 0)),
                  pl.BlockSpec((1, 1, d), mod_idx(sh_row)),
                  pl.BlockSpec((1, 1, d), mod_idx(sc_row)),
                  pl.BlockSpec((d, LANES), lambda i: (0, 0)),
                  pl.BlockSpec((1, LANES), lambda i: (0, 0))],
        out_specs=[pl.BlockSpec((tm, d), lambda i: (i, 0)),
                   pl.BlockSpec((tm, LANES), lambda i: (i, 0)),
                   pl.BlockSpec((tm, LANES), lambda i: (i, 0)),
                   pl.BlockSpec((1, LANES), lambda i: (0, 0))],
        out_shape=[jax.ShapeDtypeStruct((max(n, VMEM_BYTES // (2 * d)), d), BF16),
                   jax.ShapeDtypeStruct((n, LANES), jnp.int32),
                   jax.ShapeDtypeStruct((n, LANES), F32),
                   jax.ShapeDtypeStruct((1, LANES), F32)],
        scratch_shapes=[pltpu.VMEM((1, LANES), F32)],
        compiler_params=_cparams(("arbitrary",)),
        name="moe_router",
    )(x2, gain.reshape(1, d), modarr, modarr, w_pad, b_pad)


def _expert_kernel(be_ref, nu_ref, x_ref, wgu_ref, bgu_ref, wd_ref, bd_ref, perm_ref, o_ref, wgu_bf, wd_bf):
    j = pl.program_id(0)
    used = j < nu_ref[0]
    changed = (j == 0) | (be_ref[j] != be_ref[jnp.maximum(j - 1, 0)])

    @pl.when(used & changed)
    def _():
        wd_bf[...] = wd_ref[0].astype(BF16)
        _deinterleave_into(wgu_ref, perm_ref[...], wgu_bf)

    @pl.when(used)
    def _():
        gu = jnp.dot(x_ref[...], wgu_bf[...], preferred_element_type=F32) + bgu_ref[0]
        acts = []
        for blk in range(gu.shape[1] // GU_BLK):
            gate = jnp.minimum(gu[:, blk * GU_BLK:blk * GU_BLK + LANES], SWIGLU_LIMIT)
            up = jnp.clip(gu[:, blk * GU_BLK + LANES:(blk + 1) * GU_BLK], -SWIGLU_LIMIT, SWIGLU_LIMIT)
            acts.append((gate * jax.nn.sigmoid(SWIGLU_ALPHA * gate) * (up + 1.0)).astype(BF16))
        act = jnp.concatenate(acts, axis=1)
        o_ref[...] = (jnp.dot(act, wd_bf[...], preferred_element_type=F32) + bd_ref[0]).astype(o_ref.dtype)

    @pl.when(jnp.logical_not(used))
    def _():
        o_ref[...] = jnp.zeros_like(o_ref)


def moe_experts(xs, block_expert, n_used, wgu, bgu, wd, bd, e_off):
    cap, d = xs.shape
    de = wd.shape[1]
    nblk = cap // MOE_TM
    wmap = lambda j, be, nu: (e_off + be[j], 0, 0)
    grid_spec = pltpu.PrefetchScalarGridSpec(
        num_scalar_prefetch=2,
        grid=(nblk,),
        in_specs=[pl.BlockSpec((MOE_TM, d), lambda j, be, nu: (j, 0)),
                  pl.BlockSpec((1, d, 2 * de), wmap),
                  pl.BlockSpec((1, 1, 2 * de), wmap),
                  pl.BlockSpec((1, de, d), wmap),
                  pl.BlockSpec((1, 1, d), wmap),
                  pl.BlockSpec((GU_BLK, GU_BLK), lambda j, be, nu: (0, 0))],
        out_specs=pl.BlockSpec((MOE_TM, d), lambda j, be, nu: (j, 0)),
        scratch_shapes=[pltpu.VMEM((d, 2 * de), BF16), pltpu.VMEM((de, d), BF16)],
    )
    return pl.pallas_call(
        _expert_kernel,
        grid_spec=grid_spec,
        out_shape=jax.ShapeDtypeStruct((cap, d), BF16),
        compiler_params=_cparams(("arbitrary",)),
        name="moe_experts",
    )(block_expert, n_used, xs, wgu, bgu, wd, bd, _gate_up_perm())


def moe_ffn(x2, gain, mod16, l, n_lat, router_w, router_b, wgu, bgu, wd, bd, e_off):
    n, d = x2.shape
    sh_row, sc_row, gate_row = 3, 4, 5
    h_bf, r_i, r_g, cnt = moe_router(x2, gain, mod16.reshape(-1, 1, d), sh_row, sc_row, n_lat // ROUTER_TM,
                                     l // ROUTER_TM, router_w, router_b)
    top_idx = r_i[:, :TOP_K]
    rank = r_i[:, TOP_K:2 * TOP_K]
    gates = r_g[:, :TOP_K]
    counts = cnt[0, :N_EXPERTS].astype(jnp.int32)
    padded = (counts + MOE_TM - 1) // MOE_TM * MOE_TM
    pad_end = jnp.cumsum(padded)
    pad_start = pad_end - padded
    dest = (pad_start[top_idx] + rank).T
    nblk = -(-(n * TOP_K + N_EXPERTS * (MOE_TM - 1)) // MOE_TM)
    cap = nblk * MOE_TM
    tok = jnp.broadcast_to(jnp.arange(n, dtype=jnp.int32)[None, :], (TOP_K, n))
    slot_tok = jnp.zeros((cap,), jnp.int32).at[dest.reshape(-1)].set(tok.reshape(-1), unique_indices=True)
    blk_start = jnp.arange(nblk, dtype=jnp.int32)[:, None] * MOE_TM
    block_expert = jnp.minimum(jnp.sum((pad_end[None, :] <= blk_start).astype(jnp.int32), axis=1), N_EXPERTS - 1)
    n_used = (pad_end[-1:] // MOE_TM).astype(jnp.int32)
    xs = h_bf[slot_tok]
    y = moe_experts(xs, block_expert, n_used, wgu, bgu, wd, bd, e_off)
    yk = y[dest.reshape(-1)].reshape(TOP_K, n, d)
    return moe_combine(yk, r_g, x2, mod16, l, n_lat, gate_row)


def _combine_kernel(y_ref, g_ref, r_ref, mod_ref, o_ref, *, tiles_per_batch, n_lat, gate_row):
    is_ctx = _row_is_ctx(pl.program_id(0), r_ref.shape[0], tiles_per_batch, n_lat)
    gate = _mod_row(mod_ref[0], is_ctx, gate_row)
    g = g_ref[...]
    acc = y_ref[0].astype(F32) * g[:, 0:1]
    for kk in range(1, TOP_K):
        acc = acc + y_ref[kk].astype(F32) * g[:, kk:kk + 1]
    o_ref[...] = r_ref[...] + gate * acc


def moe_combine(yk, r_g, res, mod16, l, n_lat, gate_row):
    _, n, d = yk.shape
    tm = _pick_rows(l)
    tpb = l // tm
    return pl.pallas_call(
        functools.partial(_combine_kernel, tiles_per_batch=tpb, n_lat=n_lat, gate_row=gate_row),
        grid=(n // tm,),
        in_specs=[pl.BlockSpec((TOP_K, tm, d), lambda i: (0, i, 0)),
                  pl.BlockSpec((tm, LANES), lambda i: (i, 0)),
                  pl.BlockSpec((tm, d), lambda i: (i, 0)),
                  pl.BlockSpec((1, MOD_ROWS, d), lambda i: (i // tpb, 0, 0))],
        out_specs=pl.BlockSpec((tm, d), lambda i: (i, 0)),
        out_shape=jax.ShapeDtypeStruct((n, d), F32),
        compiler_params=_cparams(("parallel",)),
        name="moe_combine",
    )(yk, r_g, res, mod16)


def _rms(x, g):
    return x * lax.rsqrt(jnp.mean(x * x, axis=-1, keepdims=True) + EPS) * g


def _axial_rope(n_tok):
    pos = jnp.arange(n_tok)
    rows = (pos // GRID_W).astype(F32)
    cols = (pos % GRID_W).astype(F32)
    quarter = HEAD_DIM // 4
    inv_freq = ROPE_BASE ** (-jnp.arange(quarter, dtype=F32) / quarter)
    ar = rows[:, None] * inv_freq
    ac = cols[:, None] * inv_freq
    ang = jnp.concatenate([ar, ar, ac, ac], axis=-1)
    return jnp.cos(ang), jnp.sin(ang)


def _pick_tn(n):
    for tn in (1536, 1280, 1024, 768, 512, 256, 128):
        if n % tn == 0:
            return tn
    raise ValueError(n)


def _even_mixer(xa, mix_g, mod16, n_lat, w_in, w_out, conv_w, conv_b, a_log, dt_bias, d_skip, norm_g, sc_conv_w):
    b, l, d = xa.shape
    gn = SSD_GROUPS * SSD_STATE
    conv_dim = d + 2 * gn
    n_in = w_in.shape[1]
    n_pad = -(-n_in // 256) * 256
    o_dt = d + conv_dim
    w_perm = jnp.concatenate([w_in[:, :o_dt], w_in[:, o_dt + 2 * SSD_HEADS:], w_in[:, o_dt:o_dt + 2 * SSD_HEADS],
                              jnp.zeros((d, n_pad - n_in), w_in.dtype)], axis=1).astype(BF16)
    proj2 = proj_modulated(xa.reshape(b * l, d), mix_g, mod16, w_perm, l, n_lat, 0, 1, _pick_tn(n_pad))
    dtr = proj2[:, n_in - 2 * SSD_HEADS:n_in]
    dt = jax.nn.softplus(dtr.reshape(b, l, 2, SSD_HEADS) + dt_bias)
    a_neg = -jnp.exp(a_log)
    da = jnp.concatenate([dt, dt * a_neg], axis=-1)
    da = jnp.moveaxis(da, 2, 1)
    dat = jnp.swapaxes(da, 2, 3)
    xs_bf, bm, cm, sc_bf = even_prep(proj2, conv_w, conv_b, sc_conv_w, l, n_lat, d)
    xs_bf = xs_bf.reshape(b, l, d)
    bm = bm.reshape(b, l, gn)
    cm = cm.reshape(b, l, gn)
    bt = jnp.swapaxes(bm, 1, 2)
    yf = ssd_scan(xs_bf, bt, bm, cm, da, dat, n_lat, rev=False)
    post = (yf, proj2, jnp.repeat(d_skip, HEAD_DIM).reshape(1, d), norm_g.reshape(1, d))
    y_bf = ssd_scan(xs_bf, bt, bm, cm, da, dat, n_lat, rev=True, post=post)
    return matmul_residual([y_bf.reshape(b * l, d), sc_bf], w_out.astype(BF16), xa.reshape(b * l, d), mod16,
                           l, n_lat, 2).reshape(b, l, d)


def _odd_mixer(xa, mix_g, mod16, n_lat, w_in, w_out, na_qk_g, na_rel_bias, df_qk_g, df_lambda, df_subln_g,
               lambda_init, rope_tabs):
    b, l, d = xa.shape
    d_na = NA_HEADS * HEAD_DIM
    dq_w = DF_HEADS * 2 * HEAD_DIM
    n_ctx = l - n_lat
    reps = ODD_TN // HEAD_DIM
    ones = jnp.ones((ODD_TN,), F32)
    gains6 = jnp.stack([jnp.tile(na_qk_g[0], reps), jnp.tile(na_qk_g[1], reps), ones,
                        jnp.tile(df_qk_g[0], reps), jnp.tile(df_qk_g[1], reps), ones])[:, None, :]
    qkv = proj_odd(xa.reshape(b * l, d), mix_g, mod16, w_in.astype(BF16), gains6, *rope_tabs, l, n_lat, 0, 1)
    qkv = qkv.reshape(b, l, -1)
    lam_p = df_lambda.astype(F32)
    na_slabs = d_na // LANES
    df_slabs = dq_w // LANES
    na_lat = na_attention(qkv, na_bias_table(na_rel_bias), n_lat)
    na_ctx = flash_slabs(qkv, lam_p, "pair", n_ctx, n_ctx, n_ctx, n_ctx, n_lat // n_ctx, n_lat // n_ctx,
                         0, na_slabs, 2 * na_slabs, na_slabs)
    tk = next(t for t in (1408, 1280, 1024, 768, 512, 256) if l % t == 0)
    dq0 = 3 * na_slabs
    df_lat = flash_slabs(qkv, lam_p, "diff", 256, tk, n_lat, l, 0, 0, dq0, dq0 + df_slabs, dq0 + 2 * df_slabs,
                         df_slabs, lambda_init)
    df_ctx = flash_slabs(qkv, lam_p, "diff", n_ctx, n_ctx, n_ctx, n_ctx, n_lat // n_ctx, n_lat // n_ctx,
                         dq0, dq0 + df_slabs, dq0 + 2 * df_slabs, df_slabs, lambda_init)
    na_o = jnp.concatenate([na_lat, na_ctx], axis=1)
    df_o = jnp.concatenate([df_lat, df_ctx], axis=1).reshape(b, l, DF_HEADS, 2 * HEAD_DIM)
    df_o = (_rms(df_o, df_subln_g) * (1.0 - lambda_init)).reshape(b, l, dq_w)
    parts = [na_o.reshape(b * l, d_na).astype(BF16), df_o.reshape(b * l, dq_w).astype(BF16)]
    return matmul_residual(parts, w_out.astype(BF16), xa.reshape(b * l, d), mod16, l, n_lat, 2).reshape(b, l, d)


def kernel(x, c, ctx, c_ctx, ada_w, ada_b, mix_norm_g, ffn_norm_g, router_w, router_b, moe_w_gu, moe_b_gu,
           moe_w_down, moe_b_down, ev_w_in, ev_w_out, ssd_conv_w, ssd_conv_b, ssd_a_log, ssd_dt_bias, ssd_d,
           ssd_norm_g, sc_conv_w, od_w_in, od_w_out, na_qk_g, na_rel_bias, df_qk_g, df_lambda, df_subln_g):
    b, s, d = x.shape
    n_ctx = ctx.shape[1]
    l = s + n_ctx
    depth = ada_w.shape[0]
    de = moe_w_down.shape[2]
    ne = moe_w_gu.shape[1]
    xa = jnp.concatenate([x, ctx], axis=1)
    cos_l, sin_l = _axial_rope(s)
    cos_t = jnp.concatenate([cos_l, jnp.ones((n_ctx, HEAD_DIM), F32)], axis=0)
    sin_t = jnp.concatenate([sin_l, jnp.zeros((n_ctx, HEAD_DIM), F32)], axis=0)
    even_q = ((jnp.arange(HEAD_DIM) // (HEAD_DIM // 4)) % 2 == 0)[None, :]
    rope_tabs = tuple(jnp.tile(t, (1, LANES // HEAD_DIM))
                      for t in (cos_t, jnp.where(even_q, -sin_t, 0.0), jnp.where(even_q, 0.0, sin_t)))
    cond = jnp.concatenate([jax.nn.silu(c), jax.nn.silu(c_ctx)[None, :]], axis=0)
    cond_pad = jnp.zeros((16, d), F32).at[:b + 1].set(cond).astype(BF16)
    wgu_all = moe_w_gu.reshape(depth * ne, d, 2 * de)
    bgu_all = moe_b_gu.reshape(depth * ne, 2 * de // GU_BLK, LANES, 2).swapaxes(-1, -2).reshape(depth * ne, 1, 2 * de)
    wd_all = moe_w_down.reshape(depth * ne, de, d)
    bd_all = moe_b_down.reshape(depth * ne, 1, d)

    for i in range(depth):
        j = i // 2
        mod = matmul(cond_pad, ada_w[i].astype(BF16), 16, 6 * d // 4)[:b + 1] + ada_b[i]
        mod6 = mod.reshape(b + 1, 6, d)
        pad2 = jnp.zeros((b, CTX_ROW0 - 6, d), F32)
        mod16 = jnp.concatenate([mod6[:b], pad2, jnp.broadcast_to(mod6[b], (b, 6, d)), pad2], axis=1)
        if i % 2 == 0:
            xa = _even_mixer(xa, mix_norm_g[i], mod16, s, ev_w_in[j], ev_w_out[j], ssd_conv_w[j], ssd_conv_b[j],
                             ssd_a_log[j], ssd_dt_bias[j], ssd_d[j], ssd_norm_g[j], sc_conv_w[j])
        else:
            lambda_init = 0.8 - 0.6 * math.exp(-0.3 * i)
            xa = _odd_mixer(xa, mix_norm_g[i], mod16, s, od_w_in[j], od_w_out[j], na_qk_g[j], na_rel_bias[j],
                            df_qk_g[j], df_lambda[j], df_subln_g[j], lambda_init, rope_tabs)
        xa = moe_ffn(xa.reshape(b * l, d), ffn_norm_g[i], mod16, l, s, router_w[i], router_b[i],
                     wgu_all, bgu_all, wd_all, bd_all, i * ne).reshape(b, l, d)
    return xa[:, :s]
```

```python
import functools
import math

import jax
import jax.numpy as jnp
from jax import lax
from jax.experimental import pallas as pl
from jax.experimental.pallas import tpu as pltpu

F32 = jnp.float32
BF16 = jnp.bfloat16

GRID_W = 64
HEAD_DIM = 64
EPS = 1e-6
SSD_HEADS = 16
SSD_GROUPS = 4
SSD_STATE = 128
NA_ROWS = 8
NA_COLS = 16
NA_HEADS = 8
DF_HEADS = 4
N_EXPERTS = 32
TOP_K = 4
SWIGLU_LIMIT = 7.0
SWIGLU_ALPHA = 1.702
ROPE_BASE = 10000.0

LANES = 128
SSD_Q = 128
MOE_TM = 512
ROUTER_TM = 256
VMEM_LIMIT = 56 * 1024 * 1024
NEG = -1e30


def _cparams(sem):
    return pltpu.CompilerParams(dimension_semantics=sem, vmem_limit_bytes=VMEM_LIMIT)


def _mm_kernel(a_ref, w_ref, o_ref):
    o_ref[...] = jnp.dot(a_ref[...], w_ref[...], preferred_element_type=F32).astype(o_ref.dtype)


def matmul(a, w, tm, tn, out_dtype=F32):
    m, k = a.shape
    n = w.shape[1]
    return pl.pallas_call(
        _mm_kernel,
        grid=(n // tn, m // tm),
        in_specs=[pl.BlockSpec((tm, k), lambda j, i: (i, 0)),
                  pl.BlockSpec((k, tn), lambda j, i: (0, j))],
        out_specs=pl.BlockSpec((tm, tn), lambda j, i: (i, j)),
        out_shape=jax.ShapeDtypeStruct((m, n), out_dtype),
        compiler_params=_cparams(("parallel", "parallel")),
        name="matmul",
    )(a, w)


MOD_ROWS = 16
CTX_ROW0 = 8
ODD_TN = 512


def _row_is_ctx(i, tm, tiles_per_batch, n_lat):
    t = i % tiles_per_batch
    row = t * tm + lax.broadcasted_iota(jnp.int32, (tm, 1), 0)
    return row >= n_lat


def _mod_row(mod, is_ctx, which):
    return jnp.where(is_ctx, mod[CTX_ROW0 + which:CTX_ROW0 + which + 1], mod[which:which + 1])


def _modulated_rows(x_ref, g_ref, mod_ref, is_ctx, sh_row, sc_row):
    x = x_ref[...]
    y = x * lax.rsqrt(jnp.mean(x * x, axis=-1, keepdims=True) + EPS) * g_ref[...]
    mod = mod_ref[0]
    return y * (1.0 + _mod_row(mod, is_ctx, sc_row)) + _mod_row(mod, is_ctx, sh_row)


def _proj_kernel(x_ref, g_ref, mod_ref, w_ref, o_ref, h_ref, *, tiles_per_batch, n_lat, sh_row, sc_row):
    i = pl.program_id(0)

    @pl.when(pl.program_id(1) == 0)
    def _():
        is_ctx = _row_is_ctx(i, x_ref.shape[0], tiles_per_batch, n_lat)
        h_ref[...] = _modulated_rows(x_ref, g_ref, mod_ref, is_ctx, sh_row, sc_row).astype(BF16)

    o_ref[...] = jnp.dot(h_ref[...], w_ref[...], preferred_element_type=F32)


def _group_rms(a, gsum, gain):
    outs = []
    half = gsum.shape[0]
    for hf in range(a.shape[1] // half):
        ah = a[:, hf * half:(hf + 1) * half]
        ms = jnp.dot((ah * ah).astype(BF16), gsum, preferred_element_type=F32) * (1.0 / HEAD_DIM)
        outs.append(ah * lax.rsqrt(ms + EPS))
    return jnp.concatenate(outs, axis=1) * gain


def _proj_odd_kernel(x_ref, g_ref, mod_ref, w_ref, gain_ref, gsum_ref, cos_ref, sina_ref, sinb_ref, o_ref, h_ref, *,
                     tiles_per_batch, n_lat, sh_row, sc_row):
    i = pl.program_id(0)
    j = pl.program_id(1)

    @pl.when(j == 0)
    def _():
        is_ctx = _row_is_ctx(i, x_ref.shape[0], tiles_per_batch, n_lat)
        h_ref[...] = _modulated_rows(x_ref, g_ref, mod_ref, is_ctx, sh_row, sc_row).astype(BF16)

    acc = jnp.dot(h_ref[...], w_ref[...], preferred_element_type=F32)
    is_norm = (j == 0) | (j == 1) | (j == 3) | (j == 4)
    is_rope = (j == 3) | (j == 4)

    @pl.when(jnp.logical_not(is_norm))
    def _():
        o_ref[...] = acc.astype(BF16)

    @pl.when(is_norm & jnp.logical_not(is_rope))
    def _():
        o_ref[...] = _group_rms(acc, gsum_ref[...], gain_ref[0]).astype(BF16)

    @pl.when(is_rope)
    def _():
        xn = _group_rms(acc, gsum_ref[...], gain_ref[0])
        reps = xn.shape[1] // LANES
        cos = jnp.concatenate([cos_ref[...]] * reps, axis=1)
        sina = jnp.concatenate([sina_ref[...]] * reps, axis=1)
        sinb = jnp.concatenate([sinb_ref[...]] * reps, axis=1)
        quarter = HEAD_DIM // 4
        up = pltpu.roll(xn, xn.shape[1] - quarter, axis=1)
        dn = pltpu.roll(xn, quarter, axis=1)
        o_ref[...] = (xn * cos + up * sina + dn * sinb).astype(BF16)


def _pick_rows(l):
    for tm in (768, 512, 256):
        if l % tm == 0:
            return tm
    raise ValueError(l)


def proj_modulated(x2, gain, mod16, w_bf, l, n_lat, sh_row, sc_row, tn):
    m, d = x2.shape
    n = w_bf.shape[1]
    tm = _pick_rows(l)
    tpb = l // tm
    return pl.pallas_call(
        functools.partial(_proj_kernel, tiles_per_batch=tpb, n_lat=n_lat, sh_row=sh_row, sc_row=sc_row),
        grid=(m // tm, n // tn),
        in_specs=[pl.BlockSpec((tm, d), lambda i, j: (i, 0)),
                  pl.BlockSpec((1, d), lambda i, j: (0, 0)),
                  pl.BlockSpec((1, MOD_ROWS, d), lambda i, j: (i // tpb, 0, 0)),
                  pl.BlockSpec((d, tn), lambda i, j: (0, j))],
        out_specs=pl.BlockSpec((tm, tn), lambda i, j: (i, j)),
        out_shape=jax.ShapeDtypeStruct((m, n), F32),
        scratch_shapes=[pltpu.VMEM((tm, d), BF16)],
        compiler_params=_cparams(("parallel", "arbitrary")),
        name="proj_even",
    )(x2, gain.reshape(1, d), mod16, w_bf)


def proj_odd(x2, gain, mod16, w_bf, gains6, cos2, sina2, sinb2, l, n_lat, sh_row, sc_row):
    m, d = x2.shape
    n = w_bf.shape[1]
    tn = ODD_TN
    tm = _pick_rows(l)
    tpb = l // tm
    half = 256
    gi = jnp.arange(half) // HEAD_DIM
    gsum = (gi[:, None] == gi[None, :]).astype(BF16)
    tab = pl.BlockSpec((tm, LANES), lambda i, j: (i % tpb, 0))
    return pl.pallas_call(
        functools.partial(_proj_odd_kernel, tiles_per_batch=tpb, n_lat=n_lat, sh_row=sh_row, sc_row=sc_row),
        grid=(m // tm, n // tn),
        in_specs=[pl.BlockSpec((tm, d), lambda i, j: (i, 0)),
                  pl.BlockSpec((1, d), lambda i, j: (0, 0)),
                  pl.BlockSpec((1, MOD_ROWS, d), lambda i, j: (i // tpb, 0, 0)),
                  pl.BlockSpec((d, tn), lambda i, j: (0, j)),
                  pl.BlockSpec((1, 1, tn), lambda i, j: (j, 0, 0)),
                  pl.BlockSpec((half, half), lambda i, j: (0, 0)),
                  tab, tab, tab],
        out_specs=pl.BlockSpec((tm, tn), lambda i, j: (i, j)),
        out_shape=jax.ShapeDtypeStruct((m, n), BF16),
        scratch_shapes=[pltpu.VMEM((tm, d), BF16)],
        compiler_params=_cparams(("parallel", "arbitrary")),
        name="proj_odd",
    )(x2, gain.reshape(1, d), mod16, w_bf, gains6, gsum, cos2, sina2, sinb2)


def _mm_res_kernel(*refs, n_parts, tiles_per_batch, n_lat, gate_row):
    a_refs = refs[:n_parts]
    w_ref, r_ref, mod_ref, o_ref = refs[n_parts:]
    is_ctx = _row_is_ctx(pl.program_id(0), r_ref.shape[0], tiles_per_batch, n_lat)
    gate = _mod_row(mod_ref[0], is_ctx, gate_row)
    acc = None
    k0 = 0
    for a_ref in a_refs:
        k1 = k0 + a_ref.shape[1]
        t = jnp.dot(a_ref[...], w_ref[k0:k1, :], preferred_element_type=F32)
        acc = t if acc is None else acc + t
        k0 = k1
    o_ref[...] = r_ref[...] + gate * acc


def matmul_residual(parts, w_bf, res, mod16, l, n_lat, gate_row):
    m = res.shape[0]
    k, d = w_bf.shape
    tm = _pick_rows(l)
    tpb = l // tm
    return pl.pallas_call(
        functools.partial(_mm_res_kernel, n_parts=len(parts), tiles_per_batch=tpb, n_lat=n_lat, gate_row=gate_row),
        grid=(m // tm,),
        in_specs=[pl.BlockSpec((tm, a.shape[1]), lambda i: (i, 0)) for a in parts]
                 + [pl.BlockSpec((k, d), lambda i: (0, 0)),
                    pl.BlockSpec((tm, d), lambda i: (i, 0)),
                    pl.BlockSpec((1, MOD_ROWS, d), lambda i: (i // tpb, 0, 0))],
        out_specs=pl.BlockSpec((tm, d), lambda i: (i, 0)),
        out_shape=jax.ShapeDtypeStruct((m, d), F32),
        compiler_params=_cparams(("parallel",)),
        name="out_proj",
    )(*parts, w_bf, res, mod16)


EV_TM = 256
HALO = 8


def _even_prep_kernel(xa_ref, xa_p, xa_n, xb_ref, xb_p, xb_n, gb_ref, gc_ref, gc_p, gc_n, gh_ref, gh_p, gh_n,
                      cw_ref, cb_ref, sw_ref, xs_ref, bm_ref, cm_ref, sc_ref, scr, *, tiles_per_batch, n_lat, l):
    tm = xa_ref.shape[0]
    r0 = (pl.program_id(0) % tiles_per_batch) * tm
    has_prev = (r0 != 0) & (r0 != n_lat)
    has_next = (r0 + tm != n_lat) & (r0 + tm != l)

    def conv(main, prev, nxt, w):
        width = w.shape[0]
        scr[0:HALO] = jnp.where(has_prev, prev, 0.0)
        scr[HALO:HALO + tm] = main
        scr[HALO + tm:2 * HALO + tm] = jnp.where(has_next, nxt, 0.0)
        out = None
        for kk in range(width):
            o = HALO + kk - width // 2
            t = scr[o:o + tm] * w[kk:kk + 1]
            out = t if out is None else out + t
        return out

    d = xa_ref.shape[1]
    cw = cw_ref[...]
    cb = cb_ref[...]
    xs = conv(xa_ref[...], xa_p[...], xa_n[...], cw[:, :d]) + cb[:, :d]
    xs_ref[...] = (xs * jax.nn.sigmoid(xs)).astype(BF16)
    bc = conv(xb_ref[...], xb_p[...], xb_n[...], cw[:, d:]) + cb[:, d:]
    bc = (bc * jax.nn.sigmoid(bc)).astype(BF16)
    gn = bm_ref.shape[1]
    bm_ref[...] = bc[:, :gn]
    cm_ref[...] = bc[:, gn:]
    sc = gb_ref[...] * conv(gc_ref[...] * gh_ref[...], gc_p[...] * gh_p[...], gc_n[...] * gh_n[...], sw_ref[...])
    sc_ref[...] = sc.astype(BF16)


def even_prep(proj2, conv_w, conv_b, sc_conv_w, l, n_lat, d):
    m = proj2.shape[0]
    tm = EV_TM
    tpb = l // tm
    nhb = m // HALO
    gn = SSD_GROUPS * SSD_STATE

    def main(cblk):
        return pl.BlockSpec((tm, d), lambda i: (i, cblk))

    def prev(cblk):
        return pl.BlockSpec((HALO, d), lambda i: (jnp.maximum(i * (tm // HALO) - 1, 0), cblk))

    def nxt(cblk):
        return pl.BlockSpec((HALO, d), lambda i: (jnp.minimum((i + 1) * (tm // HALO), nhb - 1), cblk))

    full = lambda a: pl.BlockSpec(a.shape, lambda i: (0, 0))
    cb2 = conv_b.reshape(1, -1)
    return pl.pallas_call(
        functools.partial(_even_prep_kernel, tiles_per_batch=tpb, n_lat=n_lat, l=l),
        grid=(m // tm,),
        in_specs=[main(1), prev(1), nxt(1), main(2), prev(2), nxt(2), main(3),
                  main(4), prev(4), nxt(4), main(5), prev(5), nxt(5),
                  full(conv_w), full(cb2), full(sc_conv_w)],
        out_specs=[pl.BlockSpec((tm, d), lambda i: (i, 0)), pl.BlockSpec((tm, gn), lambda i: (i, 0)),
                   pl.BlockSpec((tm, gn), lambda i: (i, 0)), pl.BlockSpec((tm, d), lambda i: (i, 0))],
        out_shape=[jax.ShapeDtypeStruct((m, d), BF16), jax.ShapeDtypeStruct((m, gn), BF16),
                   jax.ShapeDtypeStruct((m, gn), BF16), jax.ShapeDtypeStruct((m, d), BF16)],
        scratch_shapes=[pltpu.VMEM((tm + 2 * HALO, d), F32)],
        compiler_params=_cparams(("parallel",)),
        name="even_prep",
    )(proj2, proj2, proj2, proj2, proj2, proj2, proj2, proj2, proj2, proj2, proj2, proj2, proj2,
      conv_w, cb2, sc_conv_w)


GU_BLK = 2 * LANES


def _gate_up_perm():
    src = jnp.arange(GU_BLK)
    dst = jnp.where(src % 2 == 0, src // 2, LANES + src // 2)
    return (dst[:, None] == jnp.arange(GU_BLK)[None, :]).astype(BF16)


def _deinterleave_into(w_ref, perm, o_ref):
    for blk in range(o_ref.shape[1] // GU_BLK):
        cs = slice(blk * GU_BLK, (blk + 1) * GU_BLK)
        o_ref[:, cs] = jnp.dot(w_ref[0, :, cs].astype(BF16), perm, preferred_element_type=F32).astype(BF16)


def _split3_dot(tri, a, dims):
    a1 = a.astype(BF16)
    r1 = a - a1.astype(F32)
    a2 = r1.astype(BF16)
    a3 = (r1 - a2.astype(F32)).astype(BF16)
    out = None
    for piece in (a1, a2, a3):
        if dims == "tri_a":
            t = jnp.dot(tri, piece, preferred_element_type=F32)
        else:
            t = jnp.dot(piece, tri, preferred_element_type=F32)
        out = t if out is None else out + t
    return out


def _ssd_kernel(*refs, rev, post):
    if post:
        x_ref, bt_ref, b_ref, c_ref, da_ref, dat_ref, yo_ref, z_ref, dsk_ref, ng_ref, y_ref, st_ref, yacc = refs
    else:
        x_ref, bt_ref, b_ref, c_ref, da_ref, dat_ref, y_ref, st_ref = refs
    q = SSD_Q
    step = pl.program_id(1)

    @pl.when(step == 0)
    def _():
        st_ref[...] = jnp.zeros_like(st_ref)

    da = da_ref[0, 0]
    dat = dat_ref[0, 0]
    a_c = da[:, SSD_HEADS:2 * SSD_HEADS]
    dt_r = dat[0:SSD_HEADS]
    a_r = dat[SSD_HEADS:2 * SSD_HEADS]
    ri = lax.broadcasted_iota(jnp.int32, (q, q), 0)
    ci = lax.broadcasted_iota(jnp.int32, (q, q), 1)
    tri = (ci <= ri).astype(BF16)
    tri_t = (ri <= ci).astype(BF16)
    cum_c = _split3_dot(tri, a_c, "tri_a")
    cum_r = _split3_dot(tri_t, a_r, "a_tri")
    tot_r = cum_r[:, q - 1:q]
    if rev:
        pos_c = cum_c - a_c
        pos_r = cum_r - a_r
        mask = ci >= ri
    else:
        pos_c = cum_c
        pos_r = cum_r
        mask = ri >= ci
    lane = lax.broadcasted_iota(jnp.int32, (q, LANES), 1)
    lane_n = lax.broadcasted_iota(jnp.int32, (SSD_STATE, LANES), 1)
    heads_per_group = SSD_HEADS // SSD_GROUPS

    for g in range(SSD_GROUPS):
        gs = slice(g * SSD_STATE, (g + 1) * SSD_STATE)
        bg = b_ref[0, :, gs]
        cg = c_ref[0, :, gs]
        btg = bt_ref[0, gs, :].astype(F32)
        cb = lax.dot_general(cg, bg, (((1,), (1,)), ((), ())), preferred_element_type=F32)
        cg32 = cg.astype(F32)
        for pp in range(heads_per_group // 2):
            p = g * (heads_per_group // 2) + pp
            xp = x_ref[0, :, p * LANES:(p + 1) * LANES]
            st = st_ref[p]
            rhs = jnp.concatenate([xp, st.astype(BF16)], axis=0)
            ys, ds, decs = [], [], []
            for hh in range(2):
                h = 2 * p + hh
                colb = jnp.broadcast_to(pos_c[:, h:h + 1], (q, LANES))
                row = pos_r[h:h + 1, :]
                dt_row = dt_r[h:h + 1, :]
                tot = tot_r[h:h + 1, :]
                if rev:
                    seg = row - colb
                    coff = jnp.exp(tot - colb)
                    w_row = dt_row * jnp.exp(row)
                else:
                    seg = colb - row
                    coff = jnp.exp(colb)
                    w_row = dt_row * jnp.exp(tot - row)
                decay = jnp.exp(jnp.where(mask, seg, NEG))
                m_h = (cb * decay * dt_row).astype(BF16)
                c_h = (cg32 * coff).astype(BF16)
                lhs = jnp.concatenate([m_h, c_h], axis=1)
                ys.append(jnp.dot(lhs, rhs, preferred_element_type=F32))
                btw = (btg * w_row).astype(BF16)
                ds.append(jnp.dot(btw, xp, preferred_element_type=F32))
                decs.append(jnp.exp(tot))
            y_pair = jnp.where(lane < HEAD_DIM, ys[0], ys[1])
            if post:
                yacc[:, p * LANES:(p + 1) * LANES] = y_pair
            else:
                y_ref[0, :, p * LANES:(p + 1) * LANES] = y_pair
            st_ref[p] = jnp.where(lane_n < HEAD_DIM, decs[0] * st + ds[0], decs[1] * st + ds[1])

    if post:
        z = z_ref[...]
        y = (yacc[...] + yo_ref[0] + dsk_ref[...] * x_ref[0].astype(F32)) * (z * jax.nn.sigmoid(z))
        y = y * lax.rsqrt(jnp.mean(y * y, axis=-1, keepdims=True) + EPS) * ng_ref[...]
        y_ref[0] = y.astype(y_ref.dtype)


def ssd_scan(xs, bt, bm, cm, da, dat, n_lat, rev, post=None):
    b, l, d = xs.shape
    nch = l // SSD_Q
    nlat = n_lat // SSD_Q
    nctx = nch - nlat
    d_idx = 1 if rev else 0
    if rev:
        def chunk(j):
            return nch - 1 - j
    else:
        def chunk(j):
            return jnp.where(j < nctx, nlat + j, j - nctx)
    gn = SSD_GROUPS * SSD_STATE
    tok = pl.BlockSpec((1, SSD_Q, d), lambda i, j: (i, chunk(j), 0))
    in_specs = [tok,
                pl.BlockSpec((1, gn, SSD_Q), lambda i, j: (i, 0, chunk(j))),
                pl.BlockSpec((1, SSD_Q, gn), lambda i, j: (i, chunk(j), 0)),
                pl.BlockSpec((1, SSD_Q, gn), lambda i, j: (i, chunk(j), 0)),
                pl.BlockSpec((1, 1, SSD_Q, 2 * SSD_HEADS), lambda i, j: (i, d_idx, chunk(j), 0)),
                pl.BlockSpec((1, 1, 2 * SSD_HEADS, SSD_Q), lambda i, j: (i, d_idx, 0, chunk(j)))]
    scratch = [pltpu.VMEM((SSD_HEADS // 2, SSD_STATE, LANES), F32)]
    args = (xs, bt, bm, cm, da, dat)
    if post is not None:
        row = pl.BlockSpec((1, d), lambda i, j: (0, 0))
        in_specs += [tok, pl.BlockSpec((SSD_Q, d), lambda i, j: (i * nch + chunk(j), 0)), row, row]
        scratch.append(pltpu.VMEM((SSD_Q, d), F32))
        args += tuple(post)
    return pl.pallas_call(
        functools.partial(_ssd_kernel, rev=rev, post=post is not None),
        grid=(b, nch),
        in_specs=in_specs,
        out_specs=tok,
        out_shape=jax.ShapeDtypeStruct((b, l, d), F32 if post is None else BF16),
        scratch_shapes=scratch,
        compiler_params=_cparams(("parallel", "arbitrary")),
        name="ssd_bwd" if rev else "ssd_fwd",
    )(*args)


def _stack_halves(qv):
    lane = lax.broadcasted_iota(jnp.int32, qv.shape, 1)
    zero = jnp.zeros_like(qv)
    q1 = jnp.where(lane < HEAD_DIM, qv, zero)
    q2 = jnp.where(lane >= HEAD_DIM, qv, zero)
    return jnp.concatenate([q1, q2], axis=0) * jnp.asarray(HEAD_DIM ** -0.5, qv.dtype)


def _flash_kernel(lam_ref, q_ref, k_ref, v_ref, o_ref, *, mode, tk, lambda_init):
    tq = q_ref.shape[1]
    nk = k_ref.shape[1] // tk
    nt = tk // LANES
    qs = _stack_halves(q_ref[0])

    def scores(u):
        return lax.dot_general(qs, k_ref[0, u * tk:(u + 1) * tk, :], (((1,), (1,)), ((), ())),
                               preferred_element_type=F32)

    m_old = jnp.full((2 * tq, LANES), -jnp.inf, F32)
    l_run = jnp.zeros((2 * tq, LANES), F32)
    acc = jnp.zeros((2 * tq, LANES), F32)
    s_next = scores(0)
    for u in range(nk):
        s = s_next
        if u + 1 < nk:
            s_next = scores(u + 1)
        tiles = [s[:, t * LANES:(t + 1) * LANES] for t in range(nt)]
        smax = tiles[0]
        for t in tiles[1:]:
            smax = jnp.maximum(smax, t)
        m_new = jnp.maximum(m_old, jnp.max(smax, axis=-1, keepdims=True))
        alpha = jnp.exp(m_old - m_new)
        ps = [jnp.exp(t - m_new) for t in tiles]
        psum = ps[0]
        for t in ps[1:]:
            psum = psum + t
        l_run = alpha * l_run + jnp.sum(psum, axis=-1, keepdims=True)
        p = jnp.concatenate([t.astype(BF16) for t in ps], axis=1)
        acc = alpha * acc + jnp.dot(p, v_ref[0, u * tk:(u + 1) * tk, :], preferred_element_type=F32)
        m_old = m_new
    o = acc / l_run
    o1, o2 = o[:tq], o[tq:]
    if mode == "diff":
        lp = lam_ref[...]
        s01 = jnp.sum(lp[0:1] * lp[1:2], axis=-1, keepdims=True)
        s23 = jnp.sum(lp[2:3] * lp[3:4], axis=-1, keepdims=True)
        lam = jnp.exp(s01) - jnp.exp(s23) + lambda_init
        out = o1 - lam * o2
    else:
        lane = lax.broadcasted_iota(jnp.int32, o1.shape, 1)
        out = jnp.where(lane < HEAD_DIM, o1, o2)
    o_ref[0] = out.astype(o_ref.dtype)


def flash_slabs(qkv, lam_p, mode, tq, tk, lq, lk, q_blk0, k_blk, q_slab0, k_slab0, v_slab0, n_slabs,
                lambda_init=0.0):
    b = qkv.shape[0]
    return pl.pallas_call(
        functools.partial(_flash_kernel, mode=mode, tk=tk, lambda_init=lambda_init),
        grid=(b, n_slabs, lq // tq),
        in_specs=[pl.BlockSpec(lam_p.shape, lambda i, h, j: (0, 0)),
                  pl.BlockSpec((1, tq, LANES), lambda i, h, j: (i, q_blk0 + j, q_slab0 + h)),
                  pl.BlockSpec((1, lk, LANES), lambda i, h, j: (i, k_blk, k_slab0 + h)),
                  pl.BlockSpec((1, lk, LANES), lambda i, h, j: (i, k_blk, v_slab0 + h))],
        out_specs=pl.BlockSpec((1, tq, LANES), lambda i, h, j: (i, j, h)),
        out_shape=jax.ShapeDtypeStruct((b, lq, n_slabs * LANES), F32),
        compiler_params=_cparams(("parallel", "parallel", "arbitrary")),
        name="flash_" + mode,
    )(lam_p, qkv, qkv, qkv)


NA_RB = 8
NA_BLK = NA_RB * GRID_W
NA_WIN = NA_ROWS * GRID_W


def _na_kernel(q_ref, kp_ref, kc_ref, kn_ref, vp_ref, vc_ref, vn_ref, kx_ref, vx_ref, bias_ref, o_ref,
               kbuf, vbuf, *, rows):
    rb = pl.program_id(1)
    kbuf[0:NA_BLK] = kp_ref[0]
    kbuf[NA_BLK:2 * NA_BLK] = kc_ref[0]
    kbuf[2 * NA_BLK:3 * NA_BLK] = kn_ref[0]
    vbuf[0:NA_BLK] = vp_ref[0]
    vbuf[NA_BLK:2 * NA_BLK] = vc_ref[0]
    vbuf[2 * NA_BLK:3 * NA_BLK] = vn_ref[0]
    npairs = q_ref.shape[2] // LANES
    lane = lax.broadcasted_iota(jnp.int32, (GRID_W, LANES), 1)

    def row_body(rl, carry):
        r = rb * NA_RB + rl
        r_start = jnp.clip(r - NA_ROWS // 2, 0, rows - NA_ROWS)
        off = r_start - (rb * NA_RB - NA_RB)
        di0 = r_start - r + NA_ROWS - 1
        tok0 = pl.multiple_of(off * GRID_W, GRID_W)
        q0 = pl.multiple_of(rl * GRID_W, GRID_W)
        for p in range(npairs):
            ls = slice(p * LANES, (p + 1) * LANES)
            qs = _stack_halves(q_ref[0, pl.ds(q0, GRID_W), ls])
            kw = kbuf[pl.ds(tok0, NA_WIN), ls]
            vw = vbuf[pl.ds(tok0, NA_WIN), ls]
            s_nb = lax.dot_general(qs, kw, (((1,), (1,)), ((), ())), preferred_element_type=F32)
            bias = jnp.concatenate([bias_ref[p, di0 + 2 * j] for j in range(NA_ROWS // 2)], axis=1)
            s_nb = s_nb + bias
            s_cx = lax.dot_general(qs, kx_ref[0, :, ls], (((1,), (1,)), ((), ())), preferred_element_type=F32)
            m = jnp.maximum(jnp.max(s_nb, axis=-1, keepdims=True), jnp.max(s_cx, axis=-1, keepdims=True))
            p_nb = jnp.exp(s_nb - m)
            p_cx = jnp.exp(s_cx - m)
            l = jnp.sum(p_nb, axis=-1, keepdims=True) + jnp.sum(p_cx, axis=-1, keepdims=True)
            o = (jnp.dot(p_nb.astype(BF16), vw, preferred_element_type=F32)
                 + jnp.dot(p_cx.astype(BF16), vx_ref[0, :, ls], preferred_element_type=F32)) / l
            o_ref[0, pl.ds(q0, GRID_W), ls] = jnp.where(lane < HEAD_DIM, o[:GRID_W], o[GRID_W:])
        return carry

    lax.fori_loop(0, NA_RB, row_body, 0, unroll=True)


def na_bias_table(rel_bias):
    cols = jnp.arange(GRID_W)
    c_start = jnp.clip(cols - NA_COLS // 2, 0, GRID_W - NA_COLS)
    kc = jnp.arange(GRID_W)
    valid = (kc[None, :] >= c_start[:, None]) & (kc[None, :] < c_start[:, None] + NA_COLS)
    idx = jnp.clip(kc[None, :] - cols[:, None] + NA_COLS - 1, 0, 2 * NA_COLS - 2)
    t = jnp.where(valid[None, None], rel_bias[:, :, idx], NEG)
    t2 = jnp.concatenate([t[:, :-1], t[:, 1:]], axis=-1)
    nh, nd = t2.shape[0], t2.shape[1]
    t2 = t2.reshape(nh // 2, 2, nd, GRID_W, LANES).transpose(0, 2, 1, 3, 4)
    return t2.reshape(nh // 2, nd, 2 * GRID_W, LANES).astype(F32)


def na_attention(qkv, bias_tab, n_lat):
    b, l, _ = qkv.shape
    w = NA_HEADS * HEAD_DIM
    rows = n_lat // GRID_W
    nrb = rows // NA_RB
    n_ctx = l - n_lat
    ctx_blk = n_lat // n_ctx

    def blk(step, stream):
        if step < 0:
            return pl.BlockSpec((1, NA_BLK, w), lambda i, j: (i, jnp.maximum(j - 1, 0), stream))
        if step > 0:
            return pl.BlockSpec((1, NA_BLK, w), lambda i, j: (i, jnp.minimum(j + 1, nrb - 1), stream))
        return pl.BlockSpec((1, NA_BLK, w), lambda i, j: (i, j, stream))

    def ctx(stream):
        return pl.BlockSpec((1, n_ctx, w), lambda i, j: (i, ctx_blk, stream))

    return pl.pallas_call(
        functools.partial(_na_kernel, rows=rows),
        grid=(b, nrb),
        in_specs=[blk(0, 0), blk(-1, 1), blk(0, 1), blk(1, 1), blk(-1, 2), blk(0, 2), blk(1, 2), ctx(1), ctx(2),
                  pl.BlockSpec(bias_tab.shape, lambda i, j: (0, 0, 0, 0))],
        out_specs=pl.BlockSpec((1, NA_BLK, w), lambda i, j: (i, j, 0)),
        out_shape=jax.ShapeDtypeStruct((b, n_lat, w), F32),
        scratch_shapes=[pltpu.VMEM((3 * NA_BLK, w), BF16), pltpu.VMEM((3 * NA_BLK, w), BF16)],
        compiler_params=_cparams(("parallel", "parallel")),
        name="na_attention",
    )(qkv, qkv, qkv, qkv, qkv, qkv, qkv, qkv, qkv, bias_tab)


def _router_kernel(x_ref, g_ref, sh_ref, sc_ref, w_ref, b_ref, h_ref, ri_ref, rg_ref, cnt_ref, carry_ref):
    step = pl.program_id(0)

    @pl.when(step == 0)
    def _():
        carry_ref[...] = jnp.zeros_like(carry_ref)

    tm = x_ref.shape[0]
    x = x_ref[...]
    y = x * lax.rsqrt(jnp.mean(x * x, axis=-1, keepdims=True) + EPS) * g_ref[...]
    h = y * (1.0 + sc_ref[0]) + sh_ref[0]
    h_ref[...] = h.astype(BF16)
    w = w_ref[...]
    h1 = h.astype(BF16)
    h2 = (h - h1.astype(F32)).astype(BF16)
    w1 = w.astype(BF16)
    w2 = (w - w1.astype(F32)).astype(BF16)
    logits = (jnp.dot(h1, w1, preferred_element_type=F32) + jnp.dot(h2, w1, preferred_element_type=F32)
              + jnp.dot(h1, w2, preferred_element_type=F32)) + b_ref[...]
    lane = lax.broadcasted_iota(jnp.int32, (tm, LANES), 1)
    work = logits
    tops, idxs, hots = [], [], []
    for _ in range(TOP_K):
        mx = jnp.max(work, axis=-1, keepdims=True)
        ix = jnp.min(jnp.where(work == mx, lane, LANES), axis=-1, keepdims=True)
        hot = lane == ix
        work = jnp.where(hot, -jnp.inf, work)
        tops.append(mx)
        idxs.append(ix)
        hots.append(hot)
    es = [jnp.exp(t - tops[0]) for t in tops]
    den = es[0] + es[1] + es[2] + es[3]
    multi = jnp.zeros((tm, LANES), F32)
    for hot in hots:
        multi = multi + hot.astype(F32)
    ri = lax.broadcasted_iota(jnp.int32, (tm, tm), 0)
    ci = lax.broadcasted_iota(jnp.int32, (tm, tm), 1)
    tri = (ci < ri).astype(BF16)
    cum = jnp.dot(tri, multi.astype(BF16), preferred_element_type=F32) + carry_ref[...]
    carry_new = carry_ref[...] + jnp.sum(multi, axis=0, keepdims=True)
    carry_ref[...] = carry_new
    cnt_ref[...] = carry_new
    out_i = jnp.zeros((tm, LANES), jnp.int32)
    out_g = jnp.zeros((tm, LANES), F32)
    for kk in range(TOP_K):
        rank = jnp.sum(jnp.where(hots[kk], cum, 0.0), axis=-1, keepdims=True).astype(jnp.int32)
        out_i = jnp.where(lane == kk, idxs[kk], out_i)
        out_i = jnp.where(lane == TOP_K + kk, rank, out_i)
        out_g = jnp.where(lane == kk, es[kk] / den, out_g)
    ri_ref[...] = out_i
    rg_ref[...] = out_g


def moe_router(x2, gain, modarr, sh_row, sc_row, n_lat_tiles, tiles_per_batch, router_w, router_b):
    n, d = x2.shape
    tm = ROUTER_TM
    w_pad = jnp.zeros((d, LANES), F32).at[:, :N_EXPERTS].set(router_w)
    b_pad = jnp.full((1, LANES), NEG, F32).at[0, :N_EXPERTS].set(router_b)

    def mod_idx(which):
        def f(i):
            bidx = i // tiles_per_batch
            is_ctx = (i % tiles_per_batch) >= n_lat_tiles
            return (bidx * MOD_ROWS + is_ctx.astype(jnp.int32) * CTX_ROW0 + which, 0, 0)
        return f

    return pl.pallas_call(
        _router_kernel,
        grid=(n // tm,),
        in_specs=[pl.BlockSpec((tm, d), lambda i: (i, 0)),
                  pl.BlockSpec((1, d), lambda i: (0, 0)),
                  pl.BlockSpec((1, 1, d), mod_idx(sh_row)),
                  pl.BlockSpec((1, 1, d), mod_idx(sc_row)),
                  pl.BlockSpec((d, LANES), lambda i: (0, 0)),
                  pl.BlockSpec((1, LANES), lambda i: (0, 0))],
        out_specs=[pl.BlockSpec((tm, d), lambda i: (i, 0)),
                   pl.BlockSpec((tm, LANES), lambda i: (i, 0)),
                   pl.BlockSpec((tm, LANES), lambda i: (i, 0)),
                   pl.BlockSpec((1, LANES), lambda i: (0, 0))],
        out_shape=[jax.ShapeDtypeStruct((n, d), BF16),
                   jax.ShapeDtypeStruct((n, LANES), jnp.int32),
                   jax.ShapeDtypeStruct((n, LANES), F32),
                   jax.ShapeDtypeStruct((1, LANES), F32)],
        scratch_shapes=[pltpu.VMEM((1, LANES), F32)],
        compiler_params=_cparams(("arbitrary",)),
        name="moe_router",
    )(x2, gain.reshape(1, d), modarr, modarr, w_pad, b_pad)


def _expert_kernel(be_ref, nu_ref, x_ref, wgu_ref, bgu_ref, wd_ref, bd_ref, perm_ref, o_ref, wgu_bf, wd_bf):
    j = pl.program_id(0)
    used = j < nu_ref[0]
    changed = (j == 0) | (be_ref[j] != be_ref[jnp.maximum(j - 1, 0)])

    @pl.when(used & changed)
    def _():
        wd_bf[...] = wd_ref[0].astype(BF16)
        _deinterleave_into(wgu_ref, perm_ref[...], wgu_bf)

    @pl.when(used)
    def _():
        gu = jnp.dot(x_ref[...], wgu_bf[...], preferred_element_type=F32) + bgu_ref[0]
        acts = []
        for blk in range(gu.shape[1] // GU_BLK):
            gate = jnp.minimum(gu[:, blk * GU_BLK:blk * GU_BLK + LANES], SWIGLU_LIMIT)
            up = jnp.clip(gu[:, blk * GU_BLK + LANES:(blk + 1) * GU_BLK], -SWIGLU_LIMIT, SWIGLU_LIMIT)
            acts.append((gate * jax.nn.sigmoid(SWIGLU_ALPHA * gate) * (up + 1.0)).astype(BF16))
        act = jnp.concatenate(acts, axis=1)
        o_ref[...] = (jnp.dot(act, wd_bf[...], preferred_element_type=F32) + bd_ref[0]).astype(o_ref.dtype)

    @pl.when(jnp.logical_not(used))
    def _():
        o_ref[...] = jnp.zeros_like(o_ref)


def moe_experts(xs, block_expert, n_used, wgu, bgu, wd, bd, e_off):
    cap, d = xs.shape
    de = wd.shape[1]
    nblk = cap // MOE_TM
    wmap = lambda j, be, nu: (e_off + be[j], 0, 0)
    grid_spec = pltpu.PrefetchScalarGridSpec(
        num_scalar_prefetch=2,
        grid=(nblk,),
        in_specs=[pl.BlockSpec((MOE_TM, d), lambda j, be, nu: (j, 0)),
                  pl.BlockSpec((1, d, 2 * de), wmap),
                  pl.BlockSpec((1, 1, 2 * de), wmap),
                  pl.BlockSpec((1, de, d), wmap),
                  pl.BlockSpec((1, 1, d), wmap),
                  pl.BlockSpec((GU_BLK, GU_BLK), lambda j, be, nu: (0, 0))],
        out_specs=pl.BlockSpec((MOE_TM, d), lambda j, be, nu: (j, 0)),
        scratch_shapes=[pltpu.VMEM((d, 2 * de), BF16), pltpu.VMEM((de, d), BF16)],
    )
    return pl.pallas_call(
        _expert_kernel,
        grid_spec=grid_spec,
        out_shape=jax.ShapeDtypeStruct((cap, d), BF16),
        compiler_params=_cparams(("arbitrary",)),
        name="moe_experts",
    )(block_expert, n_used, xs, wgu, bgu, wd, bd, _gate_up_perm())


def moe_ffn(x2, gain, mod16, l, n_lat, router_w, router_b, wgu, bgu, wd, bd, e_off):
    n, d = x2.shape
    sh_row, sc_row, gate_row = 3, 4, 5
    h_bf, r_i, r_g, cnt = moe_router(x2, gain, mod16.reshape(-1, 1, d), sh_row, sc_row, n_lat // ROUTER_TM,
                                     l // ROUTER_TM, router_w, router_b)
    top_idx = r_i[:, :TOP_K]
    rank = r_i[:, TOP_K:2 * TOP_K]
    gates = r_g[:, :TOP_K]
    counts = cnt[0, :N_EXPERTS].astype(jnp.int32)
    padded = (counts + MOE_TM - 1) // MOE_TM * MOE_TM
    pad_end = jnp.cumsum(padded)
    pad_start = pad_end - padded
    dest = (pad_start[top_idx] + rank).T
    nblk = -(-(n * TOP_K + N_EXPERTS * (MOE_TM - 1)) // MOE_TM)
    cap = nblk * MOE_TM
    tok = jnp.broadcast_to(jnp.arange(n, dtype=jnp.int32)[None, :], (TOP_K, n))
    slot_tok = jnp.zeros((cap,), jnp.int32).at[dest.reshape(-1)].set(tok.reshape(-1), unique_indices=True)
    blk_start = jnp.arange(nblk, dtype=jnp.int32)[:, None] * MOE_TM
    block_expert = jnp.minimum(jnp.sum((pad_end[None, :] <= blk_start).astype(jnp.int32), axis=1), N_EXPERTS - 1)
    n_used = (pad_end[-1:] // MOE_TM).astype(jnp.int32)
    xs = h_bf[slot_tok]
    y = moe_experts(xs, block_expert, n_used, wgu, bgu, wd, bd, e_off)
    yk = y[dest.reshape(-1)].reshape(TOP_K, n, d)
    return moe_combine(yk, r_g, x2, mod16, l, n_lat, gate_row)


def _combine_kernel(y_ref, g_ref, r_ref, mod_ref, o_ref, *, tiles_per_batch, n_lat, gate_row):
    is_ctx = _row_is_ctx(pl.program_id(0), r_ref.shape[0], tiles_per_batch, n_lat)
    gate = _mod_row(mod_ref[0], is_ctx, gate_row)
    g = g_ref[...]
    acc = y_ref[0].astype(F32) * g[:, 0:1]
    for kk in range(1, TOP_K):
        acc = acc + y_ref[kk].astype(F32) * g[:, kk:kk + 1]
    o_ref[...] = r_ref[...] + gate * acc


def moe_combine(yk, r_g, res, mod16, l, n_lat, gate_row):
    _, n, d = yk.shape
    tm = _pick_rows(l)
    tpb = l // tm
    return pl.pallas_call(
        functools.partial(_combine_kernel, tiles_per_batch=tpb, n_lat=n_lat, gate_row=gate_row),
        grid=(n // tm,),
        in_specs=[pl.BlockSpec((TOP_K, tm, d), lambda i: (0, i, 0)),
                  pl.BlockSpec((tm, LANES), lambda i: (i, 0)),
                  pl.BlockSpec((tm, d), lambda i: (i, 0)),
                  pl.BlockSpec((1, MOD_ROWS, d), lambda i: (i // tpb, 0, 0))],
        out_specs=pl.BlockSpec((tm, d), lambda i: (i, 0)),
        out_shape=jax.ShapeDtypeStruct((n, d), F32),
        compiler_params=_cparams(("parallel",)),
        name="moe_combine",
    )(yk, r_g, res, mod16)


def _rms(x, g):
    return x * lax.rsqrt(jnp.mean(x * x, axis=-1, keepdims=True) + EPS) * g


def _axial_rope(n_tok):
    pos = jnp.arange(n_tok)
    rows = (pos // GRID_W).astype(F32)
    cols = (pos % GRID_W).astype(F32)
    quarter = HEAD_DIM // 4
    inv_freq = ROPE_BASE ** (-jnp.arange(quarter, dtype=F32) / quarter)
    ar = rows[:, None] * inv_freq
    ac = cols[:, None] * inv_freq
    ang = jnp.concatenate([ar, ar, ac, ac], axis=-1)
    return jnp.cos(ang), jnp.sin(ang)


def _pick_tn(n):
    for tn in (1536, 1280, 1024, 768, 512, 256, 128):
        if n % tn == 0:
            return tn
    raise ValueError(n)


def _even_mixer(xa, mix_g, mod16, n_lat, w_in, w_out, conv_w, conv_b, a_log, dt_bias, d_skip, norm_g, sc_conv_w):
    b, l, d = xa.shape
    gn = SSD_GROUPS * SSD_STATE
    conv_dim = d + 2 * gn
    n_in = w_in.shape[1]
    n_pad = -(-n_in // 256) * 256
    o_dt = d + conv_dim
    w_perm = jnp.concatenate([w_in[:, :o_dt], w_in[:, o_dt + 2 * SSD_HEADS:], w_in[:, o_dt:o_dt + 2 * SSD_HEADS],
                              jnp.zeros((d, n_pad - n_in), w_in.dtype)], axis=1).astype(BF16)
    proj2 = proj_modulated(xa.reshape(b * l, d), mix_g, mod16, w_perm, l, n_lat, 0, 1, _pick_tn(n_pad))
    dtr = proj2[:, n_in - 2 * SSD_HEADS:n_in]
    dt = jax.nn.softplus(dtr.reshape(b, l, 2, SSD_HEADS) + dt_bias)
    a_neg = -jnp.exp(a_log)
    da = jnp.concatenate([dt, dt * a_neg], axis=-1)
    da = jnp.moveaxis(da, 2, 1)
    dat = jnp.swapaxes(da, 2, 3)
    xs_bf, bm, cm, sc_bf = even_prep(proj2, conv_w, conv_b, sc_conv_w, l, n_lat, d)
    xs_bf = xs_bf.reshape(b, l, d)
    bm = bm.reshape(b, l, gn)
    cm = cm.reshape(b, l, gn)
    bt = jnp.swapaxes(bm, 1, 2)
    yf = ssd_scan(xs_bf, bt, bm, cm, da, dat, n_lat, rev=False)
    post = (yf, proj2, jnp.repeat(d_skip, HEAD_DIM).reshape(1, d), norm_g.reshape(1, d))
    y_bf = ssd_scan(xs_bf, bt, bm, cm, da, dat, n_lat, rev=True, post=post)
    return matmul_residual([y_bf.reshape(b * l, d), sc_bf], w_out.astype(BF16), xa.reshape(b * l, d), mod16,
                           l, n_lat, 2).reshape(b, l, d)


def _odd_mixer(xa, mix_g, mod16, n_lat, w_in, w_out, na_qk_g, na_rel_bias, df_qk_g, df_lambda, df_subln_g,
               lambda_init, rope_tabs):
    b, l, d = xa.shape
    d_na = NA_HEADS * HEAD_DIM
    dq_w = DF_HEADS * 2 * HEAD_DIM
    n_ctx = l - n_lat
    reps = ODD_TN // HEAD_DIM
    ones = jnp.ones((ODD_TN,), F32)
    gains6 = jnp.stack([jnp.tile(na_qk_g[0], reps), jnp.tile(na_qk_g[1], reps), ones,
                        jnp.tile(df_qk_g[0], reps), jnp.tile(df_qk_g[1], reps), ones])[:, None, :]
    qkv = proj_odd(xa.reshape(b * l, d), mix_g, mod16, w_in.astype(BF16), gains6, *rope_tabs, l, n_lat, 0, 1)
    qkv = qkv.reshape(b, l, -1)
    lam_p = df_lambda.astype(F32)
    na_slabs = d_na // LANES
    df_slabs = dq_w // LANES
    na_lat = na_attention(qkv, na_bias_table(na_rel_bias), n_lat)
    na_ctx = flash_slabs(qkv, lam_p, "pair", n_ctx, n_ctx, n_ctx, n_ctx, n_lat // n_ctx, n_lat // n_ctx,
                         0, na_slabs, 2 * na_slabs, na_slabs)
    tk = next(t for t in (1408, 1280, 1024, 768, 512, 256) if l % t == 0)
    dq0 = 3 * na_slabs
    df_lat = flash_slabs(qkv, lam_p, "diff", 256, tk, n_lat, l, 0, 0, dq0, dq0 + df_slabs, dq0 + 2 * df_slabs,
                         df_slabs, lambda_init)
    df_ctx = flash_slabs(qkv, lam_p, "diff", n_ctx, n_ctx, n_ctx, n_ctx, n_lat // n_ctx, n_lat // n_ctx,
                         dq0, dq0 + df_slabs, dq0 + 2 * df_slabs, df_slabs, lambda_init)
    na_o = jnp.concatenate([na_lat, na_ctx], axis=1)
    df_o = jnp.concatenate([df_lat, df_ctx], axis=1).reshape(b, l, DF_HEADS, 2 * HEAD_DIM)
    df_o = (_rms(df_o, df_subln_g) * (1.0 - lambda_init)).reshape(b, l, dq_w)
    parts = [na_o.reshape(b * l, d_na).astype(BF16), df_o.reshape(b * l, dq_w).astype(BF16)]
    return matmul_residual(parts, w_out.astype(BF16), xa.reshape(b * l, d), mod16, l, n_lat, 2).reshape(b, l, d)


def kernel(x, c, ctx, c_ctx, ada_w, ada_b, mix_norm_g, ffn_norm_g, router_w, router_b, moe_w_gu, moe_b_gu,
           moe_w_down, moe_b_down, ev_w_in, ev_w_out, ssd_conv_w, ssd_conv_b, ssd_a_log, ssd_dt_bias, ssd_d,
           ssd_norm_g, sc_conv_w, od_w_in, od_w_out, na_qk_g, na_rel_bias, df_qk_g, df_lambda, df_subln_g):
    b, s, d = x.shape
    n_ctx = ctx.shape[1]
    l = s + n_ctx
    depth = ada_w.shape[0]
    de = moe_w_down.shape[2]
    ne = moe_w_gu.shape[1]
    xa = jnp.concatenate([x, ctx], axis=1)
    cos_l, sin_l = _axial_rope(s)
    cos_t = jnp.concatenate([cos_l, jnp.ones((n_ctx, HEAD_DIM), F32)], axis=0)
    sin_t = jnp.concatenate([sin_l, jnp.zeros((n_ctx, HEAD_DIM), F32)], axis=0)
    even_q = ((jnp.arange(HEAD_DIM) // (HEAD_DIM // 4)) % 2 == 0)[None, :]
    rope_tabs = tuple(jnp.tile(t, (1, LANES // HEAD_DIM))
                      for t in (cos_t, jnp.where(even_q, -sin_t, 0.0), jnp.where(even_q, 0.0, sin_t)))
    cond = jnp.concatenate([jax.nn.silu(c), jax.nn.silu(c_ctx)[None, :]], axis=0)
    cond_pad = jnp.zeros((16, d), F32).at[:b + 1].set(cond).astype(BF16)
    wgu_all = moe_w_gu.reshape(depth * ne, d, 2 * de)
    bgu_all = moe_b_gu.reshape(depth * ne, 2 * de // GU_BLK, LANES, 2).swapaxes(-1, -2).reshape(depth * ne, 1, 2 * de)
    wd_all = moe_w_down.reshape(depth * ne, de, d)
    bd_all = moe_b_down.reshape(depth * ne, 1, d)

    for i in range(depth):
        j = i // 2
        mod = matmul(cond_pad, ada_w[i].astype(BF16), 16, 6 * d // 4)[:b + 1] + ada_b[i]
        mod6 = mod.reshape(b + 1, 6, d)
        pad2 = jnp.zeros((b, CTX_ROW0 - 6, d), F32)
        mod16 = jnp.concatenate([mod6[:b], pad2, jnp.broadcast_to(mod6[b], (b, 6, d)), pad2], axis=1)
        if i % 2 == 0:
            xa = _even_mixer(xa, mix_norm_g[i], mod16, s, ev_w_in[j], ev_w_out[j], ssd_conv_w[j], ssd_conv_b[j],
                             ssd_a_log[j], ssd_dt_bias[j], ssd_d[j], ssd_norm_g[j], sc_conv_w[j])
        else:
            lambda_init = 0.8 - 0.6 * math.exp(-0.3 * i)
            xa = _odd_mixer(xa, mix_norm_g[i], mod16, s, od_w_in[j], od_w_out[j], na_qk_g[j], na_rel_bias[j],
                            df_qk_g[j], df_lambda[j], df_subln_g[j], lambda_init, rope_tabs)
        xa = moe_ffn(xa.reshape(b * l, d), ffn_norm_g[i], mod16, l, s, router_w[i], router_b[i],
                     wgu_all, bgu_all, wd_all, bd_all, i * ne).reshape(b, l, d)
    return xa[:, :s]
```

```python
import functools
import math

import jax
import jax.numpy as jnp
from jax import lax
from jax.experimental import pallas as pl
from jax.experimental.pallas import tpu as pltpu

F32 = jnp.float32
BF16 = jnp.bfloat16

GRID_W = 64
HEAD_DIM = 64
EPS = 1e-6
SSD_HEADS = 16
SSD_GROUPS = 4
SSD_STATE = 128
NA_ROWS = 8
NA_COLS = 16
NA_HEADS = 8
DF_HEADS = 4
N_EXPERTS = 32
TOP_K = 4
SWIGLU_LIMIT = 7.0
SWIGLU_ALPHA = 1.702
ROPE_BASE = 10000.0

LANES = 128
SSD_Q = 128
MOE_TM = 512
ROUTER_TM = 256
VMEM_LIMIT = 56 * 1024 * 1024
NEG = -1e30


def _cparams(sem):
    return pltpu.CompilerParams(dimension_semantics=sem, vmem_limit_bytes=VMEM_LIMIT)


def _mm_kernel(a_ref, w_ref, o_ref):
    o_ref[...] = jnp.dot(a_ref[...], w_ref[...], preferred_element_type=F32).astype(o_ref.dtype)


def matmul(a, w, tm, tn, out_dtype=F32):
    m, k = a.shape
    n = w.shape[1]
    return pl.pallas_call(
        _mm_kernel,
        grid=(n // tn, m // tm),
        in_specs=[pl.BlockSpec((tm, k), lambda j, i: (i, 0)),
                  pl.BlockSpec((k, tn), lambda j, i: (0, j))],
        out_specs=pl.BlockSpec((tm, tn), lambda j, i: (i, j)),
        out_shape=jax.ShapeDtypeStruct((m, n), out_dtype),
        compiler_params=_cparams(("parallel", "parallel")),
        name="matmul",
    )(a, w)


MOD_ROWS = 16
CTX_ROW0 = 8
ODD_TN = 512


def _row_is_ctx(i, tm, tiles_per_batch, n_lat):
    t = i % tiles_per_batch
    row = t * tm + lax.broadcasted_iota(jnp.int32, (tm, 1), 0)
    return row >= n_lat


def _mod_row(mod, is_ctx, which):
    return jnp.where(is_ctx, mod[CTX_ROW0 + which:CTX_ROW0 + which + 1], mod[which:which + 1])


def _modulated_rows(x_ref, g_ref, mod_ref, is_ctx, sh_row, sc_row):
    x = x_ref[...]
    y = x * lax.rsqrt(jnp.mean(x * x, axis=-1, keepdims=True) + EPS) * g_ref[...]
    mod = mod_ref[0]
    return y * (1.0 + _mod_row(mod, is_ctx, sc_row)) + _mod_row(mod, is_ctx, sh_row)


def _proj_kernel(x_ref, g_ref, mod_ref, w_ref, o_ref, h_ref, *, tiles_per_batch, n_lat, sh_row, sc_row):
    i = pl.program_id(0)

    @pl.when(pl.program_id(1) == 0)
    def _():
        is_ctx = _row_is_ctx(i, x_ref.shape[0], tiles_per_batch, n_lat)
        h_ref[...] = _modulated_rows(x_ref, g_ref, mod_ref, is_ctx, sh_row, sc_row).astype(BF16)

    o_ref[...] = jnp.dot(h_ref[...], w_ref[...], preferred_element_type=F32)


def _group_rms(a, gsum, gain):
    outs = []
    half = gsum.shape[0]
    for hf in range(a.shape[1] // half):
        ah = a[:, hf * half:(hf + 1) * half]
        ms = jnp.dot((ah * ah).astype(BF16), gsum, preferred_element_type=F32) * (1.0 / HEAD_DIM)
        outs.append(ah * lax.rsqrt(ms + EPS))
    return jnp.concatenate(outs, axis=1) * gain


def _proj_odd_kernel(x_ref, g_ref, mod_ref, w_ref, gain_ref, gsum_ref, cos_ref, sina_ref, sinb_ref, o_ref, h_ref, *,
                     tiles_per_batch, n_lat, sh_row, sc_row):
    i = pl.program_id(0)
    j = pl.program_id(1)

    @pl.when(j == 0)
    def _():
        is_ctx = _row_is_ctx(i, x_ref.shape[0], tiles_per_batch, n_lat)
        h_ref[...] = _modulated_rows(x_ref, g_ref, mod_ref, is_ctx, sh_row, sc_row).astype(BF16)

    acc = jnp.dot(h_ref[...], w_ref[...], preferred_element_type=F32)
    is_norm = (j == 0) | (j == 1) | (j == 3) | (j == 4)
    is_rope = (j == 3) | (j == 4)

    @pl.when(jnp.logical_not(is_norm))
    def _():
        o_ref[...] = acc.astype(BF16)

    @pl.when(is_norm & jnp.logical_not(is_rope))
    def _():
        o_ref[...] = _group_rms(acc, gsum_ref[...], gain_ref[0]).astype(BF16)

    @pl.when(is_rope)
    def _():
        xn = _group_rms(acc, gsum_ref[...], gain_ref[0])
        reps = xn.shape[1] // LANES
        cos = jnp.concatenate([cos_ref[...]] * reps, axis=1)
        sina = jnp.concatenate([sina_ref[...]] * reps, axis=1)
        sinb = jnp.concatenate([sinb_ref[...]] * reps, axis=1)
        quarter = HEAD_DIM // 4
        up = pltpu.roll(xn, xn.shape[1] - quarter, axis=1)
        dn = pltpu.roll(xn, quarter, axis=1)
        o_ref[...] = (xn * cos + up * sina + dn * sinb).astype(BF16)


def _pick_rows(l):
    for tm in (768, 512, 256):
        if l % tm == 0:
            return tm
    raise ValueError(l)


def proj_modulated(x2, gain, mod16, w_bf, l, n_lat, sh_row, sc_row, tn):
    m, d = x2.shape
    n = w_bf.shape[1]
    tm = _pick_rows(l)
    tpb = l // tm
    return pl.pallas_call(
        functools.partial(_proj_kernel, tiles_per_batch=tpb, n_lat=n_lat, sh_row=sh_row, sc_row=sc_row),
        grid=(m // tm, n // tn),
        in_specs=[pl.BlockSpec((tm, d), lambda i, j: (i, 0)),
                  pl.BlockSpec((1, d), lambda i, j: (0, 0)),
                  pl.BlockSpec((1, MOD_ROWS, d), lambda i, j: (i // tpb, 0, 0)),
                  pl.BlockSpec((d, tn), lambda i, j: (0, j))],
        out_specs=pl.BlockSpec((tm, tn), lambda i, j: (i, j)),
        out_shape=jax.ShapeDtypeStruct((m, n), F32),
        scratch_shapes=[pltpu.VMEM((tm, d), BF16)],
        compiler_params=_cparams(("parallel", "arbitrary")),
        name="proj_even",
    )(x2, gain.reshape(1, d), mod16, w_bf)


def proj_odd(x2, gain, mod16, w_bf, gains6, cos2, sina2, sinb2, l, n_lat, sh_row, sc_row):
    m, d = x2.shape
    n = w_bf.shape[1]
    tn = ODD_TN
    tm = _pick_rows(l)
    tpb = l // tm
    half = 256
    gi = jnp.arange(half) // HEAD_DIM
    gsum = (gi[:, None] == gi[None, :]).astype(BF16)
    tab = pl.BlockSpec((tm, LANES), lambda i, j: (i % tpb, 0))
    return pl.pallas_call(
        functools.partial(_proj_odd_kernel, tiles_per_batch=tpb, n_lat=n_lat, sh_row=sh_row, sc_row=sc_row),
        grid=(m // tm, n // tn),
        in_specs=[pl.BlockSpec((tm, d), lambda i, j: (i, 0)),
                  pl.BlockSpec((1, d), lambda i, j: (0, 0)),
                  pl.BlockSpec((1, MOD_ROWS, d), lambda i, j: (i // tpb, 0, 0)),
                  pl.BlockSpec((d, tn), lambda i, j: (0, j)),
                  pl.BlockSpec((1, 1, tn), lambda i, j: (j, 0, 0)),
                  pl.BlockSpec((half, half), lambda i, j: (0, 0)),
                  tab, tab, tab],
        out_specs=pl.BlockSpec((tm, tn), lambda i, j: (i, j)),
        out_shape=jax.ShapeDtypeStruct((m, n), BF16),
        scratch_shapes=[pltpu.VMEM((tm, d), BF16)],
        compiler_params=_cparams(("parallel", "arbitrary")),
        name="proj_odd",
    )(x2, gain.reshape(1, d), mod16, w_bf, gains6, gsum, cos2, sina2, sinb2)


def _mm_res_kernel(*refs, n_parts, tiles_per_batch, n_lat, gate_row):
    a_refs = refs[:n_parts]
    w_ref, r_ref, mod_ref, o_ref = refs[n_parts:]
    is_ctx = _row_is_ctx(pl.program_id(0), r_ref.shape[0], tiles_per_batch, n_lat)
    gate = _mod_row(mod_ref[0], is_ctx, gate_row)
    acc = None
    k0 = 0
    for a_ref in a_refs:
        k1 = k0 + a_ref.shape[1]
        t = jnp.dot(a_ref[...], w_ref[k0:k1, :], preferred_element_type=F32)
        acc = t if acc is None else acc + t
        k0 = k1
    o_ref[...] = r_ref[...] + gate * acc


def matmul_residual(parts, w_bf, res, mod16, l, n_lat, gate_row):
    m = res.shape[0]
    k, d = w_bf.shape
    tm = _pick_rows(l)
    tpb = l // tm
    return pl.pallas_call(
        functools.partial(_mm_res_kernel, n_parts=len(parts), tiles_per_batch=tpb, n_lat=n_lat, gate_row=gate_row),
        grid=(m // tm,),
        in_specs=[pl.BlockSpec((tm, a.shape[1]), lambda i: (i, 0)) for a in parts]
                 + [pl.BlockSpec((k, d), lambda i: (0, 0)),
                    pl.BlockSpec((tm, d), lambda i: (i, 0)),
                    pl.BlockSpec((1, MOD_ROWS, d), lambda i: (i // tpb, 0, 0))],
        out_specs=pl.BlockSpec((tm, d), lambda i: (i, 0)),
        out_shape=jax.ShapeDtypeStruct((m, d), F32),
        compiler_params=_cparams(("parallel",)),
        name="out_proj",
    )(*parts, w_bf, res, mod16)


EV_TM = 256
HALO = 8


def _even_prep_kernel(xa_ref, xa_p, xa_n, xb_ref, xb_p, xb_n, gb_ref, gc_ref, gc_p, gc_n, gh_ref, gh_p, gh_n,
                      cw_ref, cb_ref, sw_ref, xs_ref, bm_ref, cm_ref, sc_ref, scr, *, tiles_per_batch, n_lat, l):
    tm = xa_ref.shape[0]
    r0 = (pl.program_id(0) % tiles_per_batch) * tm
    has_prev = (r0 != 0) & (r0 != n_lat)
    has_next = (r0 + tm != n_lat) & (r0 + tm != l)

    def conv(main, prev, nxt, w):
        width = w.shape[0]
        scr[0:HALO] = jnp.where(has_prev, prev, 0.0)
        scr[HALO:HALO + tm] = main
        scr[HALO + tm:2 * HALO + tm] = jnp.where(has_next, nxt, 0.0)
        out = None
        for kk in range(width):
            o = HALO + kk - width // 2
            t = scr[o:o + tm] * w[kk:kk + 1]
            out = t if out is None else out + t
        return out

    d = xa_ref.shape[1]
    cw = cw_ref[...]
    cb = cb_ref[...]
    xs = conv(xa_ref[...], xa_p[...], xa_n[...], cw[:, :d]) + cb[:, :d]
    xs_ref[...] = (xs * jax.nn.sigmoid(xs)).astype(BF16)
    bc = conv(xb_ref[...], xb_p[...], xb_n[...], cw[:, d:]) + cb[:, d:]
    bc = (bc * jax.nn.sigmoid(bc)).astype(BF16)
    gn = bm_ref.shape[1]
    bm_ref[...] = bc[:, :gn]
    cm_ref[...] = bc[:, gn:]
    sc = gb_ref[...] * conv(gc_ref[...] * gh_ref[...], gc_p[...] * gh_p[...], gc_n[...] * gh_n[...], sw_ref[...])
    sc_ref[...] = sc.astype(BF16)


def even_prep(proj2, conv_w, conv_b, sc_conv_w, l, n_lat, d):
    m = proj2.shape[0]
    tm = EV_TM
    tpb = l // tm
    nhb = m // HALO
    gn = SSD_GROUPS * SSD_STATE

    def main(cblk):
        return pl.BlockSpec((tm, d), lambda i: (i, cblk))

    def prev(cblk):
        return pl.BlockSpec((HALO, d), lambda i: (jnp.maximum(i * (tm // HALO) - 1, 0), cblk))

    def nxt(cblk):
        return pl.BlockSpec((HALO, d), lambda i: (jnp.minimum((i + 1) * (tm // HALO), nhb - 1), cblk))

    full = lambda a: pl.BlockSpec(a.shape, lambda i: (0, 0))
    cb2 = conv_b.reshape(1, -1)
    return pl.pallas_call(
        functools.partial(_even_prep_kernel, tiles_per_batch=tpb, n_lat=n_lat, l=l),
        grid=(m // tm,),
        in_specs=[main(1), prev(1), nxt(1), main(2), prev(2), nxt(2), main(3),
                  main(4), prev(4), nxt(4), main(5), prev(5), nxt(5),
                  full(conv_w), full(cb2), full(sc_conv_w)],
        out_specs=[pl.BlockSpec((tm, d), lambda i: (i, 0)), pl.BlockSpec((tm, gn), lambda i: (i, 0)),
                   pl.BlockSpec((tm, gn), lambda i: (i, 0)), pl.BlockSpec((tm, d), lambda i: (i, 0))],
        out_shape=[jax.ShapeDtypeStruct((m, d), BF16), jax.ShapeDtypeStruct((m, gn), BF16),
                   jax.ShapeDtypeStruct((m, gn), BF16), jax.ShapeDtypeStruct((m, d), BF16)],
        scratch_shapes=[pltpu.VMEM((tm + 2 * HALO, d), F32)],
        compiler_params=_cparams(("parallel",)),
        name="even_prep",
    )(proj2, proj2, proj2, proj2, proj2, proj2, proj2, proj2, proj2, proj2, proj2, proj2, proj2,
      conv_w, cb2, sc_conv_w)


GU_BLK = 2 * LANES


def _gate_up_perm():
    src = jnp.arange(GU_BLK)
    dst = jnp.where(src % 2 == 0, src // 2, LANES + src // 2)
    return (dst[:, None] == jnp.arange(GU_BLK)[None, :]).astype(BF16)


def _deinterleave_into(w_ref, perm, o_ref):
    for blk in range(o_ref.shape[1] // GU_BLK):
        cs = slice(blk * GU_BLK, (blk + 1) * GU_BLK)
        o_ref[:, cs] = jnp.dot(w_ref[0, :, cs].astype(BF16), perm, preferred_element_type=F32).astype(BF16)


def _split3_dot(tri, a, dims):
    a1 = a.astype(BF16)
    r1 = a - a1.astype(F32)
    a2 = r1.astype(BF16)
    a3 = (r1 - a2.astype(F32)).astype(BF16)
    out = None
    for piece in (a1, a2, a3):
        if dims == "tri_a":
            t = jnp.dot(tri, piece, preferred_element_type=F32)
        else:
            t = jnp.dot(piece, tri, preferred_element_type=F32)
        out = t if out is None else out + t
    return out


def _ssd_kernel(*refs, rev, post):
    if post:
        x_ref, bt_ref, b_ref, c_ref, da_ref, dat_ref, yo_ref, z_ref, dsk_ref, ng_ref, y_ref, st_ref, yacc = refs
    else:
        x_ref, bt_ref, b_ref, c_ref, da_ref, dat_ref, y_ref, st_ref = refs
    q = SSD_Q
    step = pl.program_id(1)

    @pl.when(step == 0)
    def _():
        st_ref[...] = jnp.zeros_like(st_ref)

    da = da_ref[0, 0]
    dat = dat_ref[0, 0]
    a_c = da[:, SSD_HEADS:2 * SSD_HEADS]
    dt_r = dat[0:SSD_HEADS]
    a_r = dat[SSD_HEADS:2 * SSD_HEADS]
    ri = lax.broadcasted_iota(jnp.int32, (q, q), 0)
    ci = lax.broadcasted_iota(jnp.int32, (q, q), 1)
    tri = (ci <= ri).astype(BF16)
    tri_t = (ri <= ci).astype(BF16)
    cum_c = _split3_dot(tri, a_c, "tri_a")
    cum_r = _split3_dot(tri_t, a_r, "a_tri")
    tot_r = cum_r[:, q - 1:q]
    if rev:
        pos_c = cum_c - a_c
        pos_r = cum_r - a_r
        mask = ci >= ri
    else:
        pos_c = cum_c
        pos_r = cum_r
        mask = ri >= ci
    lane = lax.broadcasted_iota(jnp.int32, (q, LANES), 1)
    lane_n = lax.broadcasted_iota(jnp.int32, (SSD_STATE, LANES), 1)
    heads_per_group = SSD_HEADS // SSD_GROUPS

    for g in range(SSD_GROUPS):
        gs = slice(g * SSD_STATE, (g + 1) * SSD_STATE)
        bg = b_ref[0, :, gs]
        cg = c_ref[0, :, gs]
        btg = bt_ref[0, gs, :].astype(F32)
        cb = lax.dot_general(cg, bg, (((1,), (1,)), ((), ())), preferred_element_type=F32)
        cg32 = cg.astype(F32)
        for pp in range(heads_per_group // 2):
            p = g * (heads_per_group // 2) + pp
            xp = x_ref[0, :, p * LANES:(p + 1) * LANES]
            st = st_ref[p]
            rhs = jnp.concatenate([xp, st.astype(BF16)], axis=0)
            ys, ds, decs = [], [], []
            for hh in range(2):
                h = 2 * p + hh
                colb = jnp.broadcast_to(pos_c[:, h:h + 1], (q, LANES))
                row = pos_r[h:h + 1, :]
                dt_row = dt_r[h:h + 1, :]
                tot = tot_r[h:h + 1, :]
                if rev:
                    seg = row - colb
                    coff = jnp.exp(tot - colb)
                    w_row = dt_row * jnp.exp(row)
                else:
                    seg = colb - row
                    coff = jnp.exp(colb)
                    w_row = dt_row * jnp.exp(tot - row)
                decay = jnp.exp(jnp.where(mask, seg, NEG))
                m_h = (cb * decay * dt_row).astype(BF16)
                c_h = (cg32 * coff).astype(BF16)
                lhs = jnp.concatenate([m_h, c_h], axis=1)
                ys.append(jnp.dot(lhs, rhs, preferred_element_type=F32))
                btw = (btg * w_row).astype(BF16)
                ds.append(jnp.dot(btw, xp, preferred_element_type=F32))
                decs.append(jnp.exp(tot))
            y_pair = jnp.where(lane < HEAD_DIM, ys[0], ys[1])
            if post:
                yacc[:, p * LANES:(p + 1) * LANES] = y_pair
            else:
                y_ref[0, :, p * LANES:(p + 1) * LANES] = y_pair
            st_ref[p] = jnp.where(lane_n < HEAD_DIM, decs[0] * st + ds[0], decs[1] * st + ds[1])

    if post:
        z = z_ref[...]
        y = (yacc[...] + yo_ref[0] + dsk_ref[...] * x_ref[0].astype(F32)) * (z * jax.nn.sigmoid(z))
        y = y * lax.rsqrt(jnp.mean(y * y, axis=-1, keepdims=True) + EPS) * ng_ref[...]
        y_ref[0] = y.astype(y_ref.dtype)


def ssd_scan(xs, bt, bm, cm, da, dat, n_lat, rev, post=None):
    b, l, d = xs.shape
    nch = l // SSD_Q
    nlat = n_lat // SSD_Q
    nctx = nch - nlat
    d_idx = 1 if rev else 0
    if rev:
        def chunk(j):
            return nch - 1 - j
    else:
        def chunk(j):
            return jnp.where(j < nctx, nlat + j, j - nctx)
    gn = SSD_GROUPS * SSD_STATE
    tok = pl.BlockSpec((1, SSD_Q, d), lambda i, j: (i, chunk(j), 0))
    in_specs = [tok,
                pl.BlockSpec((1, gn, SSD_Q), lambda i, j: (i, 0, chunk(j))),
                pl.BlockSpec((1, SSD_Q, gn), lambda i, j: (i, chunk(j), 0)),
                pl.BlockSpec((1, SSD_Q, gn), lambda i, j: (i, chunk(j), 0)),
                pl.BlockSpec((1, 1, SSD_Q, 2 * SSD_HEADS), lambda i, j: (i, d_idx, chunk(j), 0)),
                pl.BlockSpec((1, 1, 2 * SSD_HEADS, SSD_Q), lambda i, j: (i, d_idx, 0, chunk(j)))]
    scratch = [pltpu.VMEM((SSD_HEADS // 2, SSD_STATE, LANES), F32)]
    args = (xs, bt, bm, cm, da, dat)
    if post is not None:
        row = pl.BlockSpec((1, d), lambda i, j: (0, 0))
        in_specs += [tok, pl.BlockSpec((SSD_Q, d), lambda i, j: (i * nch + chunk(j), 0)), row, row]
        scratch.append(pltpu.VMEM((SSD_Q, d), F32))
        args += tuple(post)
    return pl.pallas_call(
        functools.partial(_ssd_kernel, rev=rev, post=post is not None),
        grid=(b, nch),
        in_specs=in_specs,
        out_specs=tok,
        out_shape=jax.ShapeDtypeStruct((b, l, d), F32 if post is None else BF16),
        scratch_shapes=scratch,
        compiler_params=_cparams(("parallel", "arbitrary")),
        name="ssd_bwd" if rev else "ssd_fwd",
    )(*args)


def _stack_halves(qv):
    lane = lax.broadcasted_iota(jnp.int32, qv.shape, 1)
    zero = jnp.zeros_like(qv)
    q1 = jnp.where(lane < HEAD_DIM, qv, zero)
    q2 = jnp.where(lane >= HEAD_DIM, qv, zero)
    return jnp.concatenate([q1, q2], axis=0) * jnp.asarray(HEAD_DIM ** -0.5, qv.dtype)


def _flash_kernel(lam_ref, g_ref, q_ref, k_ref, v_ref, o_ref, *, mode, tk, lambda_init):
    tq = q_ref.shape[1]
    nk = k_ref.shape[1] // tk
    nt = tk // LANES
    qs = _stack_halves(q_ref[0])

    def scores(u):
        return lax.dot_general(qs, k_ref[0, u * tk:(u + 1) * tk, :], (((1,), (1,)), ((), ())),
                               preferred_element_type=F32)

    m_old = jnp.full((2 * tq, LANES), -jnp.inf, F32)
    l_run = jnp.zeros((2 * tq, LANES), F32)
    acc = jnp.zeros((2 * tq, LANES), F32)
    s_next = scores(0)
    for u in range(nk):
        s = s_next
        if u + 1 < nk:
            s_next = scores(u + 1)
        tiles = [s[:, t * LANES:(t + 1) * LANES] for t in range(nt)]
        smax = tiles[0]
        for t in tiles[1:]:
            smax = jnp.maximum(smax, t)
        m_new = jnp.maximum(m_old, jnp.max(smax, axis=-1, keepdims=True))
        alpha = jnp.exp(m_old - m_new)
        ps = [jnp.exp(t - m_new) for t in tiles]
        psum = ps[0]
        for t in ps[1:]:
            psum = psum + t
        l_run = alpha * l_run + jnp.sum(psum, axis=-1, keepdims=True)
        p = jnp.concatenate([t.astype(BF16) for t in ps], axis=1)
        acc = alpha * acc + jnp.dot(p, v_ref[0, u * tk:(u + 1) * tk, :], preferred_element_type=F32)
        m_old = m_new
    o = acc / l_run
    o1, o2 = o[:tq], o[tq:]
    if mode == "diff":
        lp = lam_ref[...]
        s01 = jnp.sum(lp[0:1] * lp[1:2], axis=-1, keepdims=True)
        s23 = jnp.sum(lp[2:3] * lp[3:4], axis=-1, keepdims=True)
        lam = jnp.exp(s01) - jnp.exp(s23) + lambda_init
        out = o1 - lam * o2
        out = out * lax.rsqrt(jnp.mean(out * out, axis=-1, keepdims=True) + EPS) * g_ref[...] * (1.0 - lambda_init)
    else:
        lane = lax.broadcasted_iota(jnp.int32, o1.shape, 1)
        out = jnp.where(lane < HEAD_DIM, o1, o2)
    o_ref[0] = out.astype(o_ref.dtype)


def flash_slabs(qkv, lam_p, subln_g, mode, tq, tk, lq, lk, q_blk0, k_blk, q_slab0, k_slab0, v_slab0, n_slabs,
                lambda_init=0.0):
    b = qkv.shape[0]
    return pl.pallas_call(
        functools.partial(_flash_kernel, mode=mode, tk=tk, lambda_init=lambda_init),
        grid=(b, n_slabs, lq // tq),
        in_specs=[pl.BlockSpec(lam_p.shape, lambda i, h, j: (0, 0)),
                  pl.BlockSpec((1, LANES), lambda i, h, j: (0, 0)),
                  pl.BlockSpec((1, tq, LANES), lambda i, h, j: (i, q_blk0 + j, q_slab0 + h)),
                  pl.BlockSpec((1, lk, LANES), lambda i, h, j: (i, k_blk, k_slab0 + h)),
                  pl.BlockSpec((1, lk, LANES), lambda i, h, j: (i, k_blk, v_slab0 + h))],
        out_specs=pl.BlockSpec((1, tq, LANES), lambda i, h, j: (i, j, h)),
        out_shape=jax.ShapeDtypeStruct((b, lq, n_slabs * LANES), BF16),
        compiler_params=_cparams(("parallel", "parallel", "arbitrary")),
        name="flash_" + mode,
    )(lam_p, subln_g, qkv, qkv, qkv)


NA_RB = 8
NA_BLK = NA_RB * GRID_W
NA_WIN = NA_ROWS * GRID_W


def _na_kernel(q_ref, kp_ref, kc_ref, kn_ref, vp_ref, vc_ref, vn_ref, kx_ref, vx_ref, bias_ref, o_ref,
               kbuf, vbuf, *, rows):
    rb = pl.program_id(1)
    kbuf[0:NA_BLK] = kp_ref[0]
    kbuf[NA_BLK:2 * NA_BLK] = kc_ref[0]
    kbuf[2 * NA_BLK:3 * NA_BLK] = kn_ref[0]
    vbuf[0:NA_BLK] = vp_ref[0]
    vbuf[NA_BLK:2 * NA_BLK] = vc_ref[0]
    vbuf[2 * NA_BLK:3 * NA_BLK] = vn_ref[0]
    npairs = q_ref.shape[2] // LANES
    lane = lax.broadcasted_iota(jnp.int32, (GRID_W, LANES), 1)

    def row_body(rl, carry):
        r = rb * NA_RB + rl
        r_start = jnp.clip(r - NA_ROWS // 2, 0, rows - NA_ROWS)
        off = r_start - (rb * NA_RB - NA_RB)
        di0 = r_start - r + NA_ROWS - 1
        tok0 = pl.multiple_of(off * GRID_W, GRID_W)
        q0 = pl.multiple_of(rl * GRID_W, GRID_W)
        for p in range(npairs):
            ls = slice(p * LANES, (p + 1) * LANES)
            qs = _stack_halves(q_ref[0, pl.ds(q0, GRID_W), ls])
            kw = kbuf[pl.ds(tok0, NA_WIN), ls]
            vw = vbuf[pl.ds(tok0, NA_WIN), ls]
            s_nb = lax.dot_general(qs, kw, (((1,), (1,)), ((), ())), preferred_element_type=F32)
            bias = jnp.concatenate([bias_ref[p, di0 + 2 * j] for j in range(NA_ROWS // 2)], axis=1)
            s_nb = s_nb + bias
            s_cx = lax.dot_general(qs, kx_ref[0, :, ls], (((1,), (1,)), ((), ())), preferred_element_type=F32)
            m = jnp.maximum(jnp.max(s_nb, axis=-1, keepdims=True), jnp.max(s_cx, axis=-1, keepdims=True))
            p_nb = jnp.exp(s_nb - m)
            p_cx = jnp.exp(s_cx - m)
            l = jnp.sum(p_nb, axis=-1, keepdims=True) + jnp.sum(p_cx, axis=-1, keepdims=True)
            o = (jnp.dot(p_nb.astype(BF16), vw, preferred_element_type=F32)
                 + jnp.dot(p_cx.astype(BF16), vx_ref[0, :, ls], preferred_element_type=F32)) / l
            o_ref[0, pl.ds(q0, GRID_W), ls] = jnp.where(lane < HEAD_DIM, o[:GRID_W], o[GRID_W:]).astype(o_ref.dtype)
        return carry

    lax.fori_loop(0, NA_RB, row_body, 0, unroll=True)


def na_bias_table(rel_bias):
    cols = jnp.arange(GRID_W)
    c_start = jnp.clip(cols - NA_COLS // 2, 0, GRID_W - NA_COLS)
    kc = jnp.arange(GRID_W)
    valid = (kc[None, :] >= c_start[:, None]) & (kc[None, :] < c_start[:, None] + NA_COLS)
    idx = jnp.clip(kc[None, :] - cols[:, None] + NA_COLS - 1, 0, 2 * NA_COLS - 2)
    t = jnp.where(valid[None, None], rel_bias[:, :, idx], NEG)
    t2 = jnp.concatenate([t[:, :-1], t[:, 1:]], axis=-1)
    nh, nd = t2.shape[0], t2.shape[1]
    t2 = t2.reshape(nh // 2, 2, nd, GRID_W, LANES).transpose(0, 2, 1, 3, 4)
    return t2.reshape(nh // 2, nd, 2 * GRID_W, LANES).astype(F32)


def na_attention(qkv, bias_tab, n_lat):
    b, l, _ = qkv.shape
    w = NA_HEADS * HEAD_DIM
    rows = n_lat // GRID_W
    nrb = rows // NA_RB
    n_ctx = l - n_lat
    ctx_blk = n_lat // n_ctx

    def blk(step, stream):
        if step < 0:
            return pl.BlockSpec((1, NA_BLK, w), lambda i, j: (i, jnp.maximum(j - 1, 0), stream))
        if step > 0:
            return pl.BlockSpec((1, NA_BLK, w), lambda i, j: (i, jnp.minimum(j + 1, nrb - 1), stream))
        return pl.BlockSpec((1, NA_BLK, w), lambda i, j: (i, j, stream))

    def ctx(stream):
        return pl.BlockSpec((1, n_ctx, w), lambda i, j: (i, ctx_blk, stream))

    return pl.pallas_call(
        functools.partial(_na_kernel, rows=rows),
        grid=(b, nrb),
        in_specs=[blk(0, 0), blk(-1, 1), blk(0, 1), blk(1, 1), blk(-1, 2), blk(0, 2), blk(1, 2), ctx(1), ctx(2),
                  pl.BlockSpec(bias_tab.shape, lambda i, j: (0, 0, 0, 0))],
        out_specs=pl.BlockSpec((1, NA_BLK, w), lambda i, j: (i, j, 0)),
        out_shape=jax.ShapeDtypeStruct((b, n_lat, w), BF16),
        scratch_shapes=[pltpu.VMEM((3 * NA_BLK, w), BF16), pltpu.VMEM((3 * NA_BLK, w), BF16)],
        compiler_params=_cparams(("parallel", "parallel")),
        name="na_attention",
    )(qkv, qkv, qkv, qkv, qkv, qkv, qkv, qkv, qkv, bias_tab)


def _router_kernel(x_ref, g_ref, sh_ref, sc_ref, w_ref, b_ref, h_ref, ri_ref, rg_ref, cnt_ref, carry_ref):
    step = pl.program_id(0)

    @pl.when(step == 0)
    def _():
        carry_ref[...] = jnp.zeros_like(carry_ref)

    tm = x_ref.shape[0]
    x = x_ref[...]
    y = x * lax.rsqrt(jnp.mean(x * x, axis=-1, keepdims=True) + EPS) * g_ref[...]
    h = y * (1.0 + sc_ref[0]) + sh_ref[0]
    h_ref[...] = h.astype(BF16)
    w = w_ref[...]
    h1 = h.astype(BF16)
    h2 = (h - h1.astype(F32)).astype(BF16)
    w1 = w.astype(BF16)
    w2 = (w - w1.astype(F32)).astype(BF16)
    logits = (jnp.dot(h1, w1, preferred_element_type=F32) + jnp.dot(h2, w1, preferred_element_type=F32)
              + jnp.dot(h1, w2, preferred_element_type=F32)) + b_ref[...]
    lane = lax.broadcasted_iota(jnp.int32, (tm, LANES), 1)
    work = logits
    tops, idxs, hots = [], [], []
    for _ in range(TOP_K):
        mx = jnp.max(work, axis=-1, keepdims=True)
        ix = jnp.min(jnp.where(work == mx, lane, LANES), axis=-1, keepdims=True)
        hot = lane == ix
        work = jnp.where(hot, -jnp.inf, work)
        tops.append(mx)
        idxs.append(ix)
        hots.append(hot)
    es = [jnp.exp(t - tops[0]) for t in tops]
    den = es[0] + es[1] + es[2] + es[3]
    multi = jnp.zeros((tm, LANES), F32)
    for hot in hots:
        multi = multi + hot.astype(F32)
    ri = lax.broadcasted_iota(jnp.int32, (tm, tm), 0)
    ci = lax.broadcasted_iota(jnp.int32, (tm, tm), 1)
    tri = (ci < ri).astype(BF16)
    cum = jnp.dot(tri, multi.astype(BF16), preferred_element_type=F32) + carry_ref[...]
    carry_new = carry_ref[...] + jnp.sum(multi, axis=0, keepdims=True)
    carry_ref[...] = carry_new
    cnt_ref[...] = carry_new
    out_i = jnp.zeros((tm, LANES), jnp.int32)
    out_g = jnp.zeros((tm, LANES), F32)
    for kk in range(TOP_K):
        rank = jnp.sum(jnp.where(hots[kk], cum, 0.0), axis=-1, keepdims=True).astype(jnp.int32)
        out_i = jnp.where(lane == kk, idxs[kk], out_i)
        out_i = jnp.where(lane == TOP_K + kk, rank, out_i)
        out_g = jnp.where(lane == kk, es[kk] / den, out_g)
    ri_ref[...] = out_i
    rg_ref[...] = out_g


def moe_router(x2, gain, modarr, sh_row, sc_row, n_lat_tiles, tiles_per_batch, router_w, router_b):
    n, d = x2.shape
    tm = ROUTER_TM
    w_pad = jnp.zeros((d, LANES), F32).at[:, :N_EXPERTS].set(router_w)
    b_pad = jnp.full((1, LANES), NEG, F32).at[0, :N_EXPERTS].set(router_b)

    def mod_idx(which):
        def f(i):
            bidx = i // tiles_per_batch
            is_ctx = (i % tiles_per_batch) >= n_lat_tiles
            return (bidx * MOD_ROWS + is_ctx.astype(jnp.int32) * CTX_ROW0 + which, 0, 0)
        return f

    return pl.pallas_call(
        _router_kernel,
        grid=(n // tm,),
        in_specs=[pl.BlockSpec((tm, d), lambda i: (i, 0)),
                  pl.BlockSpec((1, d), lambda i: (0, 0)),
                  pl.BlockSpec((1, 1, d), mod_idx(sh_row)),
                  pl.BlockSpec((1, 1, d), mod_idx(sc_row)),
                  pl.BlockSpec((d, LANES), lambda i: (0, 0)),
                  pl.BlockSpec((1, LANES), lambda i: (0, 0))],
        out_specs=[pl.BlockSpec((tm, d), lambda i: (i, 0)),
                   pl.BlockSpec((tm, LANES), lambda i: (i, 0)),
                   pl.BlockSpec((tm, LANES), lambda i: (i, 0)),
                   pl.BlockSpec((1, LANES), lambda i: (0, 0))],
        out_shape=[jax.ShapeDtypeStruct((n, d), BF16),
                   jax.ShapeDtypeStruct((n, LANES), jnp.int32),
                   jax.ShapeDtypeStruct((n, LANES), F32),
                   jax.ShapeDtypeStruct((1, LANES), F32)],
        scratch_shapes=[pltpu.VMEM((1, LANES), F32)],
        compiler_params=_cparams(("arbitrary",)),
        name="moe_router",
    )(x2, gain.reshape(1, d), modarr, modarr, w_pad, b_pad)


def _expert_kernel(be_ref, nu_ref, x_ref, wgu_ref, bgu_ref, wd_ref, bd_ref, perm_ref, o_ref, wgu_bf, wd_bf):
    j = pl.program_id(0)
    used = j < nu_ref[0]
    changed = (j == 0) | (be_ref[j] != be_ref[jnp.maximum(j - 1, 0)])

    @pl.when(used & changed)
    def _():
        wd_bf[...] = wd_ref[0].astype(BF16)
        _deinterleave_into(wgu_ref, perm_ref[...], wgu_bf)

    @pl.when(used)
    def _():
        gu = jnp.dot(x_ref[...], wgu_bf[...], preferred_element_type=F32) + bgu_ref[0]
        acts = []
        for blk in range(gu.shape[1] // GU_BLK):
            gate = jnp.minimum(gu[:, blk * GU_BLK:blk * GU_BLK + LANES], SWIGLU_LIMIT)
            up = jnp.clip(gu[:, blk * GU_BLK + LANES:(blk + 1) * GU_BLK], -SWIGLU_LIMIT, SWIGLU_LIMIT)
            acts.append((gate * jax.nn.sigmoid(SWIGLU_ALPHA * gate) * (up + 1.0)).astype(BF16))
        act = jnp.concatenate(acts, axis=1)
        o_ref[...] = (jnp.dot(act, wd_bf[...], preferred_element_type=F32) + bd_ref[0]).astype(o_ref.dtype)

    @pl.when(jnp.logical_not(used))
    def _():
        o_ref[...] = jnp.zeros_like(o_ref)


def moe_experts(xs, block_expert, n_used, wgu, bgu, wd, bd, e_off):
    cap, d = xs.shape
    de = wd.shape[1]
    nblk = cap // MOE_TM
    wmap = lambda j, be, nu: (e_off + be[j], 0, 0)
    grid_spec = pltpu.PrefetchScalarGridSpec(
        num_scalar_prefetch=2,
        grid=(nblk,),
        in_specs=[pl.BlockSpec((MOE_TM, d), lambda j, be, nu: (j, 0)),
                  pl.BlockSpec((1, d, 2 * de), wmap),
                  pl.BlockSpec((1, 1, 2 * de), wmap),
                  pl.BlockSpec((1, de, d), wmap),
                  pl.BlockSpec((1, 1, d), wmap),
                  pl.BlockSpec((GU_BLK, GU_BLK), lambda j, be, nu: (0, 0))],
        out_specs=pl.BlockSpec((MOE_TM, d), lambda j, be, nu: (j, 0)),
        scratch_shapes=[pltpu.VMEM((d, 2 * de), BF16), pltpu.VMEM((de, d), BF16)],
    )
    return pl.pallas_call(
        _expert_kernel,
        grid_spec=grid_spec,
        out_shape=jax.ShapeDtypeStruct((cap, d), BF16),
        compiler_params=_cparams(("arbitrary",)),
        name="moe_experts",
    )(block_expert, n_used, xs, wgu, bgu, wd, bd, _gate_up_perm())


def moe_ffn(x2, gain, mod16, l, n_lat, router_w, router_b, wgu, bgu, wd, bd, e_off):
    n, d = x2.shape
    sh_row, sc_row, gate_row = 3, 4, 5
    h_bf, r_i, r_g, cnt = moe_router(x2, gain, mod16.reshape(-1, 1, d), sh_row, sc_row, n_lat // ROUTER_TM,
                                     l // ROUTER_TM, router_w, router_b)
    top_idx = r_i[:, :TOP_K]
    rank = r_i[:, TOP_K:2 * TOP_K]
    gates = r_g[:, :TOP_K]
    counts = cnt[0, :N_EXPERTS].astype(jnp.int32)
    padded = (counts + MOE_TM - 1) // MOE_TM * MOE_TM
    pad_end = jnp.cumsum(padded)
    pad_start = pad_end - padded
    dest = (pad_start[top_idx] + rank).T
    nblk = -(-(n * TOP_K + N_EXPERTS * (MOE_TM - 1)) // MOE_TM)
    cap = nblk * MOE_TM
    tok = jnp.broadcast_to(jnp.arange(n, dtype=jnp.int32)[None, :], (TOP_K, n))
    slot_tok = jnp.zeros((cap,), jnp.int32).at[dest.reshape(-1)].set(tok.reshape(-1), unique_indices=True)
    blk_start = jnp.arange(nblk, dtype=jnp.int32)[:, None] * MOE_TM
    block_expert = jnp.minimum(jnp.sum((pad_end[None, :] <= blk_start).astype(jnp.int32), axis=1), N_EXPERTS - 1)
    n_used = (pad_end[-1:] // MOE_TM).astype(jnp.int32)
    xs = h_bf[slot_tok]
    y = moe_experts(xs, block_expert, n_used, wgu, bgu, wd, bd, e_off)
    yk = y[dest.reshape(-1)].reshape(TOP_K, n, d)
    return moe_combine(yk, r_g, x2, mod16, l, n_lat, gate_row)


def _combine_kernel(y_ref, g_ref, r_ref, mod_ref, o_ref, *, tiles_per_batch, n_lat, gate_row):
    is_ctx = _row_is_ctx(pl.program_id(0), r_ref.shape[0], tiles_per_batch, n_lat)
    gate = _mod_row(mod_ref[0], is_ctx, gate_row)
    g = g_ref[...]
    acc = y_ref[0].astype(F32) * g[:, 0:1]
    for kk in range(1, TOP_K):
        acc = acc + y_ref[kk].astype(F32) * g[:, kk:kk + 1]
    o_ref[...] = r_ref[...] + gate * acc


def moe_combine(yk, r_g, res, mod16, l, n_lat, gate_row):
    _, n, d = yk.shape
    tm = _pick_rows(l)
    tpb = l // tm
    return pl.pallas_call(
        functools.partial(_combine_kernel, tiles_per_batch=tpb, n_lat=n_lat, gate_row=gate_row),
        grid=(n // tm,),
        in_specs=[pl.BlockSpec((TOP_K, tm, d), lambda i: (0, i, 0)),
                  pl.BlockSpec((tm, LANES), lambda i: (i, 0)),
                  pl.BlockSpec((tm, d), lambda i: (i, 0)),
                  pl.BlockSpec((1, MOD_ROWS, d), lambda i: (i // tpb, 0, 0))],
        out_specs=pl.BlockSpec((tm, d), lambda i: (i, 0)),
        out_shape=jax.ShapeDtypeStruct((n, d), F32),
        compiler_params=_cparams(("parallel",)),
        name="moe_combine",
    )(yk, r_g, res, mod16)


def _axial_rope(n_tok):
    pos = jnp.arange(n_tok)
    rows = (pos // GRID_W).astype(F32)
    cols = (pos % GRID_W).astype(F32)
    quarter = HEAD_DIM // 4
    inv_freq = ROPE_BASE ** (-jnp.arange(quarter, dtype=F32) / quarter)
    ar = rows[:, None] * inv_freq
    ac = cols[:, None] * inv_freq
    ang = jnp.concatenate([ar, ar, ac, ac], axis=-1)
    return jnp.cos(ang), jnp.sin(ang)


def _pick_tn(n):
    for tn in (1536, 1280, 1024, 768, 512, 256, 128):
        if n % tn == 0:
            return tn
    raise ValueError(n)


def _even_mixer(xa, mix_g, mod16, n_lat, w_in, w_out, conv_w, conv_b, a_log, dt_bias, d_skip, norm_g, sc_conv_w):
    b, l, d = xa.shape
    gn = SSD_GROUPS * SSD_STATE
    conv_dim = d + 2 * gn
    n_in = w_in.shape[1]
    n_pad = -(-n_in // 256) * 256
    o_dt = d + conv_dim
    w_perm = jnp.concatenate([w_in[:, :o_dt], w_in[:, o_dt + 2 * SSD_HEADS:], w_in[:, o_dt:o_dt + 2 * SSD_HEADS],
                              jnp.zeros((d, n_pad - n_in), w_in.dtype)], axis=1).astype(BF16)
    proj2 = proj_modulated(xa.reshape(b * l, d), mix_g, mod16, w_perm, l, n_lat, 0, 1, _pick_tn(n_pad))
    dtr = proj2[:, n_in - 2 * SSD_HEADS:n_in]
    dt = jax.nn.softplus(dtr.reshape(b, l, 2, SSD_HEADS) + dt_bias)
    a_neg = -jnp.exp(a_log)
    da = jnp.concatenate([dt, dt * a_neg], axis=-1)
    da = jnp.moveaxis(da, 2, 1)
    dat = jnp.swapaxes(da, 2, 3)
    xs_bf, bm, cm, sc_bf = even_prep(proj2, conv_w, conv_b, sc_conv_w, l, n_lat, d)
    xs_bf = xs_bf.reshape(b, l, d)
    bm = bm.reshape(b, l, gn)
    cm = cm.reshape(b, l, gn)
    bt = jnp.swapaxes(bm, 1, 2)
    yf = ssd_scan(xs_bf, bt, bm, cm, da, dat, n_lat, rev=False)
    post = (yf, proj2, jnp.repeat(d_skip, HEAD_DIM).reshape(1, d), norm_g.reshape(1, d))
    y_bf = ssd_scan(xs_bf, bt, bm, cm, da, dat, n_lat, rev=True, post=post)
    return matmul_residual([y_bf.reshape(b * l, d), sc_bf], w_out.astype(BF16), xa.reshape(b * l, d), mod16,
                           l, n_lat, 2).reshape(b, l, d)


def _odd_mixer(xa, mix_g, mod16, n_lat, w_in, w_out, na_qk_g, na_rel_bias, df_qk_g, df_lambda, df_subln_g,
               lambda_init, rope_tabs):
    b, l, d = xa.shape
    d_na = NA_HEADS * HEAD_DIM
    dq_w = DF_HEADS * 2 * HEAD_DIM
    n_ctx = l - n_lat
    reps = ODD_TN // HEAD_DIM
    ones = jnp.ones((ODD_TN,), F32)
    gains6 = jnp.stack([jnp.tile(na_qk_g[0], reps), jnp.tile(na_qk_g[1], reps), ones,
                        jnp.tile(df_qk_g[0], reps), jnp.tile(df_qk_g[1], reps), ones])[:, None, :]
    qkv = proj_odd(xa.reshape(b * l, d), mix_g, mod16, w_in.astype(BF16), gains6, *rope_tabs, l, n_lat, 0, 1)
    qkv = qkv.reshape(b, l, -1)
    lam_p = df_lambda.astype(F32)
    na_slabs = d_na // LANES
    df_slabs = dq_w // LANES
    subln = df_subln_g.reshape(1, LANES).astype(F32)
    na_lat = na_attention(qkv, na_bias_table(na_rel_bias), n_lat)
    na_ctx = flash_slabs(qkv, lam_p, subln, "pair", n_ctx, n_ctx, n_ctx, n_ctx, n_lat // n_ctx, n_lat // n_ctx,
                         0, na_slabs, 2 * na_slabs, na_slabs)
    tk = next(t for t in (1408, 1280, 1024, 768, 512, 256) if l % t == 0)
    dq0 = 3 * na_slabs
    df_lat = flash_slabs(qkv, lam_p, subln, "diff", 256, tk, n_lat, l, 0, 0, dq0, dq0 + df_slabs,
                         dq0 + 2 * df_slabs, df_slabs, lambda_init)
    df_ctx = flash_slabs(qkv, lam_p, subln, "diff", n_ctx, n_ctx, n_ctx, n_ctx, n_lat // n_ctx, n_lat // n_ctx,
                         dq0, dq0 + df_slabs, dq0 + 2 * df_slabs, df_slabs, lambda_init)
    na_o = jnp.concatenate([na_lat, na_ctx], axis=1)
    df_o = jnp.concatenate([df_lat, df_ctx], axis=1)
    parts = [na_o.reshape(b * l, d_na), df_o.reshape(b * l, dq_w)]
    return matmul_residual(parts, w_out.astype(BF16), xa.reshape(b * l, d), mod16, l, n_lat, 2).reshape(b, l, d)


def kernel(x, c, ctx, c_ctx, ada_w, ada_b, mix_norm_g, ffn_norm_g, router_w, router_b, moe_w_gu, moe_b_gu,
           moe_w_down, moe_b_down, ev_w_in, ev_w_out, ssd_conv_w, ssd_conv_b, ssd_a_log, ssd_dt_bias, ssd_d,
           ssd_norm_g, sc_conv_w, od_w_in, od_w_out, na_qk_g, na_rel_bias, df_qk_g, df_lambda, df_subln_g):
    b, s, d = x.shape
    n_ctx = ctx.shape[1]
    l = s + n_ctx
    depth = ada_w.shape[0]
    de = moe_w_down.shape[2]
    ne = moe_w_gu.shape[1]
    xa = jnp.concatenate([x, ctx], axis=1)
    cos_l, sin_l = _axial_rope(s)
    cos_t = jnp.concatenate([cos_l, jnp.ones((n_ctx, HEAD_DIM), F32)], axis=0)
    sin_t = jnp.concatenate([sin_l, jnp.zeros((n_ctx, HEAD_DIM), F32)], axis=0)
    even_q = ((jnp.arange(HEAD_DIM) // (HEAD_DIM // 4)) % 2 == 0)[None, :]
    rope_tabs = tuple(jnp.tile(t, (1, LANES // HEAD_DIM))
                      for t in (cos_t, jnp.where(even_q, -sin_t, 0.0), jnp.where(even_q, 0.0, sin_t)))
    cond = jnp.concatenate([jax.nn.silu(c), jax.nn.silu(c_ctx)[None, :]], axis=0)
    cond_pad = jnp.zeros((16, d), F32).at[:b + 1].set(cond).astype(BF16)
    wgu_all = moe_w_gu.reshape(depth * ne, d, 2 * de)
    bgu_all = moe_b_gu.reshape(depth * ne, 2 * de // GU_BLK, LANES, 2).swapaxes(-1, -2).reshape(depth * ne, 1, 2 * de)
    wd_all = moe_w_down.reshape(depth * ne, de, d)
    bd_all = moe_b_down.reshape(depth * ne, 1, d)

    for i in range(depth):
        j = i // 2
        mod = matmul(cond_pad, ada_w[i].astype(BF16), 16, 6 * d // 4)[:b + 1] + ada_b[i]
        mod6 = mod.reshape(b + 1, 6, d)
        pad2 = jnp.zeros((b, CTX_ROW0 - 6, d), F32)
        mod16 = jnp.concatenate([mod6[:b], pad2, jnp.broadcast_to(mod6[b], (b, 6, d)), pad2], axis=1)
        if i % 2 == 0:
            xa = _even_mixer(xa, mix_norm_g[i], mod16, s, ev_w_in[j], ev_w_out[j], ssd_conv_w[j], ssd_conv_b[j],
                             ssd_a_log[j], ssd_dt_bias[j], ssd_d[j], ssd_norm_g[j], sc_conv_w[j])
        else:
            lambda_init = 0.8 - 0.6 * math.exp(-0.3 * i)
            xa = _odd_mixer(xa, mix_norm_g[i], mod16, s, od_w_in[j], od_w_out[j], na_qk_g[j], na_rel_bias[j],
                            df_qk_g[j], df_lambda[j], df_subln_g[j], lambda_init, rope_tabs)
        xa = moe_ffn(xa.reshape(b * l, d), ffn_norm_g[i], mod16, l, s, router_w[i], router_b[i],
                     wgu_all, bgu_all, wd_all, bd_all, i * ne).reshape(b, l, d)
    return xa[:, :s]
```

```python
import functools
import math

import jax
import jax.numpy as jnp
from jax import lax
from jax.experimental import pallas as pl
from jax.experimental.pallas import tpu as pltpu

F32 = jnp.float32
BF16 = jnp.bfloat16

GRID_W = 64
HEAD_DIM = 64
EPS = 1e-6
SSD_HEADS = 16
SSD_GROUPS = 4
SSD_STATE = 128
NA_ROWS = 8
NA_COLS = 16
NA_HEADS = 8
DF_HEADS = 4
N_EXPERTS = 32
TOP_K = 4
SWIGLU_LIMIT = 7.0
SWIGLU_ALPHA = 1.702
ROPE_BASE = 10000.0

LANES = 128
SSD_Q = 128
MOE_TM = 512
ROUTER_TM = 256
VMEM_LIMIT = 56 * 1024 * 1024
NEG = -1e30


def _cparams(sem):
    return pltpu.CompilerParams(dimension_semantics=sem, vmem_limit_bytes=VMEM_LIMIT)


def _mm_kernel(a_ref, w_ref, o_ref):
    o_ref[...] = jnp.dot(a_ref[...], w_ref[...], preferred_element_type=F32).astype(o_ref.dtype)


def matmul(a, w, tm, tn, out_dtype=F32):
    m, k = a.shape
    n = w.shape[1]
    return pl.pallas_call(
        _mm_kernel,
        grid=(n // tn, m // tm),
        in_specs=[pl.BlockSpec((tm, k), lambda j, i: (i, 0)),
                  pl.BlockSpec((k, tn), lambda j, i: (0, j))],
        out_specs=pl.BlockSpec((tm, tn), lambda j, i: (i, j)),
        out_shape=jax.ShapeDtypeStruct((m, n), out_dtype),
        compiler_params=_cparams(("parallel", "parallel")),
        name="matmul",
    )(a, w)


MOD_ROWS = 16
CTX_ROW0 = 8
ODD_TN = 512


def _row_is_ctx(i, tm, tiles_per_batch, n_lat):
    t = i % tiles_per_batch
    row = t * tm + lax.broadcasted_iota(jnp.int32, (tm, 1), 0)
    return row >= n_lat


def _mod_row(mod, is_ctx, which):
    return jnp.where(is_ctx, mod[CTX_ROW0 + which:CTX_ROW0 + which + 1], mod[which:which + 1])


def _modulated_rows(x_ref, g_ref, mod_ref, is_ctx, sh_row, sc_row):
    x = x_ref[...]
    y = x * lax.rsqrt(jnp.mean(x * x, axis=-1, keepdims=True) + EPS) * g_ref[...]
    mod = mod_ref[0]
    return y * (1.0 + _mod_row(mod, is_ctx, sc_row)) + _mod_row(mod, is_ctx, sh_row)


def _proj_kernel(x_ref, g_ref, mod_ref, w_ref, o_ref, h_ref, *, tiles_per_batch, n_lat, sh_row, sc_row):
    i = pl.program_id(0)

    @pl.when(pl.program_id(1) == 0)
    def _():
        is_ctx = _row_is_ctx(i, x_ref.shape[0], tiles_per_batch, n_lat)
        h_ref[...] = _modulated_rows(x_ref, g_ref, mod_ref, is_ctx, sh_row, sc_row).astype(BF16)

    o_ref[...] = jnp.dot(h_ref[...], w_ref[...], preferred_element_type=F32)


def _group_rms(a, gsum, gain):
    outs = []
    half = gsum.shape[0]
    for hf in range(a.shape[1] // half):
        ah = a[:, hf * half:(hf + 1) * half]
        ms = jnp.dot((ah * ah).astype(BF16), gsum, preferred_element_type=F32) * (1.0 / HEAD_DIM)
        outs.append(ah * lax.rsqrt(ms + EPS))
    return jnp.concatenate(outs, axis=1) * gain


def _proj_odd_kernel(x_ref, g_ref, mod_ref, w_ref, gain_ref, gsum_ref, cos_ref, sina_ref, sinb_ref, o_ref, h_ref, *,
                     tiles_per_batch, n_lat, sh_row, sc_row):
    i = pl.program_id(0)
    j = pl.program_id(1)

    @pl.when(j == 0)
    def _():
        is_ctx = _row_is_ctx(i, x_ref.shape[0], tiles_per_batch, n_lat)
        h_ref[...] = _modulated_rows(x_ref, g_ref, mod_ref, is_ctx, sh_row, sc_row).astype(BF16)

    acc = jnp.dot(h_ref[...], w_ref[...], preferred_element_type=F32)
    is_norm = (j == 0) | (j == 1) | (j == 3) | (j == 4)
    is_rope = (j == 3) | (j == 4)

    @pl.when(jnp.logical_not(is_norm))
    def _():
        o_ref[...] = acc.astype(BF16)

    @pl.when(is_norm & jnp.logical_not(is_rope))
    def _():
        o_ref[...] = _group_rms(acc, gsum_ref[...], gain_ref[0]).astype(BF16)

    @pl.when(is_rope)
    def _():
        xn = _group_rms(acc, gsum_ref[...], gain_ref[0])
        reps = xn.shape[1] // LANES
        cos = jnp.concatenate([cos_ref[...]] * reps, axis=1)
        sina = jnp.concatenate([sina_ref[...]] * reps, axis=1)
        sinb = jnp.concatenate([sinb_ref[...]] * reps, axis=1)
        quarter = HEAD_DIM // 4
        up = pltpu.roll(xn, xn.shape[1] - quarter, axis=1)
        dn = pltpu.roll(xn, quarter, axis=1)
        o_ref[...] = (xn * cos + up * sina + dn * sinb).astype(BF16)


def _pick_rows(l):
    for tm in (768, 512, 256):
        if l % tm == 0:
            return tm
    raise ValueError(l)


def proj_modulated(x2, gain, mod16, w_bf, l, n_lat, sh_row, sc_row, tn):
    m, d = x2.shape
    n = w_bf.shape[1]
    tm = _pick_rows(l)
    tpb = l // tm
    return pl.pallas_call(
        functools.partial(_proj_kernel, tiles_per_batch=tpb, n_lat=n_lat, sh_row=sh_row, sc_row=sc_row),
        grid=(m // tm, n // tn),
        in_specs=[pl.BlockSpec((tm, d), lambda i, j: (i, 0)),
                  pl.BlockSpec((1, d), lambda i, j: (0, 0)),
                  pl.BlockSpec((1, MOD_ROWS, d), lambda i, j: (i // tpb, 0, 0)),
                  pl.BlockSpec((d, tn), lambda i, j: (0, j))],
        out_specs=pl.BlockSpec((tm, tn), lambda i, j: (i, j)),
        out_shape=jax.ShapeDtypeStruct((m, n), F32),
        scratch_shapes=[pltpu.VMEM((tm, d), BF16)],
        compiler_params=_cparams(("parallel", "arbitrary")),
        name="proj_even",
    )(x2, gain.reshape(1, d), mod16, w_bf)


def proj_odd(x2, gain, mod16, w_bf, gains6, cos2, sina2, sinb2, l, n_lat, sh_row, sc_row):
    m, d = x2.shape
    n = w_bf.shape[1]
    tn = ODD_TN
    tm = _pick_rows(l)
    tpb = l // tm
    half = 256
    gi = jnp.arange(half) // HEAD_DIM
    gsum = (gi[:, None] == gi[None, :]).astype(BF16)
    tab = pl.BlockSpec((tm, LANES), lambda i, j: (i % tpb, 0))
    return pl.pallas_call(
        functools.partial(_proj_odd_kernel, tiles_per_batch=tpb, n_lat=n_lat, sh_row=sh_row, sc_row=sc_row),
        grid=(m // tm, n // tn),
        in_specs=[pl.BlockSpec((tm, d), lambda i, j: (i, 0)),
                  pl.BlockSpec((1, d), lambda i, j: (0, 0)),
                  pl.BlockSpec((1, MOD_ROWS, d), lambda i, j: (i // tpb, 0, 0)),
                  pl.BlockSpec((d, tn), lambda i, j: (0, j)),
                  pl.BlockSpec((1, 1, tn), lambda i, j: (j, 0, 0)),
                  pl.BlockSpec((half, half), lambda i, j: (0, 0)),
                  tab, tab, tab],
        out_specs=pl.BlockSpec((tm, tn), lambda i, j: (i, j)),
        out_shape=jax.ShapeDtypeStruct((m, n), BF16),
        scratch_shapes=[pltpu.VMEM((tm, d), BF16)],
        compiler_params=_cparams(("parallel", "arbitrary")),
        name="proj_odd",
    )(x2, gain.reshape(1, d), mod16, w_bf, gains6, gsum, cos2, sina2, sinb2)


def _mm_res_kernel(*refs, n_parts, tiles_per_batch, n_lat, gate_row):
    a_refs = refs[:n_parts]
    w_ref, r_ref, mod_ref, o_ref = refs[n_parts:]
    is_ctx = _row_is_ctx(pl.program_id(0), r_ref.shape[0], tiles_per_batch, n_lat)
    gate = _mod_row(mod_ref[0], is_ctx, gate_row)
    acc = None
    k0 = 0
    for a_ref in a_refs:
        k1 = k0 + a_ref.shape[1]
        t = jnp.dot(a_ref[...], w_ref[k0:k1, :], preferred_element_type=F32)
        acc = t if acc is None else acc + t
        k0 = k1
    o_ref[...] = r_ref[...] + gate * acc


def matmul_residual(parts, w_bf, res, mod16, l, n_lat, gate_row):
    m = res.shape[0]
    k, d = w_bf.shape
    tm = _pick_rows(l)
    tpb = l // tm
    return pl.pallas_call(
        functools.partial(_mm_res_kernel, n_parts=len(parts), tiles_per_batch=tpb, n_lat=n_lat, gate_row=gate_row),
        grid=(m // tm,),
        in_specs=[pl.BlockSpec((tm, a.shape[1]), lambda i: (i, 0)) for a in parts]
                 + [pl.BlockSpec((k, d), lambda i: (0, 0)),
                    pl.BlockSpec((tm, d), lambda i: (i, 0)),
                    pl.BlockSpec((1, MOD_ROWS, d), lambda i: (i // tpb, 0, 0))],
        out_specs=pl.BlockSpec((tm, d), lambda i: (i, 0)),
        out_shape=jax.ShapeDtypeStruct((m, d), F32),
        compiler_params=_cparams(("parallel",)),
        name="out_proj",
    )(*parts, w_bf, res, mod16)


EV_TM = 256
HALO = 8


def _even_prep_kernel(xa_ref, xa_p, xa_n, xb_ref, xb_p, xb_n, gb_ref, gc_ref, gc_p, gc_n, gh_ref, gh_p, gh_n,
                      cw_ref, cb_ref, sw_ref, xs_ref, bm_ref, cm_ref, sc_ref, scr, *, tiles_per_batch, n_lat, l):
    tm = xa_ref.shape[0]
    r0 = (pl.program_id(0) % tiles_per_batch) * tm
    has_prev = (r0 != 0) & (r0 != n_lat)
    has_next = (r0 + tm != n_lat) & (r0 + tm != l)

    def conv(main, prev, nxt, w):
        width = w.shape[0]
        scr[0:HALO] = jnp.where(has_prev, prev, 0.0)
        scr[HALO:HALO + tm] = main
        scr[HALO + tm:2 * HALO + tm] = jnp.where(has_next, nxt, 0.0)
        out = None
        for kk in range(width):
            o = HALO + kk - width // 2
            t = scr[o:o + tm] * w[kk:kk + 1]
            out = t if out is None else out + t
        return out

    d = xa_ref.shape[1]
    cw = cw_ref[...]
    cb = cb_ref[...]
    xs = conv(xa_ref[...], xa_p[...], xa_n[...], cw[:, :d]) + cb[:, :d]
    xs_ref[...] = (xs * jax.nn.sigmoid(xs)).astype(BF16)
    bc = conv(xb_ref[...], xb_p[...], xb_n[...], cw[:, d:]) + cb[:, d:]
    bc = (bc * jax.nn.sigmoid(bc)).astype(BF16)
    gn = bm_ref.shape[1]
    bm_ref[...] = bc[:, :gn]
    cm_ref[...] = bc[:, gn:]
    sc = gb_ref[...] * conv(gc_ref[...] * gh_ref[...], gc_p[...] * gh_p[...], gc_n[...] * gh_n[...], sw_ref[...])
    sc_ref[...] = sc.astype(BF16)


def even_prep(proj2, conv_w, conv_b, sc_conv_w, l, n_lat, d):
    m = proj2.shape[0]
    tm = EV_TM
    tpb = l // tm
    nhb = m // HALO
    gn = SSD_GROUPS * SSD_STATE

    def main(cblk):
        return pl.BlockSpec((tm, d), lambda i: (i, cblk))

    def prev(cblk):
        return pl.BlockSpec((HALO, d), lambda i: (jnp.maximum(i * (tm // HALO) - 1, 0), cblk))

    def nxt(cblk):
        return pl.BlockSpec((HALO, d), lambda i: (jnp.minimum((i + 1) * (tm // HALO), nhb - 1), cblk))

    full = lambda a: pl.BlockSpec(a.shape, lambda i: (0, 0))
    cb2 = conv_b.reshape(1, -1)
    return pl.pallas_call(
        functools.partial(_even_prep_kernel, tiles_per_batch=tpb, n_lat=n_lat, l=l),
        grid=(m // tm,),
        in_specs=[main(1), prev(1), nxt(1), main(2), prev(2), nxt(2), main(3),
                  main(4), prev(4), nxt(4), main(5), prev(5), nxt(5),
                  full(conv_w), full(cb2), full(sc_conv_w)],
        out_specs=[pl.BlockSpec((tm, d), lambda i: (i, 0)), pl.BlockSpec((tm, gn), lambda i: (i, 0)),
                   pl.BlockSpec((tm, gn), lambda i: (i, 0)), pl.BlockSpec((tm, d), lambda i: (i, 0))],
        out_shape=[jax.ShapeDtypeStruct((m, d), BF16), jax.ShapeDtypeStruct((m, gn), BF16),
                   jax.ShapeDtypeStruct((m, gn), BF16), jax.ShapeDtypeStruct((m, d), BF16)],
        scratch_shapes=[pltpu.VMEM((tm + 2 * HALO, d), F32)],
        compiler_params=_cparams(("parallel",)),
        name="even_prep",
    )(proj2, proj2, proj2, proj2, proj2, proj2, proj2, proj2, proj2, proj2, proj2, proj2, proj2,
      conv_w, cb2, sc_conv_w)


GU_BLK = 2 * LANES


def _gate_up_perm():
    src = jnp.arange(GU_BLK)
    dst = jnp.where(src % 2 == 0, src // 2, LANES + src // 2)
    return (dst[:, None] == jnp.arange(GU_BLK)[None, :]).astype(BF16)


def _deinterleave_into(w_ref, perm, o_ref):
    for blk in range(o_ref.shape[1] // GU_BLK):
        cs = slice(blk * GU_BLK, (blk + 1) * GU_BLK)
        o_ref[:, cs] = jnp.dot(w_ref[0, :, cs].astype(BF16), perm, preferred_element_type=F32).astype(BF16)


def _split3_dot(tri, a, dims):
    a1 = a.astype(BF16)
    r1 = a - a1.astype(F32)
    a2 = r1.astype(BF16)
    a3 = (r1 - a2.astype(F32)).astype(BF16)
    out = None
    for piece in (a1, a2, a3):
        if dims == "tri_a":
            t = jnp.dot(tri, piece, preferred_element_type=F32)
        else:
            t = jnp.dot(piece, tri, preferred_element_type=F32)
        out = t if out is None else out + t
    return out


def _ssd_kernel(*refs, rev, post):
    if post:
        x_ref, bt_ref, b_ref, c_ref, da_ref, dat_ref, yo_ref, z_ref, dsk_ref, ng_ref, y_ref, st_ref, yacc = refs
    else:
        x_ref, bt_ref, b_ref, c_ref, da_ref, dat_ref, y_ref, st_ref = refs
    q = SSD_Q
    step = pl.program_id(1)

    @pl.when(step == 0)
    def _():
        st_ref[...] = jnp.zeros_like(st_ref)

    da = da_ref[0, 0]
    dat = dat_ref[0, 0]
    a_c = da[:, SSD_HEADS:2 * SSD_HEADS]
    dt_r = dat[0:SSD_HEADS]
    a_r = dat[SSD_HEADS:2 * SSD_HEADS]
    ri = lax.broadcasted_iota(jnp.int32, (q, q), 0)
    ci = lax.broadcasted_iota(jnp.int32, (q, q), 1)
    tri = (ci <= ri).astype(BF16)
    tri_t = (ri <= ci).astype(BF16)
    cum_c = _split3_dot(tri, a_c, "tri_a")
    cum_r = _split3_dot(tri_t, a_r, "a_tri")
    tot_r = cum_r[:, q - 1:q]
    if rev:
        pos_c = cum_c - a_c
        pos_r = cum_r - a_r
        mask = ci >= ri
    else:
        pos_c = cum_c
        pos_r = cum_r
        mask = ri >= ci
    lane = lax.broadcasted_iota(jnp.int32, (q, LANES), 1)
    lane_n = lax.broadcasted_iota(jnp.int32, (SSD_STATE, LANES), 1)
    heads_per_group = SSD_HEADS // SSD_GROUPS

    for g in range(SSD_GROUPS):
        gs = slice(g * SSD_STATE, (g + 1) * SSD_STATE)
        bg = b_ref[0, :, gs]
        cg = c_ref[0, :, gs]
        btg = bt_ref[0, gs, :].astype(F32)
        cb = lax.dot_general(cg, bg, (((1,), (1,)), ((), ())), preferred_element_type=F32)
        cg32 = cg.astype(F32)
        for pp in range(heads_per_group // 2):
            p = g * (heads_per_group // 2) + pp
            xp = x_ref[0, :, p * LANES:(p + 1) * LANES]
            st = st_ref[p]
            rhs = jnp.concatenate([xp, st.astype(BF16)], axis=0)
            ys, ds, decs = [], [], []
            for hh in range(2):
                h = 2 * p + hh
                colb = jnp.broadcast_to(pos_c[:, h:h + 1], (q, LANES))
                row = pos_r[h:h + 1, :]
                dt_row = dt_r[h:h + 1, :]
                tot = tot_r[h:h + 1, :]
                if rev:
                    seg = row - colb
                    coff = jnp.exp(tot - colb)
                    w_row = dt_row * jnp.exp(row)
                else:
                    seg = colb - row
                    coff = jnp.exp(colb)
                    w_row = dt_row * jnp.exp(tot - row)
                decay = jnp.exp(jnp.where(mask, seg, NEG))
                m_h = (cb * decay * dt_row).astype(BF16)
                c_h = (cg32 * coff).astype(BF16)
                lhs = jnp.concatenate([m_h, c_h], axis=1)
                ys.append(jnp.dot(lhs, rhs, preferred_element_type=F32))
                btw = (btg * w_row).astype(BF16)
                ds.append(jnp.dot(btw, xp, preferred_element_type=F32))
                decs.append(jnp.exp(tot))
            y_pair = jnp.where(lane < HEAD_DIM, ys[0], ys[1])
            if post:
                yacc[:, p * LANES:(p + 1) * LANES] = y_pair
            else:
                y_ref[0, :, p * LANES:(p + 1) * LANES] = y_pair
            st_ref[p] = jnp.where(lane_n < HEAD_DIM, decs[0] * st + ds[0], decs[1] * st + ds[1])

    if post:
        z = z_ref[...]
        y = (yacc[...] + yo_ref[0] + dsk_ref[...] * x_ref[0].astype(F32)) * (z * jax.nn.sigmoid(z))
        y = y * lax.rsqrt(jnp.mean(y * y, axis=-1, keepdims=True) + EPS) * ng_ref[...]
        y_ref[0] = y.astype(y_ref.dtype)


def ssd_scan(xs, bt, bm, cm, da, dat, n_lat, rev, post=None):
    b, l, d = xs.shape
    nch = l // SSD_Q
    nlat = n_lat // SSD_Q
    nctx = nch - nlat
    d_idx = 1 if rev else 0
    if rev:
        def chunk(j):
            return nch - 1 - j
    else:
        def chunk(j):
            return jnp.where(j < nctx, nlat + j, j - nctx)
    gn = SSD_GROUPS * SSD_STATE
    tok = pl.BlockSpec((1, SSD_Q, d), lambda i, j: (i, chunk(j), 0))
    in_specs = [tok,
                pl.BlockSpec((1, gn, SSD_Q), lambda i, j: (i, 0, chunk(j))),
                pl.BlockSpec((1, SSD_Q, gn), lambda i, j: (i, chunk(j), 0)),
                pl.BlockSpec((1, SSD_Q, gn), lambda i, j: (i, chunk(j), 0)),
                pl.BlockSpec((1, 1, SSD_Q, 2 * SSD_HEADS), lambda i, j: (i, d_idx, chunk(j), 0)),
                pl.BlockSpec((1, 1, 2 * SSD_HEADS, SSD_Q), lambda i, j: (i, d_idx, 0, chunk(j)))]
    scratch = [pltpu.VMEM((SSD_HEADS // 2, SSD_STATE, LANES), F32)]
    args = (xs, bt, bm, cm, da, dat)
    if post is not None:
        row = pl.BlockSpec((1, d), lambda i, j: (0, 0))
        in_specs += [tok, pl.BlockSpec((SSD_Q, d), lambda i, j: (i * nch + chunk(j), 0)), row, row]
        scratch.append(pltpu.VMEM((SSD_Q, d), F32))
        args += tuple(post)
    return pl.pallas_call(
        functools.partial(_ssd_kernel, rev=rev, post=post is not None),
        grid=(b, nch),
        in_specs=in_specs,
        out_specs=tok,
        out_shape=jax.ShapeDtypeStruct((b, l, d), F32 if post is None else BF16),
        scratch_shapes=scratch,
        compiler_params=_cparams(("parallel", "arbitrary")),
        name="ssd_bwd" if rev else "ssd_fwd",
    )(*args)


def _stack_halves(qv):
    lane = lax.broadcasted_iota(jnp.int32, qv.shape, 1)
    zero = jnp.zeros_like(qv)
    q1 = jnp.where(lane < HEAD_DIM, qv, zero)
    q2 = jnp.where(lane >= HEAD_DIM, qv, zero)
    return jnp.concatenate([q1, q2], axis=0) * jnp.asarray(HEAD_DIM ** -0.5, qv.dtype)


def _flash_kernel(lam_ref, g_ref, q_ref, k_ref, v_ref, o_ref, *, mode, tk, lambda_init):
    tq = q_ref.shape[1]
    nk = k_ref.shape[1] // tk
    nt = tk // LANES
    qs = _stack_halves(q_ref[0])

    def scores(u):
        return lax.dot_general(qs, k_ref[0, u * tk:(u + 1) * tk, :], (((1,), (1,)), ((), ())),
                               preferred_element_type=F32)

    m_old = jnp.full((2 * tq, LANES), -jnp.inf, F32)
    l_run = jnp.zeros((2 * tq, LANES), F32)
    acc = jnp.zeros((2 * tq, LANES), F32)
    s_next = scores(0)
    for u in range(nk):
        s = s_next
        if u + 1 < nk:
            s_next = scores(u + 1)
        tiles = [s[:, t * LANES:(t + 1) * LANES] for t in range(nt)]
        smax = tiles[0]
        for t in tiles[1:]:
            smax = jnp.maximum(smax, t)
        m_new = jnp.maximum(m_old, jnp.max(smax, axis=-1, keepdims=True))
        alpha = jnp.exp(m_old - m_new)
        ps = [jnp.exp(t - m_new) for t in tiles]
        psum = ps[0]
        for t in ps[1:]:
            psum = psum + t
        l_run = alpha * l_run + jnp.sum(psum, axis=-1, keepdims=True)
        p = jnp.concatenate([t.astype(BF16) for t in ps], axis=1)
        acc = alpha * acc + jnp.dot(p, v_ref[0, u * tk:(u + 1) * tk, :], preferred_element_type=F32)
        m_old = m_new
    o = acc / l_run
    o1, o2 = o[:tq], o[tq:]
    if mode == "diff":
        lp = lam_ref[...]
        s01 = jnp.sum(lp[0:1] * lp[1:2], axis=-1, keepdims=True)
        s23 = jnp.sum(lp[2:3] * lp[3:4], axis=-1, keepdims=True)
        lam = jnp.exp(s01) - jnp.exp(s23) + lambda_init
        out = o1 - lam * o2
        out = out * lax.rsqrt(jnp.mean(out * out, axis=-1, keepdims=True) + EPS) * g_ref[...] * (1.0 - lambda_init)
    else:
        lane = lax.broadcasted_iota(jnp.int32, o1.shape, 1)
        out = jnp.where(lane < HEAD_DIM, o1, o2)
    o_ref[0] = out.astype(o_ref.dtype)


def flash_slabs(qkv, lam_p, subln_g, mode, tq, tk, lq, lk, q_blk0, k_blk, q_slab0, k_slab0, v_slab0, n_slabs,
                lambda_init=0.0):
    b = qkv.shape[0]
    return pl.pallas_call(
        functools.partial(_flash_kernel, mode=mode, tk=tk, lambda_init=lambda_init),
        grid=(b, n_slabs, lq // tq),
        in_specs=[pl.BlockSpec(lam_p.shape, lambda i, h, j: (0, 0)),
                  pl.BlockSpec((1, LANES), lambda i, h, j: (0, 0)),
                  pl.BlockSpec((1, tq, LANES), lambda i, h, j: (i, q_blk0 + j, q_slab0 + h)),
                  pl.BlockSpec((1, lk, LANES), lambda i, h, j: (i, k_blk, k_slab0 + h)),
                  pl.BlockSpec((1, lk, LANES), lambda i, h, j: (i, k_blk, v_slab0 + h))],
        out_specs=pl.BlockSpec((1, tq, LANES), lambda i, h, j: (i, j, h)),
        out_shape=jax.ShapeDtypeStruct((b, lq, n_slabs * LANES), BF16),
        compiler_params=_cparams(("parallel", "parallel", "arbitrary")),
        name="flash_" + mode,
    )(lam_p, subln_g, qkv, qkv, qkv)


NA_RB = 8
NA_BLK = NA_RB * GRID_W
NA_WIN = NA_ROWS * GRID_W


def _na_kernel(q_ref, kp_ref, kc_ref, kn_ref, vp_ref, vc_ref, vn_ref, kx_ref, vx_ref, bias_ref, o_ref,
               kbuf, vbuf, *, rows):
    rb = pl.program_id(1)
    kbuf[0:NA_BLK] = kp_ref[0]
    kbuf[NA_BLK:2 * NA_BLK] = kc_ref[0]
    kbuf[2 * NA_BLK:3 * NA_BLK] = kn_ref[0]
    vbuf[0:NA_BLK] = vp_ref[0]
    vbuf[NA_BLK:2 * NA_BLK] = vc_ref[0]
    vbuf[2 * NA_BLK:3 * NA_BLK] = vn_ref[0]
    npairs = q_ref.shape[2] // LANES
    lane = lax.broadcasted_iota(jnp.int32, (GRID_W, LANES), 1)

    def row_body(rl, carry):
        r = rb * NA_RB + rl
        r_start = jnp.clip(r - NA_ROWS // 2, 0, rows - NA_ROWS)
        off = r_start - (rb * NA_RB - NA_RB)
        di0 = r_start - r + NA_ROWS - 1
        tok0 = pl.multiple_of(off * GRID_W, GRID_W)
        q0 = pl.multiple_of(rl * GRID_W, GRID_W)
        for p in range(npairs):
            ls = slice(p * LANES, (p + 1) * LANES)
            qs = _stack_halves(q_ref[0, pl.ds(q0, GRID_W), ls])
            kw = kbuf[pl.ds(tok0, NA_WIN), ls]
            vw = vbuf[pl.ds(tok0, NA_WIN), ls]
            s_nb = lax.dot_general(qs, kw, (((1,), (1,)), ((), ())), preferred_element_type=F32)
            bias = jnp.concatenate([bias_ref[p, di0 + 2 * j] for j in range(NA_ROWS // 2)], axis=1)
            s_nb = s_nb + bias
            s_cx = lax.dot_general(qs, kx_ref[0, :, ls], (((1,), (1,)), ((), ())), preferred_element_type=F32)
            m = jnp.maximum(jnp.max(s_nb, axis=-1, keepdims=True), jnp.max(s_cx, axis=-1, keepdims=True))
            p_nb = jnp.exp(s_nb - m)
            p_cx = jnp.exp(s_cx - m)
            l = jnp.sum(p_nb, axis=-1, keepdims=True) + jnp.sum(p_cx, axis=-1, keepdims=True)
            o = (jnp.dot(p_nb.astype(BF16), vw, preferred_element_type=F32)
                 + jnp.dot(p_cx.astype(BF16), vx_ref[0, :, ls], preferred_element_type=F32)) / l
            o_ref[0, pl.ds(q0, GRID_W), ls] = jnp.where(lane < HEAD_DIM, o[:GRID_W], o[GRID_W:]).astype(o_ref.dtype)
        return carry

    lax.fori_loop(0, NA_RB, row_body, 0, unroll=True)


def na_bias_table(rel_bias):
    cols = jnp.arange(GRID_W)
    c_start = jnp.clip(cols - NA_COLS // 2, 0, GRID_W - NA_COLS)
    kc = jnp.arange(GRID_W)
    valid = (kc[None, :] >= c_start[:, None]) & (kc[None, :] < c_start[:, None] + NA_COLS)
    idx = jnp.clip(kc[None, :] - cols[:, None] + NA_COLS - 1, 0, 2 * NA_COLS - 2)
    t = jnp.where(valid[None, None], rel_bias[:, :, idx], NEG)
    t2 = jnp.concatenate([t[:, :-1], t[:, 1:]], axis=-1)
    nh, nd = t2.shape[0], t2.shape[1]
    t2 = t2.reshape(nh // 2, 2, nd, GRID_W, LANES).transpose(0, 2, 1, 3, 4)
    return t2.reshape(nh // 2, nd, 2 * GRID_W, LANES).astype(F32)


def na_attention(qkv, bias_tab, n_lat):
    b, l, _ = qkv.shape
    w = NA_HEADS * HEAD_DIM
    rows = n_lat // GRID_W
    nrb = rows // NA_RB
    n_ctx = l - n_lat
    ctx_blk = n_lat // n_ctx

    def blk(step, stream):
        if step < 0:
            return pl.BlockSpec((1, NA_BLK, w), lambda i, j: (i, jnp.maximum(j - 1, 0), stream))
        if step > 0:
            return pl.BlockSpec((1, NA_BLK, w), lambda i, j: (i, jnp.minimum(j + 1, nrb - 1), stream))
        return pl.BlockSpec((1, NA_BLK, w), lambda i, j: (i, j, stream))

    def ctx(stream):
        return pl.BlockSpec((1, n_ctx, w), lambda i, j: (i, ctx_blk, stream))

    return pl.pallas_call(
        functools.partial(_na_kernel, rows=rows),
        grid=(b, nrb),
        in_specs=[blk(0, 0), blk(-1, 1), blk(0, 1), blk(1, 1), blk(-1, 2), blk(0, 2), blk(1, 2), ctx(1), ctx(2),
                  pl.BlockSpec(bias_tab.shape, lambda i, j: (0, 0, 0, 0))],
        out_specs=pl.BlockSpec((1, NA_BLK, w), lambda i, j: (i, j, 0)),
        out_shape=jax.ShapeDtypeStruct((b, n_lat, w), BF16),
        scratch_shapes=[pltpu.VMEM((3 * NA_BLK, w), BF16), pltpu.VMEM((3 * NA_BLK, w), BF16)],
        compiler_params=_cparams(("parallel", "parallel")),
        name="na_attention",
    )(qkv, qkv, qkv, qkv, qkv, qkv, qkv, qkv, qkv, bias_tab)


def _router_kernel(x_ref, g_ref, sh_ref, sc_ref, w_ref, b_ref, h_ref, ri_ref, rg_ref, cnt_ref, carry_ref):
    step = pl.program_id(0)

    @pl.when(step == 0)
    def _():
        carry_ref[...] = jnp.zeros_like(carry_ref)

    tm = x_ref.shape[0]
    x = x_ref[...]
    y = x * lax.rsqrt(jnp.mean(x * x, axis=-1, keepdims=True) + EPS) * g_ref[...]
    h = y * (1.0 + sc_ref[0]) + sh_ref[0]
    h_ref[...] = h.astype(BF16)
    w = w_ref[...]
    h1 = h.astype(BF16)
    h2 = (h - h1.astype(F32)).astype(BF16)
    w1 = w.astype(BF16)
    w2 = (w - w1.astype(F32)).astype(BF16)
    logits = (jnp.dot(h1, w1, preferred_element_type=F32) + jnp.dot(h2, w1, preferred_element_type=F32)
              + jnp.dot(h1, w2, preferred_element_type=F32)) + b_ref[...]
    lane = lax.broadcasted_iota(jnp.int32, (tm, LANES), 1)
    work = logits
    tops, idxs, hots = [], [], []
    for _ in range(TOP_K):
        mx = jnp.max(work, axis=-1, keepdims=True)
        ix = jnp.min(jnp.where(work == mx, lane, LANES), axis=-1, keepdims=True)
        hot = lane == ix
        work = jnp.where(hot, -jnp.inf, work)
        tops.append(mx)
        idxs.append(ix)
        hots.append(hot)
    es = [jnp.exp(t - tops[0]) for t in tops]
    den = es[0] + es[1] + es[2] + es[3]
    multi = jnp.zeros((tm, LANES), F32)
    for hot in hots:
        multi = multi + hot.astype(F32)
    ri = lax.broadcasted_iota(jnp.int32, (tm, tm), 0)
    ci = lax.broadcasted_iota(jnp.int32, (tm, tm), 1)
    tri = (ci < ri).astype(BF16)
    cum = jnp.dot(tri, multi.astype(BF16), preferred_element_type=F32) + carry_ref[...]
    carry_new = carry_ref[...] + jnp.sum(multi, axis=0, keepdims=True)
    carry_ref[...] = carry_new
    cnt_ref[...] = carry_new
    out_i = jnp.zeros((tm, LANES), jnp.int32)
    out_g = jnp.zeros((tm, LANES), F32)
    for kk in range(TOP_K):
        rank = jnp.sum(jnp.where(hots[kk], cum, 0.0), axis=-1, keepdims=True).astype(jnp.int32)
        out_i = jnp.where(lane == kk, idxs[kk], out_i)
        out_i = jnp.where(lane == TOP_K + kk, rank, out_i)
        out_g = jnp.where(lane == kk, es[kk] / den, out_g)
    ri_ref[...] = out_i
    rg_ref[...] = out_g


def moe_router(x2, gain, modarr, sh_row, sc_row, n_lat_tiles, tiles_per_batch, router_w, router_b):
    n, d = x2.shape
    tm = ROUTER_TM
    w_pad = jnp.zeros((d, LANES), F32).at[:, :N_EXPERTS].set(router_w)
    b_pad = jnp.full((1, LANES), NEG, F32).at[0, :N_EXPERTS].set(router_b)

    def mod_idx(which):
        def f(i):
            bidx = i // tiles_per_batch
            is_ctx = (i % tiles_per_batch) >= n_lat_tiles
            return (bidx * MOD_ROWS + is_ctx.astype(jnp.int32) * CTX_ROW0 + which, 0, 0)
        return f

    return pl.pallas_call(
        _router_kernel,
        grid=(n // tm,),
        in_specs=[pl.BlockSpec((tm, d), lambda i: (i, 0)),
                  pl.BlockSpec((1, d), lambda i: (0, 0)),
                  pl.BlockSpec((1, 1, d), mod_idx(sh_row)),
                  pl.BlockSpec((1, 1, d), mod_idx(sc_row)),
                  pl.BlockSpec((d, LANES), lambda i: (0, 0)),
                  pl.BlockSpec((1, LANES), lambda i: (0, 0))],
        out_specs=[pl.BlockSpec((tm, d), lambda i: (i, 0)),
                   pl.BlockSpec((tm, LANES), lambda i: (i, 0)),
                   pl.BlockSpec((tm, LANES), lambda i: (i, 0)),
                   pl.BlockSpec((1, LANES), lambda i: (0, 0))],
        out_shape=[jax.ShapeDtypeStruct((n, d), BF16),
                   jax.ShapeDtypeStruct((n, LANES), jnp.int32),
                   jax.ShapeDtypeStruct((n, LANES), F32),
                   jax.ShapeDtypeStruct((1, LANES), F32)],
        scratch_shapes=[pltpu.VMEM((1, LANES), F32)],
        compiler_params=_cparams(("arbitrary",)),
        name="moe_router",
    )(x2, gain.reshape(1, d), modarr, modarr, w_pad, b_pad)


def _expert_kernel(be_ref, nu_ref, x_ref, wgu_ref, bgu_ref, wd_ref, bd_ref, perm_ref, o_ref, wgu_bf, wd_bf):
    j = pl.program_id(0)
    used = j < nu_ref[0]
    changed = (j == 0) | (be_ref[j] != be_ref[jnp.maximum(j - 1, 0)])

    @pl.when(used & changed)
    def _():
        wd_bf[...] = wd_ref[0].astype(BF16)
        _deinterleave_into(wgu_ref, perm_ref[...], wgu_bf)

    @pl.when(used)
    def _():
        gu = jnp.dot(x_ref[...], wgu_bf[...], preferred_element_type=F32) + bgu_ref[0]
        acts = []
        for blk in range(gu.shape[1] // GU_BLK):
            gate = jnp.minimum(gu[:, blk * GU_BLK:blk * GU_BLK + LANES], SWIGLU_LIMIT)
            up = jnp.clip(gu[:, blk * GU_BLK + LANES:(blk + 1) * GU_BLK], -SWIGLU_LIMIT, SWIGLU_LIMIT)
            acts.append((gate * jax.nn.sigmoid(SWIGLU_ALPHA * gate) * (up + 1.0)).astype(BF16))
        act = jnp.concatenate(acts, axis=1)
        o_ref[...] = (jnp.dot(act, wd_bf[...], preferred_element_type=F32) + bd_ref[0]).astype(o_ref.dtype)

    @pl.when(jnp.logical_not(used))
    def _():
        o_ref[...] = jnp.zeros_like(o_ref)


def moe_experts(xs, block_expert, n_used, wgu, bgu, wd, bd, e_off):
    cap, d = xs.shape
    de = wd.shape[1]
    nblk = cap // MOE_TM
    wmap = lambda j, be, nu: (e_off + be[j], 0, 0)
    grid_spec = pltpu.PrefetchScalarGridSpec(
        num_scalar_prefetch=2,
        grid=(nblk,),
        in_specs=[pl.BlockSpec((MOE_TM, d), lambda j, be, nu: (j, 0)),
                  pl.BlockSpec((1, d, 2 * de), wmap),
                  pl.BlockSpec((1, 1, 2 * de), wmap),
                  pl.BlockSpec((1, de, d), wmap),
                  pl.BlockSpec((1, 1, d), wmap),
                  pl.BlockSpec((GU_BLK, GU_BLK), lambda j, be, nu: (0, 0))],
        out_specs=pl.BlockSpec((MOE_TM, d), lambda j, be, nu: (j, 0)),
        scratch_shapes=[pltpu.VMEM((d, 2 * de), BF16), pltpu.VMEM((de, d), BF16)],
    )
    return pl.pallas_call(
        _expert_kernel,
        grid_spec=grid_spec,
        out_shape=jax.ShapeDtypeStruct((cap, d), BF16),
        compiler_params=_cparams(("arbitrary",)),
        name="moe_experts",
    )(block_expert, n_used, xs, wgu, bgu, wd, bd, _gate_up_perm())


def moe_ffn(x2, gain, mod16, l, n_lat, router_w, router_b, wgu, bgu, wd, bd, e_off):
    n, d = x2.shape
    sh_row, sc_row, gate_row = 3, 4, 5
    h_bf, r_i, r_g, cnt = moe_router(x2, gain, mod16.reshape(-1, 1, d), sh_row, sc_row, n_lat // ROUTER_TM,
                                     l // ROUTER_TM, router_w, router_b)
    top_idx = r_i[:, :TOP_K]
    rank = r_i[:, TOP_K:2 * TOP_K]
    gates = r_g[:, :TOP_K]
    counts = cnt[0, :N_EXPERTS].astype(jnp.int32)
    padded = (counts + MOE_TM - 1) // MOE_TM * MOE_TM
    pad_end = jnp.cumsum(padded)
    pad_start = pad_end - padded
    dest = (pad_start[top_idx] + rank).T
    nblk = -(-(n * TOP_K + N_EXPERTS * (MOE_TM - 1)) // MOE_TM)
    cap = nblk * MOE_TM
    tok = jnp.broadcast_to(jnp.arange(n, dtype=jnp.int32)[None, :], (TOP_K, n))
    blk_start = jnp.arange(nblk, dtype=jnp.int32)[:, None] * MOE_TM
    block_expert = jnp.minimum(jnp.sum((pad_end[None, :] <= blk_start).astype(jnp.int32), axis=1), N_EXPERTS - 1)
    n_used = (pad_end[-1:] // MOE_TM).astype(jnp.int32)
    _, sorted_tok = lax.sort_key_val(dest.reshape(-1), tok.reshape(-1))
    grp_start = jnp.cumsum(counts) - counts
    slot_e = jnp.repeat(block_expert, MOE_TM)
    rank_s = jnp.arange(cap, dtype=jnp.int32) - pad_start[slot_e]
    src = jnp.clip(grp_start[slot_e] + rank_s, 0, n * TOP_K - 1)
    slot_tok = jnp.where(rank_s < counts[slot_e], sorted_tok[src], 0)
    xs = h_bf[slot_tok]
    y = moe_experts(xs, block_expert, n_used, wgu, bgu, wd, bd, e_off)
    yk = y[dest.reshape(-1)].reshape(TOP_K, n, d)
    return moe_combine(yk, r_g, x2, mod16, l, n_lat, gate_row)


def _combine_kernel(y_ref, g_ref, r_ref, mod_ref, o_ref, *, tiles_per_batch, n_lat, gate_row):
    is_ctx = _row_is_ctx(pl.program_id(0), r_ref.shape[0], tiles_per_batch, n_lat)
    gate = _mod_row(mod_ref[0], is_ctx, gate_row)
    g = g_ref[...]
    acc = y_ref[0].astype(F32) * g[:, 0:1]
    for kk in range(1, TOP_K):
        acc = acc + y_ref[kk].astype(F32) * g[:, kk:kk + 1]
    o_ref[...] = r_ref[...] + gate * acc


def moe_combine(yk, r_g, res, mod16, l, n_lat, gate_row):
    _, n, d = yk.shape
    tm = _pick_rows(l)
    tpb = l // tm
    return pl.pallas_call(
        functools.partial(_combine_kernel, tiles_per_batch=tpb, n_lat=n_lat, gate_row=gate_row),
        grid=(n // tm,),
        in_specs=[pl.BlockSpec((TOP_K, tm, d), lambda i: (0, i, 0)),
                  pl.BlockSpec((tm, LANES), lambda i: (i, 0)),
                  pl.BlockSpec((tm, d), lambda i: (i, 0)),
                  pl.BlockSpec((1, MOD_ROWS, d), lambda i: (i // tpb, 0, 0))],
        out_specs=pl.BlockSpec((tm, d), lambda i: (i, 0)),
        out_shape=jax.ShapeDtypeStruct((n, d), F32),
        compiler_params=_cparams(("parallel",)),
        name="moe_combine",
    )(yk, r_g, res, mod16)


def _axial_rope(n_tok):
    pos = jnp.arange(n_tok)
    rows = (pos // GRID_W).astype(F32)
    cols = (pos % GRID_W).astype(F32)
    quarter = HEAD_DIM // 4
    inv_freq = ROPE_BASE ** (-jnp.arange(quarter, dtype=F32) / quarter)
    ar = rows[:, None] * inv_freq
    ac = cols[:, None] * inv_freq
    ang = jnp.concatenate([ar, ar, ac, ac], axis=-1)
    return jnp.cos(ang), jnp.sin(ang)


def _pick_tn(n):
    for tn in (1536, 1280, 1024, 768, 512, 256, 128):
        if n % tn == 0:
            return tn
    raise ValueError(n)


def _even_mixer(xa, mix_g, mod16, n_lat, w_in, w_out, conv_w, conv_b, a_log, dt_bias, d_skip, norm_g, sc_conv_w):
    b, l, d = xa.shape
    gn = SSD_GROUPS * SSD_STATE
    conv_dim = d + 2 * gn
    n_in = w_in.shape[1]
    n_pad = -(-n_in // 256) * 256
    o_dt = d + conv_dim
    w_perm = jnp.concatenate([w_in[:, :o_dt], w_in[:, o_dt + 2 * SSD_HEADS:], w_in[:, o_dt:o_dt + 2 * SSD_HEADS],
                              jnp.zeros((d, n_pad - n_in), w_in.dtype)], axis=1).astype(BF16)
    proj2 = proj_modulated(xa.reshape(b * l, d), mix_g, mod16, w_perm, l, n_lat, 0, 1, _pick_tn(n_pad))
    dtr = proj2[:, n_in - 2 * SSD_HEADS:n_in]
    dt = jax.nn.softplus(dtr.reshape(b, l, 2, SSD_HEADS) + dt_bias)
    a_neg = -jnp.exp(a_log)
    da = jnp.concatenate([dt, dt * a_neg], axis=-1)
    da = jnp.moveaxis(da, 2, 1)
    dat = jnp.swapaxes(da, 2, 3)
    xs_bf, bm, cm, sc_bf = even_prep(proj2, conv_w, conv_b, sc_conv_w, l, n_lat, d)
    xs_bf = xs_bf.reshape(b, l, d)
    bm = bm.reshape(b, l, gn)
    cm = cm.reshape(b, l, gn)
    bt = jnp.swapaxes(bm, 1, 2)
    yf = ssd_scan(xs_bf, bt, bm, cm, da, dat, n_lat, rev=False)
    post = (yf, proj2, jnp.repeat(d_skip, HEAD_DIM).reshape(1, d), norm_g.reshape(1, d))
    y_bf = ssd_scan(xs_bf, bt, bm, cm, da, dat, n_lat, rev=True, post=post)
    return matmul_residual([y_bf.reshape(b * l, d), sc_bf], w_out.astype(BF16), xa.reshape(b * l, d), mod16,
                           l, n_lat, 2).reshape(b, l, d)


def _odd_mixer(xa, mix_g, mod16, n_lat, w_in, w_out, na_qk_g, na_rel_bias, df_qk_g, df_lambda, df_subln_g,
               lambda_init, rope_tabs):
    b, l, d = xa.shape
    d_na = NA_HEADS * HEAD_DIM
    dq_w = DF_HEADS * 2 * HEAD_DIM
    n_ctx = l - n_lat
    reps = ODD_TN // HEAD_DIM
    ones = jnp.ones((ODD_TN,), F32)
    gains6 = jnp.stack([jnp.tile(na_qk_g[0], reps), jnp.tile(na_qk_g[1], reps), ones,
                        jnp.tile(df_qk_g[0], reps), jnp.tile(df_qk_g[1], reps), ones])[:, None, :]
    qkv = proj_odd(xa.reshape(b * l, d), mix_g, mod16, w_in.astype(BF16), gains6, *rope_tabs, l, n_lat, 0, 1)
    qkv = qkv.reshape(b, l, -1)
    lam_p = df_lambda.astype(F32)
    na_slabs = d_na // LANES
    df_slabs = dq_w // LANES
    subln = df_subln_g.reshape(1, LANES).astype(F32)
    na_lat = na_attention(qkv, na_bias_table(na_rel_bias), n_lat)
    na_ctx = flash_slabs(qkv, lam_p, subln, "pair", n_ctx, n_ctx, n_ctx, n_ctx, n_lat // n_ctx, n_lat // n_ctx,
                         0, na_slabs, 2 * na_slabs, na_slabs)
    tk = next(t for t in (1408, 1280, 1024, 768, 512, 256) if l % t == 0)
    dq0 = 3 * na_slabs
    df_lat = flash_slabs(qkv, lam_p, subln, "diff", 256, tk, n_lat, l, 0, 0, dq0, dq0 + df_slabs,
                         dq0 + 2 * df_slabs, df_slabs, lambda_init)
    df_ctx = flash_slabs(qkv, lam_p, subln, "diff", n_ctx, n_ctx, n_ctx, n_ctx, n_lat // n_ctx, n_lat // n_ctx,
                         dq0, dq0 + df_slabs, dq0 + 2 * df_slabs, df_slabs, lambda_init)
    na_o = jnp.concatenate([na_lat, na_ctx], axis=1)
    df_o = jnp.concatenate([df_lat, df_ctx], axis=1)
    parts = [na_o.reshape(b * l, d_na), df_o.reshape(b * l, dq_w)]
    return matmul_residual(parts, w_out.astype(BF16), xa.reshape(b * l, d), mod16, l, n_lat, 2).reshape(b, l, d)


def kernel(x, c, ctx, c_ctx, ada_w, ada_b, mix_norm_g, ffn_norm_g, router_w, router_b, moe_w_gu, moe_b_gu,
           moe_w_down, moe_b_down, ev_w_in, ev_w_out, ssd_conv_w, ssd_conv_b, ssd_a_log, ssd_dt_bias, ssd_d,
           ssd_norm_g, sc_conv_w, od_w_in, od_w_out, na_qk_g, na_rel_bias, df_qk_g, df_lambda, df_subln_g):
    b, s, d = x.shape
    n_ctx = ctx.shape[1]
    l = s + n_ctx
    depth = ada_w.shape[0]
    de = moe_w_down.shape[2]
    ne = moe_w_gu.shape[1]
    xa = jnp.concatenate([x, ctx], axis=1)
    cos_l, sin_l = _axial_rope(s)
    cos_t = jnp.concatenate([cos_l, jnp.ones((n_ctx, HEAD_DIM), F32)], axis=0)
    sin_t = jnp.concatenate([sin_l, jnp.zeros((n_ctx, HEAD_DIM), F32)], axis=0)
    even_q = ((jnp.arange(HEAD_DIM) // (HEAD_DIM // 4)) % 2 == 0)[None, :]
    rope_tabs = tuple(jnp.tile(t, (1, LANES // HEAD_DIM))
                      for t in (cos_t, jnp.where(even_q, -sin_t, 0.0), jnp.where(even_q, 0.0, sin_t)))
    cond = jnp.concatenate([jax.nn.silu(c), jax.nn.silu(c_ctx)[None, :]], axis=0)
    cond_pad = jnp.zeros((16, d), F32).at[:b + 1].set(cond).astype(BF16)
    wgu_all = moe_w_gu.reshape(depth * ne, d, 2 * de)
    bgu_all = moe_b_gu.reshape(depth * ne, 2 * de // GU_BLK, LANES, 2).swapaxes(-1, -2).reshape(depth * ne, 1, 2 * de)
    wd_all = moe_w_down.reshape(depth * ne, de, d)
    bd_all = moe_b_down.reshape(depth * ne, 1, d)

    for i in range(depth):
        j = i // 2
        mod = matmul(cond_pad, ada_w[i].astype(BF16), 16, 6 * d // 4)[:b + 1] + ada_b[i]
        mod6 = mod.reshape(b + 1, 6, d)
        pad2 = jnp.zeros((b, CTX_ROW0 - 6, d), F32)
        mod16 = jnp.concatenate([mod6[:b], pad2, jnp.broadcast_to(mod6[b], (b, 6, d)), pad2], axis=1)
        if i % 2 == 0:
            xa = _even_mixer(xa, mix_norm_g[i], mod16, s, ev_w_in[j], ev_w_out[j], ssd_conv_w[j], ssd_conv_b[j],
                             ssd_a_log[j], ssd_dt_bias[j], ssd_d[j], ssd_norm_g[j], sc_conv_w[j])
        else:
            lambda_init = 0.8 - 0.6 * math.exp(-0.3 * i)
            xa = _odd_mixer(xa, mix_norm_g[i], mod16, s, od_w_in[j], od_w_out[j], na_qk_g[j], na_rel_bias[j],
                            df_qk_g[j], df_lambda[j], df_subln_g[j], lambda_init, rope_tabs)
        xa = moe_ffn(xa.reshape(b * l, d), ffn_norm_g[i], mod16, l, s, router_w[i], router_b[i],
                     wgu_all, bgu_all, wd_all, bd_all, i * ne).reshape(b, l, d)
    return xa[:, :s]
```
